```python
import math
import jax
import jax.numpy as jnp
from jax import lax
import numpy as np

D_MODEL = 1024
BATCH = 4
SEQ = 4096
DEPTH = 2
DEC_BATCH = 32
DEC_SEQ = 4
PAST_LEN = 8192
PAGE_SIZE = 128

RMS_EPS = 1e-6

POOL_WINDOWS = (2, 4, 8, 16)
POOL_GROUPS = 4
POOL_GROUP = D_MODEL // 8
POOL_WIDTH = POOL_GROUPS * POOL_GROUP
POOL_HIST = 15

GLA_HEADS = 4
GLA_DK = D_MODEL // 16
GLA_DV = D_MODEL // 8
GLA_KW = GLA_HEADS * GLA_DK
GLA_VW = GLA_HEADS * GLA_DV
GLA_GATE_RANK = 16
GLA_TAU = 16.0
GLA_CHUNK = 64

ATT_GROUPS = ((128, 1), (512, 4), (2048, 16))
ATT_N_GROUPS = 3
ATT_HPG = 8
ATT_HEAD_DIM = D_MODEL // 16
ATT_HEADS = ATT_N_GROUPS * ATT_HPG
ATT_W = ATT_HEADS * ATT_HEAD_DIM
ATT_OUT = ATT_HPG * ATT_HEAD_DIM
ATT_QBLOCK = 128
REL_BUCKETS = 32
REL_MAX_DIST = 2048

PEER_HEADS = 8
PEER_NKEYS = 128
PEER_EXPERTS = PEER_NKEYS * PEER_NKEYS
PEER_QDIM = 256
PEER_HALF = PEER_QDIM // 2
PEER_TOPK = 16
PEER_BLOCK = 256

N_BRANCH = 3
BRANCH_W = D_MODEL // 2
IN_SIZES = (POOL_WIDTH, GLA_KW, GLA_KW, GLA_VW, GLA_VW, GLA_GATE_RANK,
            ATT_W, ATT_W, ATT_W, N_BRANCH * D_MODEL)
IN_WIDTH = sum(IN_SIZES)

kernel_name = 'hybrid_pool_gla_dilated_peer_step'


def rmsnorm(x, g):
    xf = x.astype(jnp.float32)
    y = xf * lax.rsqrt(jnp.mean(xf * xf, axis=-1, keepdims=True) + RMS_EPS)
    return (y * g.astype(jnp.float32)).astype(x.dtype)


def rel_bucket(dist):
    d = np.asarray(dist, dtype=np.int64)
    max_exact = REL_BUCKETS // 2
    ratio = np.log(np.maximum(d, 1) / max_exact) / np.log(REL_MAX_DIST / max_exact)
    large = np.minimum(max_exact + (ratio * (REL_BUCKETS - max_exact)).astype(np.int64), REL_BUCKETS - 1)
    return np.where(d < max_exact, d, large).astype(np.int32)


def pool_mix(u, hist, pos0, w_pool, s_pool):
    B, T, _ = u.shape
    ext = jnp.concatenate([hist.astype(u.dtype), u], axis=1)
    cs = jnp.cumsum(ext.astype(jnp.float32), axis=1)
    cs = jnp.concatenate([jnp.zeros((B, 1, POOL_WIDTH), jnp.float32), cs], axis=1)
    pos = (pos0 + jnp.arange(T)).astype(jnp.float32)
    means = []
    for gi, w in enumerate(POOL_WINDOWS):
        c = cs[:, :, gi * POOL_GROUP:(gi + 1) * POOL_GROUP]
        s = c[:, POOL_HIST + 1:POOL_HIST + 1 + T] - c[:, POOL_HIST + 1 - w:POOL_HIST + 1 - w + T]
        cnt = jnp.minimum(float(w), pos + 1.0)
        means.append(s / cnt[None, :, None])
    pooled = jnp.stack(means, axis=2)
    d = pooled - u.astype(jnp.float32).reshape(B, T, POOL_GROUPS, POOL_GROUP)
    y = jnp.einsum('btgc,gce->btge', d, w_pool.astype(jnp.float32))
    y = y * s_pool.reshape(POOL_GROUPS, POOL_GROUP).astype(jnp.float32)
    return y.reshape(B, T, POOL_WIDTH).astype(u.dtype), ext[:, -POOL_HIST:]


def gla_scan(q, k, v, log_a, s0):
    B, T, H, _ = q.shape
    C = math.gcd(T, GLA_CHUNK)
    n = T // C

    def chunks(a):
        a = a.astype(jnp.float32)
        return a.reshape(B, n, C, H, a.shape[-1]).transpose(1, 0, 2, 3, 4)

    causal = jnp.tril(jnp.ones((C, C), dtype=bool))[None, :, :, None, None]

    def step(s, inp):
        qc, kc, vc, gc = inp
        b = jnp.cumsum(gc, axis=1)
        o_inter = jnp.einsum('bchk,bhkv->bchv', qc * jnp.exp(b), s)
        decay = jnp.exp(jnp.where(causal, b[:, :, None] - b[:, None, :], -jnp.inf))
        att = jnp.einsum('bihk,bjhk,bijhk->bhij', qc, kc, decay)
        o_intra = jnp.einsum('bhij,bjhv->bihv', att, vc)
        b_last = b[:, -1]
        s_new = jnp.exp(b_last)[..., None] * s + jnp.einsum(
            'bchk,bchv->bhkv', kc * jnp.exp(b_last[:, None] - b), vc)
        return s_new, o_inter + o_intra

    s_fin, o = lax.scan(step, s0.astype(jnp.float32),
                        (chunks(q) * (GLA_DK ** -0.5), chunks(k), chunks(v), chunks(log_a)))
    return o.transpose(1, 0, 2, 3, 4).reshape(B, T, H, v.shape[-1]), s_fin


def dilated_attention(q, k_ext, v_ext, hist_lens, rel_table):
    B, T = q.shape[0], q.shape[1]
    QB = math.gcd(T, ATT_QBLOCK)
    nb = T // QB
    scale = ATT_HEAD_DIM ** -0.5
    biases, offsets = [], []
    for gi, (w, d) in enumerate(ATT_GROUPS):
        j = np.arange(w // d + 1)
        biases.append(rel_table[rel_bucket(j * d)][:, gi * ATT_HPG:(gi + 1) * ATT_HPG].T.astype(jnp.float32))
        offsets.append(jnp.asarray(j * d, dtype=jnp.int32))

    def block(qs):
        qb = lax.dynamic_slice_in_dim(q, qs, QB, axis=1)
        lses, outs = [], []
        for gi in range(ATT_N_GROUPS):
            nk = offsets[gi].shape[0]
            idx = hist_lens[gi] + qs + jnp.arange(QB)[:, None] - offsets[gi][None, :]
            valid = idx >= 0
            idx = jnp.maximum(idx, 0).reshape(-1)
            kg = jnp.take(k_ext[gi], idx, axis=1).reshape(B, QB, nk, ATT_HPG, ATT_HEAD_DIM)
            vg = jnp.take(v_ext[gi], idx, axis=1).reshape(B, QB, nk, ATT_HPG, ATT_HEAD_DIM)
            qg = qb[:, :, gi * ATT_HPG:(gi + 1) * ATT_HPG]
            s = jnp.einsum('bqhc,bqjhc->bhqj', qg, kg, preferred_element_type=jnp.float32) * scale
            s = jnp.where(valid[None, None], s + biases[gi][None, :, None, :], -jnp.inf)
            lse = jax.nn.logsumexp(s, axis=-1)
            p = jnp.exp(s - lse[..., None])
            outs.append(jnp.einsum('bhqj,bqjhc->bqhc', p, vg.astype(jnp.float32)))
            lses.append(lse)
        wts = jax.nn.softmax(jnp.stack(lses, axis=0), axis=0)
        wts = wts.transpose(0, 1, 3, 2)[..., None]
        return jnp.sum(wts * jnp.stack(outs, axis=0), axis=0)

    out = lax.map(block, jnp.arange(nb, dtype=jnp.int32) * QB)
    return out.transpose(1, 0, 2, 3, 4).reshape(B, T, ATT_OUT)


def peer_ffn(h, wq, k1, k2, u_tab, v_tab):
    B, T, D = h.shape
    n_tok = B * T
    nb = -(-n_tok // PEER_BLOCK)
    hp = jnp.pad(h.reshape(n_tok, D), ((0, nb * PEER_BLOCK - n_tok), (0, 0))).reshape(nb, PEER_BLOCK, D)

    def block(hb):
        q = jnp.einsum('nd,de->ne', hb, wq).astype(jnp.float32).reshape(PEER_BLOCK, PEER_HEADS, 2, PEER_HALF)
        s1 = jnp.einsum('nhc,hkc->nhk', q[:, :, 0], k1.astype(jnp.float32))
        s2 = jnp.einsum('nhc,hkc->nhk', q[:, :, 1], k2.astype(jnp.float32))
        v1, i1 = lax.top_k(s1, PEER_TOPK)
        v2, i2 = lax.top_k(s2, PEER_TOPK)
        cand = (v1[..., :, None] + v2[..., None, :]).reshape(PEER_BLOCK, PEER_HEADS, PEER_TOPK * PEER_TOPK)
        sc, ci = lax.top_k(cand, PEER_TOPK)
        e = (jnp.take_along_axis(i1, ci // PEER_TOPK, axis=-1) * PEER_NKEYS
             + jnp.take_along_axis(i2, ci % PEER_TOPK, axis=-1))
        g = jax.nn.softmax(sc, axis=-1)
        act = jax.nn.gelu(jnp.einsum('nd,nhkd->nhk', hb, u_tab[e]).astype(jnp.float32), approximate=False)
        return jnp.einsum('nhk,nhkd->nd', (g * act).astype(v_tab.dtype), v_tab[e])

    out = lax.map(block, hp).reshape(nb * PEER_BLOCK, D)[:n_tok]
    return out.reshape(B, T, D).astype(h.dtype)


def trunk_layer(x, pos0, pool_hist, gla_s0, k_hist, v_hist, rel_table,
                norm1_g, w_in, w_a2, b_a, w_pool, s_pool, gla_norm_g, w_branch, w_out,
                norm2_g, peer_wq, peer_k1, peer_k2, peer_u, peer_v):
    B, T, _ = x.shape
    h = rmsnorm(x, norm1_g)
    z = jnp.einsum('btd,de->bte', h, w_in)
    offs = np.cumsum(IN_SIZES)[:-1].tolist()
    u_pool, gq, gk, gv, gr, ga, aq, ak, av, gz = jnp.split(z, offs, axis=-1)
    y_pool, pool_tail = pool_mix(u_pool, pool_hist, pos0, w_pool, s_pool)
    log_a = jax.nn.log_sigmoid((jnp.einsum('btr,rk->btk', ga, w_a2) + b_a).astype(jnp.float32)) / GLA_TAU
    o, gla_s = gla_scan(gq.reshape(B, T, GLA_HEADS, GLA_DK), gk.reshape(B, T, GLA_HEADS, GLA_DK),
                        gv.reshape(B, T, GLA_HEADS, GLA_DV), log_a.reshape(B, T, GLA_HEADS, GLA_DK), gla_s0)
    o = o * lax.rsqrt(jnp.mean(o * o, axis=-1, keepdims=True) + RMS_EPS) * gla_norm_g.astype(jnp.float32)
    y_gla = (o.reshape(B, T, GLA_VW) * jax.nn.silu(gr.astype(jnp.float32))).astype(x.dtype)
    q = aq.reshape(B, T, ATT_HEADS, ATT_HEAD_DIM)
    k = ak.reshape(B, T, ATT_N_GROUPS, ATT_HPG, ATT_HEAD_DIM)
    v = av.reshape(B, T, ATT_N_GROUPS, ATT_HPG, ATT_HEAD_DIM)
    k_ext = [jnp.concatenate([k_hist[g].astype(k.dtype), k[:, :, g]], axis=1) for g in range(ATT_N_GROUPS)]
    v_ext = [jnp.concatenate([v_hist[g].astype(v.dtype), v[:, :, g]], axis=1) for g in range(ATT_N_GROUPS)]
    hist_lens = [k_hist[g].shape[1] for g in range(ATT_N_GROUPS)]
    y_att = dilated_attention(q, k_ext, v_ext, hist_lens, rel_table).astype(x.dtype)
    ys = jnp.stack([y_pool, y_gla, y_att], axis=2)
    proj = jnp.einsum('btrw,rwd->btrd', ys, w_branch)
    gates = jax.nn.sigmoid(gz.reshape(B, T, N_BRANCH, D_MODEL))
    x = x + jnp.einsum('btd,de->bte', jnp.sum(gates * proj, axis=2), w_out)
    x = x + peer_ffn(rmsnorm(x, norm2_g), peer_wq, peer_k1, peer_k2, peer_u, peer_v)
    kv_new = [jnp.stack([k[:, :, g], v[:, :, g]], axis=2) for g in range(ATT_N_GROUPS)]
    return x, pool_tail, gla_s, kv_new


def setup_inputs(seed: int = 0) -> dict:
    key = jax.random.key(seed)
    ks = jax.random.split(key, 24)
    f32 = jnp.float32

    def nrm(k, shape, scale):
        return jax.random.normal(k, shape, f32) * scale

    lens = [min(w, PAST_LEN) for (w, _) in ATT_GROUPS]
    return {
        'x_prompt': nrm(ks[0], (BATCH, SEQ, D_MODEL), 1.0),
        'x_sample': nrm(ks[1], (DEC_BATCH, DEC_SEQ, D_MODEL), 1.0),
        'state_pool': nrm(ks[2], (DEPTH, DEC_BATCH, POOL_HIST, POOL_WIDTH), 1.0),
        'state_gla': nrm(ks[3], (DEPTH, DEC_BATCH, GLA_HEADS, GLA_DK, GLA_DV), 1.0),
        'cache_att1': nrm(ks[4], (DEPTH, DEC_BATCH, lens[0], 2, ATT_HPG, ATT_HEAD_DIM), 1.0),
        'cache_att2': nrm(ks[5], (DEPTH, DEC_BATCH, lens[1], 2, ATT_HPG, ATT_HEAD_DIM), 1.0),
        'cache_att3': nrm(ks[6], (DEPTH, DEC_BATCH, lens[2], 2, ATT_HPG, ATT_HEAD_DIM), 1.0),
        'rel_table': nrm(ks[7], (REL_BUCKETS, ATT_HEADS), 0.5),
        'norm1_g': 1.0 + nrm(ks[8], (DEPTH, D_MODEL), 0.02),
        'w_in': nrm(ks[9], (DEPTH, D_MODEL, IN_WIDTH), D_MODEL ** -0.5),
        'w_a2': nrm(ks[10], (DEPTH, GLA_GATE_RANK, GLA_KW), GLA_GATE_RANK ** -0.5),
        'b_a': nrm(ks[11], (DEPTH, GLA_KW), 0.1),
        'w_pool': nrm(ks[12], (DEPTH, POOL_GROUPS, POOL_GROUP, POOL_GROUP), POOL_GROUP ** -0.5),
        's_pool': 1.0 + nrm(ks[13], (DEPTH, POOL_WIDTH), 0.02),
        'gla_norm_g': 1.0 + nrm(ks[14], (DEPTH, GLA_DV), 0.02),
        'w_branch': nrm(ks[15], (DEPTH, N_BRANCH, BRANCH_W, D_MODEL), BRANCH_W ** -0.5),
        'w_out': nrm(ks[16], (DEPTH, D_MODEL, D_MODEL), D_MODEL ** -0.5),
        'norm2_g': 1.0 + nrm(ks[17], (DEPTH, D_MODEL), 0.02),
        'peer_wq': nrm(ks[18], (DEPTH, D_MODEL, PEER_HEADS * PEER_QDIM), D_MODEL ** -0.5),
        'peer_k1': nrm(ks[19], (DEPTH, PEER_HEADS, PEER_NKEYS, PEER_HALF), PEER_HALF ** -0.5),
        'peer_k2': nrm(ks[20], (DEPTH, PEER_HEADS, PEER_NKEYS, PEER_HALF), PEER_HALF ** -0.5),
        'peer_u': nrm(ks[21], (DEPTH, PEER_EXPERTS, D_MODEL), D_MODEL ** -0.5),
        'peer_v': nrm(ks[22], (DEPTH, PEER_EXPERTS, D_MODEL), 0.3),
        'final_norm_g': 1.0 + nrm(ks[23], (D_MODEL,), 0.02),
    }


def reference(x_prompt, x_sample, state_pool, state_gla, cache_att1, cache_att2, cache_att3,
              rel_table, norm1_g, w_in, w_a2, b_a, w_pool, s_pool, gla_norm_g, w_branch, w_out,
              norm2_g, peer_wq, peer_k1, peer_k2, peer_u, peer_v, final_norm_g):
    caches = (cache_att1, cache_att2, cache_att3)
    bp, tp = x_prompt.shape[0], x_prompt.shape[1]
    xp, xs = x_prompt, x_sample
    pool_p, pool_s, gla_p, gla_s = [], [], [], []
    att_p = [[] for _ in range(ATT_N_GROUPS)]
    att_s = [[] for _ in range(ATT_N_GROUPS)]
    for l in range(DEPTH):
        lw = (rel_table, norm1_g[l], w_in[l], w_a2[l], b_a[l], w_pool[l], s_pool[l], gla_norm_g[l],
              w_branch[l], w_out[l], norm2_g[l], peer_wq[l], peer_k1[l], peer_k2[l], peer_u[l], peer_v[l])
        empty = jnp.zeros((bp, 0, ATT_HPG, ATT_HEAD_DIM), xp.dtype)
        xp, pt, gp, kvp = trunk_layer(
            xp, 0, jnp.zeros((bp, POOL_HIST, POOL_WIDTH), xp.dtype),
            jnp.zeros((bp, GLA_HEADS, GLA_DK, GLA_DV), jnp.float32),
            [empty] * ATT_N_GROUPS, [empty] * ATT_N_GROUPS, *lw)
        xs, ps, gs, kvs = trunk_layer(
            xs, PAST_LEN, state_pool[l], state_gla[l],
            [c[l][:, :, 0] for c in caches], [c[l][:, :, 1] for c in caches], *lw)
        pool_p.append(pt)
        pool_s.append(ps)
        gla_p.append(gp)
        gla_s.append(gs)
        for g, (w, _) in enumerate(ATT_GROUPS):
            att_p[g].append(kvp[g][:, tp - min(w, tp):])
            att_s[g].append(kvs[g])
    y_prompt = rmsnorm(xp, final_norm_g)
    y_sample = rmsnorm(xs, final_norm_g)
    state_pool_prompt = jnp.stack(pool_p)
    state_pool_sample = jnp.stack(pool_s)
    state_gla_prompt = jnp.stack(gla_p)
    state_gla_sample = jnp.stack(gla_s)
    cache_att1_prompt = jnp.stack(att_p[0])
    cache_att1_sample = jnp.stack(att_s[0])
    cache_att2_prompt = jnp.stack(att_p[1])
    cache_att2_sample = jnp.stack(att_s[1])
    cache_att3_prompt = jnp.stack(att_p[2])
    cache_att3_sample = jnp.stack(att_s[2])
    return (y_prompt, y_sample, state_pool_prompt, state_pool_sample, state_gla_prompt, state_gla_sample,
            cache_att1_prompt, cache_att1_sample, cache_att2_prompt, cache_att2_sample,
            cache_att3_prompt, cache_att3_sample)
```

```python
import functools
import math

import numpy as np
import jax
import jax.numpy as jnp
from jax import lax
from jax.experimental import pallas as pl
from jax.experimental.pallas import tpu as pltpu

D_MODEL = 1024
RMS_EPS = 1e-6
PAST_LEN = 8192

POOL_WINDOWS = (2, 4, 8, 16)
POOL_GROUP = 128
POOL_WIDTH = 512
POOL_HIST = 15
POOL_HALO = 16

GLA_HEADS = 4
GLA_DK = 64
GLA_DV = 128
GLA_KW = 256
GLA_VW = 512
GLA_GATE_RANK = 16
GLA_TAU = 16.0
GLA_CHUNK = 64

ATT_GROUPS = ((128, 1), (512, 4), (2048, 16))
ATT_HPG = 8
ATT_HEAD_DIM = 64
ATT_GW = ATT_HPG * ATT_HEAD_DIM
ATT_BAND = 128
REL_BUCKETS = 32
REL_MAX_DIST = 2048
NEG = -1e30

PEER_HEADS = 8
PEER_NKEYS = 128
PEER_EXPERTS = PEER_NKEYS * PEER_NKEYS
PEER_HALF = 128
PEER_TOPK = 16
PEER_ROWS = 8

Z_GZ = 0
Z_KV = 3072
Z_Q = 6144
Z_POOL = 7680
Z_GV = 8192
Z_GR = 8704
Z_GQ = 9216
Z_GK = 9472
Z_GA = 9728
Z_WIDTH = 10240

VMEM_LIMIT = 48 * 1024 * 1024


def _cparams(*sem):
    return pltpu.CompilerParams(dimension_semantics=sem, vmem_limit_bytes=VMEM_LIMIT)


def _dot_nt(a, b):
    return lax.dot_general(a, b, (((1,), (1,)), ((), ())), preferred_element_type=jnp.float32)


def _dot_tn(a, b):
    return lax.dot_general(a, b, (((0,), (0,)), ((), ())), preferred_element_type=jnp.float32)


def _rmsnorm_kernel(x_ref, g_ref, o_ref):
    x = x_ref[...]
    ms = jnp.mean(x * x, axis=-1, keepdims=True)
    o_ref[...] = (x * lax.rsqrt(ms + RMS_EPS) * g_ref[...]).astype(o_ref.dtype)


def _rmsnorm(x, g, out_dtype, tm):
    n, d = x.shape
    return pl.pallas_call(
        _rmsnorm_kernel,
        grid=(n // tm,),
        in_specs=[pl.BlockSpec((tm, d), lambda i: (i, 0)),
                  pl.BlockSpec((1, d), lambda i: (0, 0))],
        out_specs=pl.BlockSpec((tm, d), lambda i: (i, 0)),
        out_shape=jax.ShapeDtypeStruct((n, d), out_dtype),
        compiler_params=_cparams("parallel"),
        name="rmsnorm",
    )(x, g.reshape(1, d))


def _mm_kernel(a_ref, b_ref, o_ref):
    o_ref[...] = jnp.dot(a_ref[...], b_ref[...], preferred_element_type=jnp.float32)


def _matmul(a, b, tm, tn, name):
    m, k = a.shape
    n = b.shape[1]
    return pl.pallas_call(
        _mm_kernel,
        grid=(n // tn, m // tm),
        in_specs=[pl.BlockSpec((tm, k), lambda j, i: (i, 0)),
                  pl.BlockSpec((k, tn), lambda j, i: (0, j))],
        out_specs=pl.BlockSpec((tm, tn), lambda j, i: (i, j)),
        out_shape=jax.ShapeDtypeStruct((m, n), jnp.float32),
        compiler_params=_cparams("parallel", "parallel"),
        name=name,
    )(a, b)


def _pool_kernel(u_ref, prev_ref, hist_ref, w_ref, s_ref, o_ref, ext_ref, *, tt, pos0):
    t = pl.program_id(1)
    ext_ref[0:POOL_HALO, :] = jnp.where(t == 0, hist_ref[...], prev_ref[...])
    u = u_ref[...]
    ext_ref[POOL_HALO:, :] = u
    pos = (pos0 + t * tt + lax.broadcasted_iota(jnp.int32, (tt, 1), 0)).astype(jnp.float32)
    for gi, w in enumerate(POOL_WINDOWS):
        lo, hi = gi * POOL_GROUP, (gi + 1) * POOL_GROUP
        s = ext_ref[POOL_HALO:POOL_HALO + tt, lo:hi]
        for back in range(1, w):
            s = s + ext_ref[POOL_HALO - back:POOL_HALO - back + tt, lo:hi]
        cnt = jnp.minimum(float(w), pos + 1.0)
        d = s / cnt - u[:, lo:hi]
        y = jnp.dot(d.astype(jnp.bfloat16), w_ref[gi], preferred_element_type=jnp.float32)
        o_ref[:, lo:hi] = y * s_ref[:, lo:hi]


def _pool_mix(z, hist16, w_pool, s_pool, B, T, tt, pos0):
    nt = T // tt
    cb = Z_POOL // POOL_WIDTH

    def prev_map(b, t):
        return (jnp.maximum(b * T + t * tt - POOL_HALO, 0) // POOL_HALO, cb)

    return pl.pallas_call(
        functools.partial(_pool_kernel, tt=tt, pos0=pos0),
        grid=(B, nt),
        in_specs=[pl.BlockSpec((tt, POOL_WIDTH), lambda b, t: (b * nt + t, cb)),
                  pl.BlockSpec((POOL_HALO, POOL_WIDTH), prev_map),
                  pl.BlockSpec((None, POOL_HALO, POOL_WIDTH), lambda b, t: (b, 0, 0)),
                  pl.BlockSpec((4, POOL_GROUP, POOL_GROUP), lambda b, t: (0, 0, 0)),
                  pl.BlockSpec((1, POOL_WIDTH), lambda b, t: (0, 0))],
        out_specs=pl.BlockSpec((tt, POOL_WIDTH), lambda b, t: (b * nt + t, 0)),
        out_shape=jax.ShapeDtypeStruct((B * T, POOL_WIDTH), jnp.float32),
        scratch_shapes=[pltpu.VMEM((POOL_HALO + tt, POOL_WIDTH), jnp.float32)],
        compiler_params=_cparams("parallel", "parallel"),
        name="pool_mix",
    )(z, z, hist16, w_pool.astype(jnp.bfloat16), s_pool.reshape(1, POOL_WIDTH))


def _gla_kernel(q_ref, k_ref, v_ref, r_ref, ga_ref, wa_ref, ba_ref, gn_ref, s0_ref,
                y_ref, sout_ref, st_ref, *, chunk, t_valid):
    c = pl.program_id(1)

    @pl.when(c == 0)
    def _():
        st_ref[...] = s0_ref[...]

    x = jnp.dot(ga_ref[...].astype(jnp.bfloat16), wa_ref[...],
                preferred_element_type=jnp.float32) + ba_ref[...]
    g = -(jnp.maximum(-x, 0.0) + jnp.log1p(jnp.exp(-jnp.abs(x)))) / GLA_TAU
    row = c * chunk + lax.broadcasted_iota(jnp.int32, (chunk, 1), 0)
    valid = row < t_valid
    g = jnp.where(valid, g, 0.0)
    ri = lax.broadcasted_iota(jnp.int32, (chunk, chunk), 0)
    ci = lax.broadcasted_iota(jnp.int32, (chunk, chunk), 1)
    causal = ri >= ci
    b = jnp.dot(causal.astype(jnp.float32), g, preferred_element_type=jnp.float32,
                precision=lax.Precision.HIGHEST)
    b_last = b[chunk - 1:chunk, :]
    k = k_ref[...]
    qe = (q_ref[...] * (GLA_DK ** -0.5) * jnp.exp(b)).astype(jnp.bfloat16)
    ke = (k * jnp.exp(-b)).astype(jnp.bfloat16)
    kl = jnp.where(valid, k * jnp.exp(b_last - b), 0.0).astype(jnp.bfloat16)
    a_last = jnp.exp(b_last)
    v = v_ref[...].astype(jnp.bfloat16)
    r = r_ref[...]
    for h in range(GLA_HEADS):
        ks = slice(h * GLA_DK, (h + 1) * GLA_DK)
        vs = slice(h * GLA_DV, (h + 1) * GLA_DV)
        att = jnp.where(causal, _dot_nt(qe[:, ks], ke[:, ks]), 0.0)
        st = st_ref[h]
        o = (jnp.dot(att.astype(jnp.bfloat16), v[:, vs], preferred_element_type=jnp.float32)
             + _dot_nt(qe[:, ks], st.astype(jnp.bfloat16)))
        st_ref[h] = st * a_last[:, ks] + _dot_tn(v[:, vs], kl[:, ks])
        o = o * lax.rsqrt(jnp.mean(o * o, axis=-1, keepdims=True) + RMS_EPS) * gn_ref[...]
        rh = r[:, vs]
        y_ref[:, vs] = o * (rh / (1.0 + jnp.exp(-rh)))

    @pl.when(c == pl.num_programs(1) - 1)
    def _():
        sout_ref[...] = st_ref[...]


def _gla(z, s0t, w_a2, b_a, gnorm, B, T, chunk, t_valid):
    nc = T // chunk
    wa = jnp.zeros((128, GLA_KW), jnp.float32).at[:GLA_GATE_RANK].set(w_a2).astype(jnp.bfloat16)
    row = lambda b, c: b * nc + c
    return pl.pallas_call(
        functools.partial(_gla_kernel, chunk=chunk, t_valid=t_valid),
        grid=(B, nc),
        in_specs=[pl.BlockSpec((chunk, GLA_KW), lambda b, c: (row(b, c), Z_GQ // GLA_KW)),
                  pl.BlockSpec((chunk, GLA_KW), lambda b, c: (row(b, c), Z_GK // GLA_KW)),
                  pl.BlockSpec((chunk, GLA_VW), lambda b, c: (row(b, c), Z_GV // GLA_VW)),
                  pl.BlockSpec((chunk, GLA_VW), lambda b, c: (row(b, c), Z_GR // GLA_VW)),
                  pl.BlockSpec((chunk, 128), lambda b, c: (row(b, c), Z_GA // 128)),
                  pl.BlockSpec((128, GLA_KW), lambda b, c: (0, 0)),
                  pl.BlockSpec((1, GLA_KW), lambda b, c: (0, 0)),
                  pl.BlockSpec((1, GLA_DV), lambda b, c: (0, 0)),
                  pl.BlockSpec((None, GLA_HEADS, GLA_DV, GLA_DK), lambda b, c: (b, 0, 0, 0))],
        out_specs=[pl.BlockSpec((chunk, GLA_VW), lambda b, c: (row(b, c), 0)),
                   pl.BlockSpec((None, GLA_HEADS, GLA_DV, GLA_DK), lambda b, c: (b, 0, 0, 0))],
        out_shape=[jax.ShapeDtypeStruct((B * T, GLA_VW), jnp.float32),
                   jax.ShapeDtypeStruct((B, GLA_HEADS, GLA_DV, GLA_DK), jnp.float32)],
        scratch_shapes=[pltpu.VMEM((GLA_HEADS, GLA_DV, GLA_DK), jnp.float32)],
        compiler_params=_cparams("parallel", "arbitrary"),
        name="gla_scan",
    )(z, z, z, z, z, wa, b_a.reshape(1, GLA_KW), gnorm.reshape(1, GLA_DV), s0t)


def _att_kernel(q_ref, kp_ref, vp_ref, kc_ref, vc_ref, bias_ref, o_ref, l_ref, *, kp_rows, mask_first):
    q = (q_ref[...] * (ATT_HEAD_DIM ** -0.5)).astype(jnp.bfloat16)
    kp = kp_ref[...].astype(jnp.bfloat16)
    vp = vp_ref[...].astype(jnp.bfloat16)
    kc = kc_ref[...].astype(jnp.bfloat16)
    vc = vc_ref[...].astype(jnp.bfloat16)
    qb = q.shape[0]
    first = pl.program_id(2) == 0
    for h in range(ATT_HPG):
        sl = slice(h * ATT_HEAD_DIM, (h + 1) * ATT_HEAD_DIM)
        s1 = _dot_nt(q[:, sl], kp[:, sl]) + bias_ref[h, :, 0:kp_rows]
        if mask_first:
            s1 = jnp.where(first, NEG, s1)
        s2 = _dot_nt(q[:, sl], kc[:, sl]) + bias_ref[h, :, kp_rows:]
        m = jnp.maximum(jnp.max(s1, axis=-1, keepdims=True), jnp.max(s2, axis=-1, keepdims=True))
        p1 = jnp.exp(s1 - m)
        p2 = jnp.exp(s2 - m)
        den = jnp.sum(p1, axis=-1, keepdims=True) + jnp.sum(p2, axis=-1, keepdims=True)
        o = (jnp.dot(p1.astype(jnp.bfloat16), vp[:, sl], preferred_element_type=jnp.float32)
             + jnp.dot(p2.astype(jnp.bfloat16), vc[:, sl], preferred_element_type=jnp.float32))
        o_ref[:, sl] = o / den
        l_ref[:, sl] = jnp.broadcast_to(m + jnp.log(den), (qb, ATT_HEAD_DIM))


def _rel_bucket(dist):
    d = np.asarray(dist, dtype=np.int64)
    max_exact = REL_BUCKETS // 2
    ratio = np.log(np.maximum(d, 1) / max_exact) / np.log(REL_MAX_DIST / max_exact)
    large = np.minimum(max_exact + (ratio * (REL_BUCKETS - max_exact)).astype(np.int64), REL_BUCKETS - 1)
    return np.where(d < max_exact, d, large).astype(np.int32)


def _att_bias(rel_table, gi, qb, single_query):
    _, dil = ATT_GROUPS[gi]
    steps = np.arange(ATT_BAND + 1)
    tab = rel_table[_rel_bucket(steps * dil)][:, gi * ATT_HPG:(gi + 1) * ATT_HPG].T.astype(jnp.float32)
    qi = np.arange(qb)[:, None]
    col = np.arange(ATT_BAND + qb)[None, :]
    if single_query:
        j = np.where(col < ATT_BAND, ATT_BAND - col, np.where(col - ATT_BAND == qi, 0, -1))
    else:
        j = qi + ATT_BAND - col
    ok = (j >= 0) & (j <= ATT_BAND)
    jc = np.clip(j, 0, ATT_BAND)
    return jnp.where(jnp.asarray(ok)[None], tab[:, jc], NEG)


def _att_call(args, in_specs, grid, out_rows, out_block_map, qb, mask_first, name):
    blk = pl.BlockSpec((None, qb, ATT_GW), out_block_map)
    shape = jax.ShapeDtypeStruct(out_rows, jnp.float32)
    return pl.pallas_call(
        functools.partial(_att_kernel, kp_rows=ATT_BAND, mask_first=mask_first),
        grid=grid,
        in_specs=in_specs,
        out_specs=[blk, blk],
        out_shape=[shape, shape],
        compiler_params=_cparams("parallel", "parallel", "parallel"),
        name=name,
    )(*args)


def _att_prompt(z, rel_table, gi, B, T):
    _, dil = ATT_GROUPS[gi]
    qb = ATT_BAND
    nm = T // dil // qb
    z3 = z.reshape(B, T // dil, dil * Z_WIDTH)
    per = Z_WIDTH // ATT_GW
    cq, ck, cv = Z_Q // ATT_GW + gi, Z_KV // ATT_GW + 2 * gi, Z_KV // ATT_GW + 2 * gi + 1
    cur = lambda col: pl.BlockSpec((None, qb, ATT_GW), lambda b, r, m: (b, m, r * per + col))
    prev = lambda col: pl.BlockSpec((None, qb, ATT_GW), lambda b, r, m: (b, jnp.maximum(m - 1, 0), r * per + col))
    bias = _att_bias(rel_table, gi, qb, False)
    o, l = _att_call(
        (z3, z3, z3, z3, z3, bias),
        [cur(cq), prev(ck), prev(cv), cur(ck), cur(cv),
         pl.BlockSpec(bias.shape, lambda b, r, m: (0, 0, 0))],
        (B, dil, nm), (B, T // dil, dil * ATT_GW), lambda b, r, m: (b, m, r), qb, True,
        "att_prompt_g%d" % gi)
    return o.reshape(B * T, ATT_GW), l.reshape(B * T, ATT_GW)


def _att_sample(z, cache_l, rel_table, gi, B, T, t_valid):
    width, dil = ATT_GROUPS[gi]
    assert cache_l.shape[1] == width and width // dil == ATT_BAND
    z3 = z.reshape(B, T, Z_WIDTH)
    cq, ck, cv = Z_Q // ATT_GW + gi, Z_KV // ATT_GW + 2 * gi, Z_KV // ATT_GW + 2 * gi + 1
    cview = cache_l.reshape(B, ATT_BAND, dil * 2 * ATT_GW)
    single = dil > 1
    assert (not single) or t_valid <= dil
    nr = t_valid if single else 1
    cur = lambda col: pl.BlockSpec((None, T, ATT_GW), lambda b, r, m: (b, 0, col))
    hist = lambda kv: pl.BlockSpec((None, ATT_BAND, ATT_GW), lambda b, r, m: (b, 0, 2 * r + kv))
    bias = _att_bias(rel_table, gi, T, single)
    o, l = _att_call(
        (z3, cview, cview, z3, z3, bias),
        [cur(cq), hist(0), hist(1), cur(ck), cur(cv),
         pl.BlockSpec(bias.shape, lambda b, r, m: (0, 0, 0))],
        (B, nr, 1), (B, nr * T, ATT_GW), lambda b, r, m: (b, r, 0), T, False,
        "att_sample_g%d" % gi)
    if single:
        idx = np.arange(nr)
        pick = lambda a: jnp.pad(a.reshape(B, nr, T, ATT_GW)[:, idx, idx], ((0, 0), (0, T - nr), (0, 0)))
        o, l = pick(o), pick(l)
    return o.reshape(B * T, ATT_GW), l.reshape(B * T, ATT_GW)


def _merge_kernel(yp_ref, yg_ref, o0_ref, o1_ref, o2_ref, l0_ref, l1_ref, l2_ref, gz_ref, x_ref,
                  wb_ref, wo_ref, g2_ref, x1_ref, hn_ref):
    l0, l1, l2 = l0_ref[...], l1_ref[...], l2_ref[...]
    lm = jnp.maximum(jnp.maximum(l0, l1), l2)
    e0, e1, e2 = jnp.exp(l0 - lm), jnp.exp(l1 - lm), jnp.exp(l2 - lm)
    ya = (e0 * o0_ref[...] + e1 * o1_ref[...] + e2 * o2_ref[...]) / (e0 + e1 + e2)
    acc = None
    for bi, y in enumerate((yp_ref[...], yg_ref[...], ya)):
        proj = jnp.dot(y.astype(jnp.bfloat16), wb_ref[bi], preferred_element_type=jnp.float32)
        gz = gz_ref[:, bi * D_MODEL:(bi + 1) * D_MODEL]
        term = proj / (1.0 + jnp.exp(-gz))
        acc = term if acc is None else acc + term
    x1 = x_ref[...] + jnp.dot(acc.astype(jnp.bfloat16), wo_ref[...], preferred_element_type=jnp.float32)
    x1_ref[...] = x1
    ms = jnp.mean(x1 * x1, axis=-1, keepdims=True)
    hn_ref[...] = (x1 * lax.rsqrt(ms + RMS_EPS) * g2_ref[...]).astype(hn_ref.dtype)


def _merge(yp, yg, att, z, x, w_branch, w_out, norm2_g, tm):
    n = x.shape[0]
    half = pl.BlockSpec((tm, 512), lambda i: (i, 0))
    full = pl.BlockSpec((tm, D_MODEL), lambda i: (i, 0))
    (o0, l0), (o1, l1), (o2, l2) = att
    return pl.pallas_call(
        _merge_kernel,
        grid=(n // tm,),
        in_specs=[half] * 8 + [
            pl.BlockSpec((tm, 3 * D_MODEL), lambda i: (i, Z_GZ // (3 * D_MODEL))),
            full,
            pl.BlockSpec((3, 512, D_MODEL), lambda i: (0, 0, 0)),
            pl.BlockSpec((D_MODEL, D_MODEL), lambda i: (0, 0)),
            pl.BlockSpec((1, D_MODEL), lambda i: (0, 0))],
        out_specs=[full, full],
        out_shape=[jax.ShapeDtypeStruct((n, D_MODEL), jnp.float32),
                   jax.ShapeDtypeStruct((n, D_MODEL), jnp.bfloat16)],
        compiler_params=_cparams("parallel"),
        name="branch_merge",
    )(yp, yg, o0, o1, o2, l0, l1, l2, z, x, w_branch.astype(jnp.bfloat16),
      w_out.astype(jnp.bfloat16), norm2_g.reshape(1, D_MODEL))


def _peer_select_kernel(q_ref, k1_ref, k2_ref, c1_ref, th_ref, a2_ref, s2_ref):
    q = q_ref[...].astype(jnp.bfloat16)
    s1 = _dot_nt(k1_ref[...], q[:, :PEER_HALF])
    s2 = _dot_nt(k2_ref[...], q[:, PEER_HALF:])

    def top(vals, count):
        out, work = [], vals
        for _ in range(count):
            m = jnp.max(work, axis=0, keepdims=True)
            out.append(m)
            work = jnp.where(work >= m, -jnp.inf, work)
        return out

    v1 = top(s1, PEER_TOPK)
    v2 = top(s2, PEER_TOPK)
    rid = lax.broadcasted_iota(jnp.int32, (PEER_TOPK, s2.shape[1]), 0)
    v2all = jnp.zeros((PEER_TOPK, s2.shape[1]), jnp.float32)
    for b, vb in enumerate(v2):
        v2all = jnp.where(rid == b, vb, v2all)
    cand = jnp.concatenate([v1[0] + v2all] + [v1[a] + v2all[:8] for a in range(1, PEER_TOPK)], axis=0)
    best = top(cand, PEER_TOPK + 1)
    tau = 0.5 * (best[PEER_TOPK - 1] + best[PEER_TOPK])
    den = jnp.sum(jnp.where(cand >= tau, jnp.exp(cand - best[0]), 0.0), axis=0, keepdims=True)
    c1_ref[...] = jnp.where(s1 >= v1[-1], jnp.exp(s1 - v1[0]) / den, 0.0)
    th_ref[...] = tau - s1
    a2_ref[...] = jnp.where(s2 >= v2[-1], jnp.exp(s2 - v2[0]), 0.0)
    s2_ref[...] = s2


def _peer_select(q, k1, k2, tn):
    n = q.shape[0]
    keyspec = pl.BlockSpec((None, PEER_NKEYS, PEER_HALF), lambda i, h: (h, 0, 0))
    out = pl.BlockSpec((None, PEER_NKEYS, tn), lambda i, h: (h, 0, i))
    shape = jax.ShapeDtypeStruct((PEER_HEADS, PEER_NKEYS, n), jnp.float32)
    return pl.pallas_call(
        _peer_select_kernel,
        grid=(n // tn, PEER_HEADS),
        in_specs=[pl.BlockSpec((tn, 2 * PEER_HALF), lambda i, h: (i, h)), keyspec, keyspec],
        out_specs=[out] * 4,
        out_shape=[shape] * 4,
        compiler_params=_cparams("parallel", "parallel"),
        name="peer_select",
    )(q, k1.astype(jnp.bfloat16), k2.astype(jnp.bfloat16))


def _peer_expert_kernel(hn_ref, u_ref, vt_ref, c1_ref, th_ref, a2_ref, s2_ref, x_ref, o_ref,
                        acc_ref, p_ref):
    c = pl.program_id(1)

    @pl.when(c == 0)
    def _():
        acc_ref[...] = jnp.zeros_like(acc_ref)

    st = _dot_nt(u_ref[...], hn_ref[...])
    for ii in range(PEER_ROWS):
        rows = slice(ii * PEER_NKEYS, (ii + 1) * PEER_NKEYS)
        s = st[rows, :]
        act = 0.5 * s * (1.0 + lax.erf(s * (2.0 ** -0.5)))
        w = None
        for h in range(PEER_HEADS):
            sel = jnp.where(s2_ref[h] >= th_ref[h, ii:ii + 1, :], a2_ref[h], 0.0) * c1_ref[h, ii:ii + 1, :]
            w = sel if w is None else w + sel
        p_ref[rows, :] = (w * act).astype(jnp.bfloat16)
    acc_ref[...] += jnp.dot(vt_ref[...], p_ref[...], preferred_element_type=jnp.float32)

    @pl.when(c == pl.num_programs(1) - 1)
    def _():
        o_ref[...] = x_ref[...] + acc_ref[...].T


def _peer_experts(hn, u_bf, vt_bf, sel, x1, tn):
    n = hn.shape[0]
    ce = PEER_ROWS * PEER_NKEYS
    c1, th, a2, s2 = sel
    rowblk = pl.BlockSpec((PEER_HEADS, PEER_ROWS, tn), lambda i, c: (0, c, i))
    allblk = pl.BlockSpec((PEER_HEADS, PEER_NKEYS, tn), lambda i, c: (0, 0, i))
    tok = pl.BlockSpec((tn, D_MODEL), lambda i, c: (i, 0))
    return pl.pallas_call(
        _peer_expert_kernel,
        grid=(n // tn, PEER_EXPERTS // ce),
        in_specs=[tok,
                  pl.BlockSpec((ce, D_MODEL), lambda i, c: (c, 0)),
                  pl.BlockSpec((D_MODEL, ce), lambda i, c: (0, c)),
                  rowblk, rowblk, allblk, allblk, tok],
        out_specs=tok,
        out_shape=jax.ShapeDtypeStruct((n, D_MODEL), jnp.float32),
        scratch_shapes=[pltpu.VMEM((D_MODEL, tn), jnp.float32),
                        pltpu.VMEM((ce, tn), jnp.bfloat16)],
        compiler_params=_cparams("parallel", "arbitrary"),
        name="peer_experts",
    )(hn, u_bf, vt_bf, c1, th, a2, s2, x1)


def _arrange_w_in(w_in):
    sizes = (512, 256, 256, 512, 512, 16, 1536, 1536, 1536, 3072)
    offs = np.concatenate([[0], np.cumsum(sizes)])
    up, gq, gk, gv, gr, ga, aq, ak, av, gz = [w_in[:, offs[i]:offs[i + 1]] for i in range(10)]
    kv = []
    for g in range(3):
        kv += [ak[:, g * ATT_GW:(g + 1) * ATT_GW], av[:, g * ATT_GW:(g + 1) * ATT_GW]]
    pad = jnp.zeros((w_in.shape[0], Z_WIDTH - Z_GA - GLA_GATE_RANK), w_in.dtype)
    return jnp.concatenate([gz] + kv + [aq, up, gv, gr, gq, gk, ga, pad], axis=1).astype(jnp.bfloat16)


def _layer(x, hn, B, T, t_valid, pos0, pool_hist, gla_s0, caches_l, rel_table, lw, tiles):
    (w_in_r, w_a2, b_a, w_pool, s_pool, gla_norm_g, w_branch, w_out, norm2_g,
     wq_bf, k1, k2, u_bf, vt_bf) = lw
    tm, tt, chunk, tn = tiles
    z = _matmul(hn, w_in_r, tm, 1024, "in_proj")
    hist16 = jnp.pad(pool_hist, ((0, 0), (POOL_HALO - POOL_HIST, 0), (0, 0)))
    y_pool = _pool_mix(z, hist16, w_pool, s_pool, B, T, tt, pos0)
    y_gla, st = _gla(z, jnp.swapaxes(gla_s0, 2, 3), w_a2, b_a, gla_norm_g, B, T, chunk, t_valid)
    if caches_l is None:
        att = [_att_prompt(z, rel_table, gi, B, T) for gi in range(3)]
    else:
        att = [_att_sample(z, caches_l[gi], rel_table, gi, B, T, t_valid) for gi in range(3)]
    x1, hn2 = _merge(y_pool, y_gla, att, z, x, w_branch, w_out, norm2_g, tm)
    q = _matmul(hn2, wq_bf, tm, 1024, "peer_query")
    sel = _peer_select(q, k1, k2, tn)
    x2 = _peer_experts(hn2, u_bf, vt_bf, sel, x1, tn)
    z3 = z.reshape(B, T, Z_WIDTH)
    u_new = z3[:, :t_valid, Z_POOL:Z_POOL + POOL_WIDTH]
    pool_tail = jnp.concatenate([pool_hist, u_new], axis=1)[:, -POOL_HIST:]
    kv_new = [z3[:, :t_valid, Z_KV + 2 * ATT_GW * g:Z_KV + 2 * ATT_GW * (g + 1)]
              .reshape(B, t_valid, 2, ATT_HPG, ATT_HEAD_DIM) for g in range(3)]
    return x2, pool_tail, jnp.swapaxes(st, 2, 3), kv_new


def kernel(x_prompt, x_sample, state_pool, state_gla, cache_att1, cache_att2, cache_att3, rel_table,
           norm1_g, w_in, w_a2, b_a, w_pool, s_pool, gla_norm_g, w_branch, w_out, norm2_g,
           peer_wq, peer_k1, peer_k2, peer_u, peer_v, final_norm_g):
    bp, tp, _ = x_prompt.shape
    bs, ts, _ = x_sample.shape
    depth = w_in.shape[0]
    ts_pad = 8
    caches = (cache_att1, cache_att2, cache_att3)
    xp = x_prompt.reshape(bp * tp, D_MODEL)
    xs = jnp.pad(x_sample, ((0, 0), (0, ts_pad - ts), (0, 0))).reshape(bs * ts_pad, D_MODEL)
    tiles_p = (512, 512, GLA_CHUNK, 512)
    rows_s = min(256, bs * ts_pad)
    tiles_s = (rows_s, ts_pad, ts_pad, rows_s)
    pool_p, pool_s, gla_p, gla_s = [], [], [], []
    att_p = [[] for _ in range(3)]
    att_s = [[] for _ in range(3)]
    for l in range(depth):
        lw = (_arrange_w_in(w_in[l]), w_a2[l], b_a[l], w_pool[l], s_pool[l], gla_norm_g[l], w_branch[l],
              w_out[l], norm2_g[l], peer_wq[l].astype(jnp.bfloat16), peer_k1[l], peer_k2[l],
              peer_u[l].astype(jnp.bfloat16), peer_v[l].T.astype(jnp.bfloat16))
        hp = _rmsnorm(xp, norm1_g[l], jnp.bfloat16, tiles_p[0])
        hs = _rmsnorm(xs, norm1_g[l], jnp.bfloat16, tiles_s[0])
        xp, pt, gp, kvp = _layer(
            xp, hp, bp, tp, tp, 0, jnp.zeros((bp, POOL_HIST, POOL_WIDTH), jnp.float32),
            jnp.zeros((bp, GLA_HEADS, GLA_DK, GLA_DV), jnp.float32), None, rel_table, lw, tiles_p)
        xs, ps, gs, kvs = _layer(
            xs, hs, bs, ts_pad, ts, PAST_LEN, state_pool[l], state_gla[l], [c[l] for c in caches],
            rel_table, lw, tiles_s)
        pool_p.append(pt)
        pool_s.append(ps)
        gla_p.append(gp)
        gla_s.append(gs)
        for g, (w, _) in enumerate(ATT_GROUPS):
            att_p[g].append(kvp[g][:, tp - min(w, tp):])
            att_s[g].append(kvs[g])
    y_prompt = _rmsnorm(xp, final_norm_g, jnp.float32, tiles_p[0]).reshape(bp, tp, D_MODEL)
    y_sample = _rmsnorm(xs, final_norm_g, jnp.float32, tiles_s[0]).reshape(bs, ts_pad, D_MODEL)[:, :ts]
    outs = [y_prompt, y_sample, jnp.stack(pool_p), jnp.stack(pool_s), jnp.stack(gla_p), jnp.stack(gla_s)]
    for g in range(3):
        outs += [jnp.stack(att_p[g]), jnp.stack(att_s[g])]
    return tuple(outs)
```

```python
import functools
import math

import numpy as np
import jax
import jax.numpy as jnp
from jax import lax
from jax.experimental import pallas as pl
from jax.experimental.pallas import tpu as pltpu

D_MODEL = 1024
RMS_EPS = 1e-6
PAST_LEN = 8192

POOL_WINDOWS = (2, 4, 8, 16)
POOL_GROUP = 128
POOL_WIDTH = 512
POOL_HIST = 15
POOL_HALO = 16

GLA_HEADS = 4
GLA_DK = 64
GLA_DV = 128
GLA_KW = 256
GLA_VW = 512
GLA_GATE_RANK = 16
GLA_TAU = 16.0
GLA_CHUNK = 64

ATT_GROUPS = ((128, 1), (512, 4), (2048, 16))
ATT_HPG = 8
ATT_HEAD_DIM = 64
ATT_GW = ATT_HPG * ATT_HEAD_DIM
ATT_BAND = 128
REL_BUCKETS = 32
REL_MAX_DIST = 2048
NEG = -1e30

PEER_HEADS = 8
PEER_NKEYS = 128
PEER_EXPERTS = PEER_NKEYS * PEER_NKEYS
PEER_HALF = 128
PEER_TOPK = 16
PEER_ROWS = 8

Z_GZ = 0
Z_KV = 3072
Z_Q = 6144
Z_POOL = 7680
Z_GV = 8192
Z_GR = 8704
Z_GQ = 9216
Z_GK = 9472
Z_GA = 9728
Z_WIDTH = 10240

VMEM_LIMIT = 48 * 1024 * 1024


def _cparams(*sem):
    return pltpu.CompilerParams(dimension_semantics=sem, vmem_limit_bytes=VMEM_LIMIT)


def _dot_nt(a, b):
    return lax.dot_general(a, b, (((1,), (1,)), ((), ())), preferred_element_type=jnp.float32)


def _dot_tn(a, b):
    return lax.dot_general(a, b, (((0,), (0,)), ((), ())), preferred_element_type=jnp.float32)


def _rmsnorm_kernel(x_ref, g_ref, o_ref):
    x = x_ref[...]
    ms = jnp.mean(x * x, axis=-1, keepdims=True)
    o_ref[...] = (x * lax.rsqrt(ms + RMS_EPS) * g_ref[...]).astype(o_ref.dtype)


def _rmsnorm(x, g, out_dtype, tm):
    n, d = x.shape
    return pl.pallas_call(
        _rmsnorm_kernel,
        grid=(n // tm,),
        in_specs=[pl.BlockSpec((tm, d), lambda i: (i, 0)),
                  pl.BlockSpec((1, d), lambda i: (0, 0))],
        out_specs=pl.BlockSpec((tm, d), lambda i: (i, 0)),
        out_shape=jax.ShapeDtypeStruct((n, d), out_dtype),
        compiler_params=_cparams("parallel"),
        name="rmsnorm",
    )(x, g.reshape(1, d))


def _mm_kernel(a_ref, b_ref, o_ref):
    o_ref[...] = jnp.dot(a_ref[...], b_ref[...], preferred_element_type=jnp.float32)


def _matmul(a, b, tm, tn, name):
    m, k = a.shape
    n = b.shape[1]
    return pl.pallas_call(
        _mm_kernel,
        grid=(n // tn, m // tm),
        in_specs=[pl.BlockSpec((tm, k), lambda j, i: (i, 0)),
                  pl.BlockSpec((k, tn), lambda j, i: (0, j))],
        out_specs=pl.BlockSpec((tm, tn), lambda j, i: (i, j)),
        out_shape=jax.ShapeDtypeStruct((m, n), jnp.float32),
        compiler_params=_cparams("parallel", "parallel"),
        name=name,
    )(a, b)


def _pool_kernel(u_ref, prev_ref, hist_ref, w_ref, s_ref, o_ref, ext_ref, *, tt, pos0):
    t = pl.program_id(1)
    ext_ref[0:POOL_HALO, :] = jnp.where(t == 0, hist_ref[...], prev_ref[...])
    u = u_ref[...]
    ext_ref[POOL_HALO:, :] = u
    pos = (pos0 + t * tt + lax.broadcasted_iota(jnp.int32, (tt, 1), 0)).astype(jnp.float32)
    for gi, w in enumerate(POOL_WINDOWS):
        lo, hi = gi * POOL_GROUP, (gi + 1) * POOL_GROUP
        s = ext_ref[POOL_HALO:POOL_HALO + tt, lo:hi]
        for back in range(1, w):
            s = s + ext_ref[POOL_HALO - back:POOL_HALO - back + tt, lo:hi]
        cnt = jnp.minimum(float(w), pos + 1.0)
        d = s / cnt - u[:, lo:hi]
        y = jnp.dot(d.astype(jnp.bfloat16), w_ref[gi], preferred_element_type=jnp.float32)
        o_ref[:, lo:hi] = y * s_ref[:, lo:hi]


def _pool_mix(z, hist16, w_pool, s_pool, B, T, tt, pos0):
    nt = T // tt
    cb = Z_POOL // POOL_WIDTH

    def prev_map(b, t):
        return (jnp.maximum(b * T + t * tt - POOL_HALO, 0) // POOL_HALO, cb)

    return pl.pallas_call(
        functools.partial(_pool_kernel, tt=tt, pos0=pos0),
        grid=(B, nt),
        in_specs=[pl.BlockSpec((tt, POOL_WIDTH), lambda b, t: (b * nt + t, cb)),
                  pl.BlockSpec((POOL_HALO, POOL_WIDTH), prev_map),
                  pl.BlockSpec((None, POOL_HALO, POOL_WIDTH), lambda b, t: (b, 0, 0)),
                  pl.BlockSpec((4, POOL_GROUP, POOL_GROUP), lambda b, t: (0, 0, 0)),
                  pl.BlockSpec((1, POOL_WIDTH), lambda b, t: (0, 0))],
        out_specs=pl.BlockSpec((tt, POOL_WIDTH), lambda b, t: (b * nt + t, 0)),
        out_shape=jax.ShapeDtypeStruct((B * T, POOL_WIDTH), jnp.float32),
        scratch_shapes=[pltpu.VMEM((POOL_HALO + tt, POOL_WIDTH), jnp.float32)],
        compiler_params=_cparams("parallel", "parallel"),
        name="pool_mix",
    )(z, z, hist16, w_pool.astype(jnp.bfloat16), s_pool.reshape(1, POOL_WIDTH))


def _gla_kernel(q_ref, k_ref, v_ref, r_ref, ga_ref, wa_ref, ba_ref, gn_ref, s0_ref,
                y_ref, sout_ref, st_ref, *, chunk, t_valid):
    c = pl.program_id(1)

    @pl.when(c == 0)
    def _():
        st_ref[...] = s0_ref[...]

    x = jnp.dot(ga_ref[...].astype(jnp.bfloat16), wa_ref[...],
                preferred_element_type=jnp.float32) + ba_ref[...]
    g = -(jnp.maximum(-x, 0.0) + jnp.log1p(jnp.exp(-jnp.abs(x)))) / GLA_TAU
    row = c * chunk + lax.broadcasted_iota(jnp.int32, (chunk, 1), 0)
    valid = row < t_valid
    g = jnp.where(valid, g, 0.0)
    ri = lax.broadcasted_iota(jnp.int32, (chunk, chunk), 0)
    ci = lax.broadcasted_iota(jnp.int32, (chunk, chunk), 1)
    causal = ri >= ci
    b = jnp.dot(causal.astype(jnp.float32), g, preferred_element_type=jnp.float32,
                precision=lax.Precision.HIGHEST)
    b_last = b[chunk - 1:chunk, :]
    k = k_ref[...]
    qe = (q_ref[...] * (GLA_DK ** -0.5) * jnp.exp(b)).astype(jnp.bfloat16)
    ke = (k * jnp.exp(-b)).astype(jnp.bfloat16)
    kl = jnp.where(valid, k * jnp.exp(b_last - b), 0.0).astype(jnp.bfloat16)
    a_last = jnp.exp(b_last)
    v = v_ref[...].astype(jnp.bfloat16)
    r = r_ref[...]
    for h in range(GLA_HEADS):
        ks = slice(h * GLA_DK, (h + 1) * GLA_DK)
        vs = slice(h * GLA_DV, (h + 1) * GLA_DV)
        att = jnp.where(causal, _dot_nt(qe[:, ks], ke[:, ks]), 0.0)
        st = st_ref[h]
        o = (jnp.dot(att.astype(jnp.bfloat16), v[:, vs], preferred_element_type=jnp.float32)
             + _dot_nt(qe[:, ks], st.astype(jnp.bfloat16)))
        st_ref[h] = st * a_last[:, ks] + _dot_tn(v[:, vs], kl[:, ks])
        o = o * lax.rsqrt(jnp.mean(o * o, axis=-1, keepdims=True) + RMS_EPS) * gn_ref[...]
        rh = r[:, vs]
        y_ref[:, vs] = o * (rh / (1.0 + jnp.exp(-rh)))

    @pl.when(c == pl.num_programs(1) - 1)
    def _():
        sout_ref[...] = st_ref[...]


def _gla(z, s0t, w_a2, b_a, gnorm, B, T, chunk, t_valid):
    nc = T // chunk
    wa = jnp.zeros((128, GLA_KW), jnp.float32).at[:GLA_GATE_RANK].set(w_a2).astype(jnp.bfloat16)
    row = lambda b, c: b * nc + c
    return pl.pallas_call(
        functools.partial(_gla_kernel, chunk=chunk, t_valid=t_valid),
        grid=(B, nc),
        in_specs=[pl.BlockSpec((chunk, GLA_KW), lambda b, c: (row(b, c), Z_GQ // GLA_KW)),
                  pl.BlockSpec((chunk, GLA_KW), lambda b, c: (row(b, c), Z_GK // GLA_KW)),
                  pl.BlockSpec((chunk, GLA_VW), lambda b, c: (row(b, c), Z_GV // GLA_VW)),
                  pl.BlockSpec((chunk, GLA_VW), lambda b, c: (row(b, c), Z_GR // GLA_VW)),
                  pl.BlockSpec((chunk, 128), lambda b, c: (row(b, c), Z_GA // 128)),
                  pl.BlockSpec((128, GLA_KW), lambda b, c: (0, 0)),
                  pl.BlockSpec((1, GLA_KW), lambda b, c: (0, 0)),
                  pl.BlockSpec((1, GLA_DV), lambda b, c: (0, 0)),
                  pl.BlockSpec((None, GLA_HEADS, GLA_DV, GLA_DK), lambda b, c: (b, 0, 0, 0))],
        out_specs=[pl.BlockSpec((chunk, GLA_VW), lambda b, c: (row(b, c), 0)),
                   pl.BlockSpec((None, GLA_HEADS, GLA_DV, GLA_DK), lambda b, c: (b, 0, 0, 0))],
        out_shape=[jax.ShapeDtypeStruct((B * T, GLA_VW), jnp.float32),
                   jax.ShapeDtypeStruct((B, GLA_HEADS, GLA_DV, GLA_DK), jnp.float32)],
        scratch_shapes=[pltpu.VMEM((GLA_HEADS, GLA_DV, GLA_DK), jnp.float32)],
        compiler_params=_cparams("parallel", "arbitrary"),
        name="gla_scan",
    )(z, z, z, z, z, wa, b_a.reshape(1, GLA_KW), gnorm.reshape(1, GLA_DV), s0t)


def _att_kernel(q_ref, kp_ref, vp_ref, kc_ref, vc_ref, bias_ref, o_ref, l_ref, *, kp_rows, mask_first):
    q = (q_ref[...] * (ATT_HEAD_DIM ** -0.5)).astype(jnp.bfloat16)
    kp = kp_ref[...].astype(jnp.bfloat16)
    vp = vp_ref[...].astype(jnp.bfloat16)
    kc = kc_ref[...].astype(jnp.bfloat16)
    vc = vc_ref[...].astype(jnp.bfloat16)
    qb = q.shape[0]
    first = pl.program_id(2) == 0
    for h in range(ATT_HPG):
        sl = slice(h * ATT_HEAD_DIM, (h + 1) * ATT_HEAD_DIM)
        s1 = _dot_nt(q[:, sl], kp[:, sl]) + bias_ref[h, :, 0:kp_rows]
        if mask_first:
            s1 = jnp.where(first, NEG, s1)
        s2 = _dot_nt(q[:, sl], kc[:, sl]) + bias_ref[h, :, kp_rows:]
        m = jnp.maximum(jnp.max(s1, axis=-1, keepdims=True), jnp.max(s2, axis=-1, keepdims=True))
        p1 = jnp.exp(s1 - m)
        p2 = jnp.exp(s2 - m)
        den = jnp.sum(p1, axis=-1, keepdims=True) + jnp.sum(p2, axis=-1, keepdims=True)
        o = (jnp.dot(p1.astype(jnp.bfloat16), vp[:, sl], preferred_element_type=jnp.float32)
             + jnp.dot(p2.astype(jnp.bfloat16), vc[:, sl], preferred_element_type=jnp.float32))
        o_ref[:, sl] = o / den
        l_ref[:, sl] = jnp.broadcast_to(m + jnp.log(den), (qb, ATT_HEAD_DIM))


def _rel_bucket(dist):
    d = np.asarray(dist, dtype=np.int64)
    max_exact = REL_BUCKETS // 2
    ratio = np.log(np.maximum(d, 1) / max_exact) / np.log(REL_MAX_DIST / max_exact)
    large = np.minimum(max_exact + (ratio * (REL_BUCKETS - max_exact)).astype(np.int64), REL_BUCKETS - 1)
    return np.where(d < max_exact, d, large).astype(np.int32)


def _att_bias(rel_table, gi, qb, single_query):
    _, dil = ATT_GROUPS[gi]
    steps = np.arange(ATT_BAND + 1)
    tab = rel_table[_rel_bucket(steps * dil)][:, gi * ATT_HPG:(gi + 1) * ATT_HPG].T.astype(jnp.float32)
    qi = np.arange(qb)[:, None]
    col = np.arange(ATT_BAND + qb)[None, :]
    if single_query:
        j = np.where(col < ATT_BAND, ATT_BAND - col, np.where(col - ATT_BAND == qi, 0, -1))
    else:
        j = qi + ATT_BAND - col
    ok = (j >= 0) & (j <= ATT_BAND)
    jc = np.clip(j, 0, ATT_BAND)
    return jnp.where(jnp.asarray(ok)[None], tab[:, jc], NEG)


def _att_call(args, in_specs, grid, out_rows, out_block_map, qb, mask_first, name):
    blk = pl.BlockSpec((None, qb, ATT_GW), out_block_map)
    shape = jax.ShapeDtypeStruct(out_rows, jnp.float32)
    return pl.pallas_call(
        functools.partial(_att_kernel, kp_rows=ATT_BAND, mask_first=mask_first),
        grid=grid,
        in_specs=in_specs,
        out_specs=[blk, blk],
        out_shape=[shape, shape],
        compiler_params=_cparams("parallel", "parallel", "parallel"),
        name=name,
    )(*args)


def _att_prompt(z, rel_table, gi, B, T):
    _, dil = ATT_GROUPS[gi]
    qb = ATT_BAND
    nm = T // dil // qb
    if dil == 1:
        zq = zkv = z.reshape(B, T, Z_WIDTH)
        per_q = per_kv = Z_WIDTH // ATT_GW
        cq, ck = Z_Q // ATT_GW + gi, Z_KV // ATT_GW + 2 * gi
    else:
        zq = z[:, Z_Q + gi * ATT_GW:Z_Q + (gi + 1) * ATT_GW].reshape(B, T // dil, dil * ATT_GW)
        zkv = z[:, Z_KV + 2 * gi * ATT_GW:Z_KV + 2 * (gi + 1) * ATT_GW].reshape(B, T // dil, dil * 2 * ATT_GW)
        per_q, per_kv, cq, ck = 1, 2, 0, 0
    cur = lambda per, col: pl.BlockSpec((None, qb, ATT_GW), lambda b, r, m: (b, m, r * per + col))
    prev = lambda per, col: pl.BlockSpec((None, qb, ATT_GW),
                                         lambda b, r, m: (b, jnp.maximum(m - 1, 0), r * per + col))
    bias = _att_bias(rel_table, gi, qb, False)
    o, l = _att_call(
        (zq, zkv, zkv, zkv, zkv, bias),
        [cur(per_q, cq), prev(per_kv, ck), prev(per_kv, ck + 1), cur(per_kv, ck), cur(per_kv, ck + 1),
         pl.BlockSpec(bias.shape, lambda b, r, m: (0, 0, 0))],
        (B, dil, nm), (B, T // dil, dil * ATT_GW), lambda b, r, m: (b, m, r), qb, True,
        "att_prompt_g%d" % gi)
    return o.reshape(B * T, ATT_GW), l.reshape(B * T, ATT_GW)


def _att_sample(z, cache_l, rel_table, gi, B, T, t_valid):
    width, dil = ATT_GROUPS[gi]
    assert cache_l.shape[1] == width and width // dil == ATT_BAND
    z3 = z.reshape(B, T, Z_WIDTH)
    cq, ck, cv = Z_Q // ATT_GW + gi, Z_KV // ATT_GW + 2 * gi, Z_KV // ATT_GW + 2 * gi + 1
    cview = cache_l.reshape(B, ATT_BAND, dil * 2 * ATT_GW)
    single = dil > 1
    assert (not single) or t_valid <= dil
    nr = t_valid if single else 1
    cur = lambda col: pl.BlockSpec((None, T, ATT_GW), lambda b, r, m: (b, 0, col))
    hist = lambda kv: pl.BlockSpec((None, ATT_BAND, ATT_GW), lambda b, r, m: (b, 0, 2 * r + kv))
    bias = _att_bias(rel_table, gi, T, single)
    o, l = _att_call(
        (z3, cview, cview, z3, z3, bias),
        [cur(cq), hist(0), hist(1), cur(ck), cur(cv),
         pl.BlockSpec(bias.shape, lambda b, r, m: (0, 0, 0))],
        (B, nr, 1), (B, nr * T, ATT_GW), lambda b, r, m: (b, r, 0), T, False,
        "att_sample_g%d" % gi)
    if single:
        idx = np.arange(nr)
        pick = lambda a: jnp.pad(a.reshape(B, nr, T, ATT_GW)[:, idx, idx], ((0, 0), (0, T - nr), (0, 0)))
        o, l = pick(o), pick(l)
    return o.reshape(B * T, ATT_GW), l.reshape(B * T, ATT_GW)


def _merge_kernel(yp_ref, yg_ref, o0_ref, o1_ref, o2_ref, l0_ref, l1_ref, l2_ref, gz_ref, x_ref,
                  wb_ref, wo_ref, g2_ref, x1_ref, hn_ref):
    l0, l1, l2 = l0_ref[...], l1_ref[...], l2_ref[...]
    lm = jnp.maximum(jnp.maximum(l0, l1), l2)
    e0, e1, e2 = jnp.exp(l0 - lm), jnp.exp(l1 - lm), jnp.exp(l2 - lm)
    ya = (e0 * o0_ref[...] + e1 * o1_ref[...] + e2 * o2_ref[...]) / (e0 + e1 + e2)
    acc = None
    for bi, y in enumerate((yp_ref[...], yg_ref[...], ya)):
        proj = jnp.dot(y.astype(jnp.bfloat16), wb_ref[bi], preferred_element_type=jnp.float32)
        gz = gz_ref[:, bi * D_MODEL:(bi + 1) * D_MODEL]
        term = proj / (1.0 + jnp.exp(-gz))
        acc = term if acc is None else acc + term
    x1 = x_ref[...] + jnp.dot(acc.astype(jnp.bfloat16), wo_ref[...], preferred_element_type=jnp.float32)
    x1_ref[...] = x1
    ms = jnp.mean(x1 * x1, axis=-1, keepdims=True)
    hn_ref[...] = (x1 * lax.rsqrt(ms + RMS_EPS) * g2_ref[...]).astype(hn_ref.dtype)


def _merge(yp, yg, att, z, x, w_branch, w_out, norm2_g, tm):
    n = x.shape[0]
    half = pl.BlockSpec((tm, 512), lambda i: (i, 0))
    full = pl.BlockSpec((tm, D_MODEL), lambda i: (i, 0))
    (o0, l0), (o1, l1), (o2, l2) = att
    return pl.pallas_call(
        _merge_kernel,
        grid=(n // tm,),
        in_specs=[half] * 8 + [
            pl.BlockSpec((tm, 3 * D_MODEL), lambda i: (i, Z_GZ // (3 * D_MODEL))),
            full,
            pl.BlockSpec((3, 512, D_MODEL), lambda i: (0, 0, 0)),
            pl.BlockSpec((D_MODEL, D_MODEL), lambda i: (0, 0)),
            pl.BlockSpec((1, D_MODEL), lambda i: (0, 0))],
        out_specs=[full, full],
        out_shape=[jax.ShapeDtypeStruct((n, D_MODEL), jnp.float32),
                   jax.ShapeDtypeStruct((n, D_MODEL), jnp.bfloat16)],
        compiler_params=_cparams("parallel"),
        name="branch_merge",
    )(yp, yg, o0, o1, o2, l0, l1, l2, z, x, w_branch.astype(jnp.bfloat16),
      w_out.astype(jnp.bfloat16), norm2_g.reshape(1, D_MODEL))


def _peer_select_kernel(q_ref, k1_ref, k2_ref, c1_ref, n2_ref, a2_ref, r2_ref):
    q = q_ref[...].astype(jnp.bfloat16)
    s1 = _dot_nt(k1_ref[...], q[:, :PEER_HALF])
    s2 = _dot_nt(k2_ref[...], q[:, PEER_HALF:])
    none = float(PEER_TOPK)

    def top(vals, count, ranked):
        out, work = [], vals
        rank = jnp.full(vals.shape, none, jnp.float32) if ranked else None
        for a in range(count):
            m = jnp.max(work, axis=0, keepdims=True)
            out.append(m)
            hit = work >= m
            if ranked:
                rank = jnp.where(hit, float(a), rank)
            work = jnp.where(hit, -jnp.inf, work)
        return out, rank

    v1, rank1 = top(s1, PEER_TOPK, True)
    v2, rank2 = top(s2, PEER_TOPK, True)
    rid = lax.broadcasted_iota(jnp.int32, (PEER_TOPK, s2.shape[1]), 0)
    v2all = jnp.zeros((PEER_TOPK, s2.shape[1]), jnp.float32)
    for b, vb in enumerate(v2):
        v2all = jnp.where(rid == b, vb, v2all)
    cand = jnp.concatenate([v1[0] + v2all] + [v1[a] + v2all[:8] for a in range(1, PEER_TOPK)], axis=0)
    best, _ = top(cand, PEER_TOPK, False)
    tau = best[PEER_TOPK - 1]
    den = jnp.zeros_like(tau)
    n2 = jnp.zeros_like(s1)
    for a in range(PEER_TOPK):
        pair = v1[a] + v2all
        keep = pair >= tau
        den = den + jnp.sum(jnp.where(keep, jnp.exp(pair - best[0]), 0.0), axis=0, keepdims=True)
        n2 = jnp.where(rank1 == float(a), jnp.sum(keep.astype(jnp.float32), axis=0, keepdims=True), n2)
    c1_ref[...] = jnp.where(rank1 < none, jnp.exp(s1 - v1[0]) / den, 0.0)
    n2_ref[...] = n2
    a2_ref[...] = jnp.where(rank2 < none, jnp.exp(s2 - v2[0]), 0.0).astype(a2_ref.dtype)
    r2_ref[...] = rank2.astype(r2_ref.dtype)


def _peer_select(q, k1, k2, tn):
    n = q.shape[0]
    keyspec = pl.BlockSpec((None, PEER_NKEYS, PEER_HALF), lambda i, h: (h, 0, 0))
    out = pl.BlockSpec((None, PEER_NKEYS, tn), lambda i, h: (h, 0, i))
    shape = lambda dt: jax.ShapeDtypeStruct((PEER_HEADS, PEER_NKEYS, n), dt)
    return pl.pallas_call(
        _peer_select_kernel,
        grid=(n // tn, PEER_HEADS),
        in_specs=[pl.BlockSpec((tn, 2 * PEER_HALF), lambda i, h: (i, h)), keyspec, keyspec],
        out_specs=[out] * 4,
        out_shape=[shape(jnp.float32), shape(jnp.float32), shape(jnp.bfloat16), shape(jnp.bfloat16)],
        compiler_params=_cparams("parallel", "parallel"),
        name="peer_select",
    )(q, k1.astype(jnp.bfloat16), k2.astype(jnp.bfloat16))


def _peer_expert_kernel(hn_ref, u_ref, vt_ref, c1_ref, n2_ref, a2_ref, r2_ref, x_ref, o_ref,
                        acc_ref, st_ref, p_ref, a2s_ref, r2s_ref):
    c = pl.program_id(1)

    @pl.when(c == 0)
    def _():
        acc_ref[...] = jnp.zeros_like(acc_ref)
        a2s_ref[...] = a2_ref[...]
        r2s_ref[...] = r2_ref[...]

    st_ref[...] = _dot_nt(u_ref[...], hn_ref[...])
    tn = st_ref.shape[1]
    sub = 16
    for ii in range(PEER_ROWS):
        for lb in range(tn // 128):
            lanes = slice(lb * 128, (lb + 1) * 128)
            row = lambda ref, h: jnp.broadcast_to(ref[h, ii:ii + 1, lanes], (sub, 128)).astype(jnp.bfloat16)
            n2 = [row(n2_ref, h) for h in range(PEER_HEADS)]
            c1 = [row(c1_ref, h) for h in range(PEER_HEADS)]
            for jb in range(PEER_NKEYS // sub):
                js = slice(jb * sub, (jb + 1) * sub)
                es = slice(ii * PEER_NKEYS + jb * sub, ii * PEER_NKEYS + (jb + 1) * sub)
                s = st_ref[es, lanes]
                act = (0.5 * s * (1.0 + lax.erf(s * (2.0 ** -0.5)))).astype(jnp.bfloat16)
                w = None
                for h in range(PEER_HEADS):
                    sel = jnp.where(r2s_ref[h, js, lanes] < n2[h], a2s_ref[h, js, lanes], jnp.bfloat16(0)) * c1[h]
                    w = sel if w is None else w + sel
                p_ref[es, lanes] = w * act
    acc_ref[...] += jnp.dot(vt_ref[...], p_ref[...], preferred_element_type=jnp.float32)

    @pl.when(c == pl.num_programs(1) - 1)
    def _():
        o_ref[...] = x_ref[...] + acc_ref[...].T


def _peer_experts(hn, u_bf, vt_bf, sel, x1, tn):
    n = hn.shape[0]
    ce = PEER_ROWS * PEER_NKEYS
    c1, n2, a2, r2 = sel
    rowblk = pl.BlockSpec((PEER_HEADS, PEER_ROWS, tn), lambda i, c: (0, c, i))
    allblk = pl.BlockSpec((PEER_HEADS, PEER_NKEYS, tn), lambda i, c: (0, 0, i))
    tok = pl.BlockSpec((tn, D_MODEL), lambda i, c: (i, 0))
    return pl.pallas_call(
        _peer_expert_kernel,
        grid=(n // tn, PEER_EXPERTS // ce),
        in_specs=[tok,
                  pl.BlockSpec((ce, D_MODEL), lambda i, c: (c, 0)),
                  pl.BlockSpec((D_MODEL, ce), lambda i, c: (0, c)),
                  rowblk, rowblk, allblk, allblk, tok],
        out_specs=tok,
        out_shape=jax.ShapeDtypeStruct((n, D_MODEL), jnp.float32),
        scratch_shapes=[pltpu.VMEM((D_MODEL, tn), jnp.float32),
                        pltpu.VMEM((ce, tn), jnp.float32),
                        pltpu.VMEM((ce, tn), jnp.bfloat16),
                        pltpu.VMEM((PEER_HEADS, PEER_NKEYS, tn), jnp.bfloat16),
                        pltpu.VMEM((PEER_HEADS, PEER_NKEYS, tn), jnp.bfloat16)],
        compiler_params=_cparams("parallel", "arbitrary"),
        name="peer_experts",
    )(hn, u_bf, vt_bf, c1, n2, a2, r2, x1)


def _arrange_w_in(w_in):
    sizes = (512, 256, 256, 512, 512, 16, 1536, 1536, 1536, 3072)
    offs = np.concatenate([[0], np.cumsum(sizes)])
    up, gq, gk, gv, gr, ga, aq, ak, av, gz = [w_in[:, offs[i]:offs[i + 1]] for i in range(10)]
    kv = []
    for g in range(3):
        kv += [ak[:, g * ATT_GW:(g + 1) * ATT_GW], av[:, g * ATT_GW:(g + 1) * ATT_GW]]
    pad = jnp.zeros((w_in.shape[0], Z_WIDTH - Z_GA - GLA_GATE_RANK), w_in.dtype)
    return jnp.concatenate([gz] + kv + [aq, up, gv, gr, gq, gk, ga, pad], axis=1).astype(jnp.bfloat16)


def _layer(x, hn, B, T, t_valid, pos0, pool_hist, gla_s0, caches_l, rel_table, lw, tiles):
    (w_in_r, w_a2, b_a, w_pool, s_pool, gla_norm_g, w_branch, w_out, norm2_g,
     wq_bf, k1, k2, u_bf, vt_bf) = lw
    tm, tt, chunk, tn = tiles
    z = _matmul(hn, w_in_r, tm, 1024, "in_proj")
    hist16 = jnp.pad(pool_hist, ((0, 0), (POOL_HALO - POOL_HIST, 0), (0, 0)))
    y_pool = _pool_mix(z, hist16, w_pool, s_pool, B, T, tt, pos0)
    y_gla, st = _gla(z, jnp.swapaxes(gla_s0, 2, 3), w_a2, b_a, gla_norm_g, B, T, chunk, t_valid)
    if caches_l is None:
        att = [_att_prompt(z, rel_table, gi, B, T) for gi in range(3)]
    else:
        att = [_att_sample(z, caches_l[gi], rel_table, gi, B, T, t_valid) for gi in range(3)]
    x1, hn2 = _merge(y_pool, y_gla, att, z, x, w_branch, w_out, norm2_g, tm)
    q = _matmul(hn2, wq_bf, tm, 1024, "peer_query")
    sel = _peer_select(q, k1, k2, tn)
    x2 = _peer_experts(hn2, u_bf, vt_bf, sel, x1, tn)
    z3 = z.reshape(B, T, Z_WIDTH)
    u_new = z3[:, :t_valid, Z_POOL:Z_POOL + POOL_WIDTH]
    pool_tail = jnp.concatenate([pool_hist, u_new], axis=1)[:, -POOL_HIST:]
    kv_new = [z3[:, :t_valid, Z_KV + 2 * ATT_GW * g:Z_KV + 2 * ATT_GW * (g + 1)]
              .reshape(B, t_valid, 2, ATT_HPG, ATT_HEAD_DIM) for g in range(3)]
    return x2, pool_tail, jnp.swapaxes(st, 2, 3), kv_new


def kernel(x_prompt, x_sample, state_pool, state_gla, cache_att1, cache_att2, cache_att3, rel_table,
           norm1_g, w_in, w_a2, b_a, w_pool, s_pool, gla_norm_g, w_branch, w_out, norm2_g,
           peer_wq, peer_k1, peer_k2, peer_u, peer_v, final_norm_g):
    bp, tp, _ = x_prompt.shape
    bs, ts, _ = x_sample.shape
    depth = w_in.shape[0]
    ts_pad = 8
    caches = (cache_att1, cache_att2, cache_att3)
    xp = x_prompt.reshape(bp * tp, D_MODEL)
    xs = jnp.pad(x_sample, ((0, 0), (0, ts_pad - ts), (0, 0))).reshape(bs * ts_pad, D_MODEL)
    tiles_p = (512, 512, GLA_CHUNK, 512)
    rows_s = min(256, bs * ts_pad)
    tiles_s = (rows_s, ts_pad, ts_pad, rows_s)
    pool_p, pool_s, gla_p, gla_s = [], [], [], []
    att_p = [[] for _ in range(3)]
    att_s = [[] for _ in range(3)]
    for l in range(depth):
        lw = (_arrange_w_in(w_in[l]), w_a2[l], b_a[l], w_pool[l], s_pool[l], gla_norm_g[l], w_branch[l],
              w_out[l], norm2_g[l], peer_wq[l].astype(jnp.bfloat16), peer_k1[l], peer_k2[l],
              peer_u[l].astype(jnp.bfloat16), peer_v[l].T.astype(jnp.bfloat16))
        hp = _rmsnorm(xp, norm1_g[l], jnp.bfloat16, tiles_p[0])
        hs = _rmsnorm(xs, norm1_g[l], jnp.bfloat16, tiles_s[0])
        xp, pt, gp, kvp = _layer(
            xp, hp, bp, tp, tp, 0, jnp.zeros((bp, POOL_HIST, POOL_WIDTH), jnp.float32),
            jnp.zeros((bp, GLA_HEADS, GLA_DK, GLA_DV), jnp.float32), None, rel_table, lw, tiles_p)
        xs, ps, gs, kvs = _layer(
            xs, hs, bs, ts_pad, ts, PAST_LEN, state_pool[l], state_gla[l], [c[l] for c in caches],
            rel_table, lw, tiles_s)
        pool_p.append(pt)
        pool_s.append(ps)
        gla_p.append(gp)
        gla_s.append(gs)
        for g, (w, _) in enumerate(ATT_GROUPS):
            att_p[g].append(kvp[g][:, tp - min(w, tp):])
            att_s[g].append(kvs[g])
    y_prompt = _rmsnorm(xp, final_norm_g, jnp.float32, tiles_p[0]).reshape(bp, tp, D_MODEL)
    y_sample = _rmsnorm(xs, final_norm_g, jnp.float32, tiles_s[0]).reshape(bs, ts_pad, D_MODEL)[:, :ts]
    outs = [y_prompt, y_sample, jnp.stack(pool_p), jnp.stack(pool_s), jnp.stack(gla_p), jnp.stack(gla_s)]
    for g in range(3):
        outs += [jnp.stack(att_p[g]), jnp.stack(att_s[g])]
    return tuple(outs)
```

```python
import functools
import math

import numpy as np
import jax
import jax.numpy as jnp
from jax import lax
from jax.experimental import pallas as pl
from jax.experimental.pallas import tpu as pltpu

D_MODEL = 1024
RMS_EPS = 1e-6
PAST_LEN = 8192

POOL_WINDOWS = (2, 4, 8, 16)
POOL_GROUP = 128
POOL_WIDTH = 512
POOL_HIST = 15
POOL_HALO = 16

GLA_HEADS = 4
GLA_DK = 64
GLA_DV = 128
GLA_KW = 256
GLA_VW = 512
GLA_GATE_RANK = 16
GLA_TAU = 16.0
GLA_CHUNK = 64

ATT_GROUPS = ((128, 1), (512, 4), (2048, 16))
ATT_HPG = 8
ATT_HEAD_DIM = 64
ATT_GW = ATT_HPG * ATT_HEAD_DIM
ATT_BAND = 128
REL_BUCKETS = 32
REL_MAX_DIST = 2048
NEG = -1e30

PEER_HEADS = 8
PEER_NKEYS = 128
PEER_EXPERTS = PEER_NKEYS * PEER_NKEYS
PEER_HALF = 128
PEER_TOPK = 16
PEER_ROWS = 8

Z_GZ = 0
Z_KV = 3072
Z_Q = 6144
Z_POOL = 7680
Z_GV = 8192
Z_GR = 8704
Z_GQ = 9216
Z_GK = 9472
Z_GA = 9728
Z_WIDTH = 10240

VMEM_LIMIT = 48 * 1024 * 1024


def _cparams(*sem):
    return pltpu.CompilerParams(dimension_semantics=sem, vmem_limit_bytes=VMEM_LIMIT)


def _dot_nt(a, b):
    return lax.dot_general(a, b, (((1,), (1,)), ((), ())), preferred_element_type=jnp.float32)


def _dot_tn(a, b):
    return lax.dot_general(a, b, (((0,), (0,)), ((), ())), preferred_element_type=jnp.float32)


def _rmsnorm_kernel(x_ref, g_ref, o_ref):
    x = x_ref[...]
    ms = jnp.mean(x * x, axis=-1, keepdims=True)
    o_ref[...] = (x * lax.rsqrt(ms + RMS_EPS) * g_ref[...]).astype(o_ref.dtype)


def _rmsnorm(x, g, out_dtype, tm):
    n, d = x.shape
    return pl.pallas_call(
        _rmsnorm_kernel,
        grid=(n // tm,),
        in_specs=[pl.BlockSpec((tm, d), lambda i: (i, 0)),
                  pl.BlockSpec((1, d), lambda i: (0, 0))],
        out_specs=pl.BlockSpec((tm, d), lambda i: (i, 0)),
        out_shape=jax.ShapeDtypeStruct((n, d), out_dtype),
        compiler_params=_cparams("parallel"),
        name="rmsnorm",
    )(x, g.reshape(1, d))


def _add_rmsnorm_kernel(x_ref, d_ref, g_ref, s_ref, o_ref):
    x = x_ref[...] + d_ref[...]
    s_ref[...] = x
    ms = jnp.mean(x * x, axis=-1, keepdims=True)
    o_ref[...] = (x * lax.rsqrt(ms + RMS_EPS) * g_ref[...]).astype(o_ref.dtype)


def _add_rmsnorm(x, delta, g, out_dtype, tm):
    n, d = x.shape
    blk = pl.BlockSpec((tm, d), lambda i: (i, 0))
    return pl.pallas_call(
        _add_rmsnorm_kernel,
        grid=(n // tm,),
        in_specs=[blk, blk, pl.BlockSpec((1, d), lambda i: (0, 0))],
        out_specs=[blk, blk],
        out_shape=[jax.ShapeDtypeStruct((n, d), jnp.float32), jax.ShapeDtypeStruct((n, d), out_dtype)],
        compiler_params=_cparams("parallel"),
        name="add_rmsnorm",
    )(x, delta, g.reshape(1, d))


def _mm_kernel(a_ref, b_ref, o_ref):
    o_ref[...] = jnp.dot(a_ref[...], b_ref[...], preferred_element_type=jnp.float32)


def _matmul(a, b, tm, tn, name):
    m, k = a.shape
    n = b.shape[1]
    return pl.pallas_call(
        _mm_kernel,
        grid=(n // tn, m // tm),
        in_specs=[pl.BlockSpec((tm, k), lambda j, i: (i, 0)),
                  pl.BlockSpec((k, tn), lambda j, i: (0, j))],
        out_specs=pl.BlockSpec((tm, tn), lambda j, i: (i, j)),
        out_shape=jax.ShapeDtypeStruct((m, n), jnp.float32),
        compiler_params=_cparams("parallel", "parallel"),
        name=name,
    )(a, b)


def _pool_kernel(u_ref, prev_ref, hist_ref, w_ref, s_ref, o_ref, ext_ref, *, tt, pos0):
    t = pl.program_id(1)
    ext_ref[0:POOL_HALO, :] = jnp.where(t == 0, hist_ref[...], prev_ref[...])
    u = u_ref[...]
    ext_ref[POOL_HALO:, :] = u
    pos = (pos0 + t * tt + lax.broadcasted_iota(jnp.int32, (tt, 1), 0)).astype(jnp.float32)
    for gi, w in enumerate(POOL_WINDOWS):
        lo, hi = gi * POOL_GROUP, (gi + 1) * POOL_GROUP
        s = ext_ref[POOL_HALO:POOL_HALO + tt, lo:hi]
        for back in range(1, w):
            s = s + ext_ref[POOL_HALO - back:POOL_HALO - back + tt, lo:hi]
        cnt = jnp.minimum(float(w), pos + 1.0)
        d = s / cnt - u[:, lo:hi]
        y = jnp.dot(d.astype(jnp.bfloat16), w_ref[gi], preferred_element_type=jnp.float32)
        o_ref[:, lo:hi] = y * s_ref[:, lo:hi]


def _pool_mix(z, hist16, w_pool, s_pool, B, T, tt, pos0):
    nt = T // tt
    cb = Z_POOL // POOL_WIDTH

    def prev_map(b, t):
        return (jnp.maximum(b * T + t * tt - POOL_HALO, 0) // POOL_HALO, cb)

    return pl.pallas_call(
        functools.partial(_pool_kernel, tt=tt, pos0=pos0),
        grid=(B, nt),
        in_specs=[pl.BlockSpec((tt, POOL_WIDTH), lambda b, t: (b * nt + t, cb)),
                  pl.BlockSpec((POOL_HALO, POOL_WIDTH), prev_map),
                  pl.BlockSpec((None, POOL_HALO, POOL_WIDTH), lambda b, t: (b, 0, 0)),
                  pl.BlockSpec((4, POOL_GROUP, POOL_GROUP), lambda b, t: (0, 0, 0)),
                  pl.BlockSpec((1, POOL_WIDTH), lambda b, t: (0, 0))],
        out_specs=pl.BlockSpec((tt, POOL_WIDTH), lambda b, t: (b * nt + t, 0)),
        out_shape=jax.ShapeDtypeStruct((B * T, POOL_WIDTH), jnp.float32),
        scratch_shapes=[pltpu.VMEM((POOL_HALO + tt, POOL_WIDTH), jnp.float32)],
        compiler_params=_cparams("parallel", "parallel"),
        name="pool_mix",
    )(z, z, hist16, w_pool.astype(jnp.bfloat16), s_pool.reshape(1, POOL_WIDTH))


def _gla_kernel(q_ref, k_ref, v_ref, r_ref, ga_ref, wa_ref, ba_ref, gn_ref, s0_ref,
                y_ref, sout_ref, st_ref, *, chunk, t_valid):
    c = pl.program_id(1)

    @pl.when(c == 0)
    def _():
        st_ref[...] = s0_ref[...]

    x = jnp.dot(ga_ref[...].astype(jnp.bfloat16), wa_ref[...],
                preferred_element_type=jnp.float32) + ba_ref[...]
    g = -(jnp.maximum(-x, 0.0) + jnp.log1p(jnp.exp(-jnp.abs(x)))) / GLA_TAU
    row = c * chunk + lax.broadcasted_iota(jnp.int32, (chunk, 1), 0)
    valid = row < t_valid
    g = jnp.where(valid, g, 0.0)
    ri = lax.broadcasted_iota(jnp.int32, (chunk, chunk), 0)
    ci = lax.broadcasted_iota(jnp.int32, (chunk, chunk), 1)
    causal = ri >= ci
    b = jnp.dot(causal.astype(jnp.float32), g, preferred_element_type=jnp.float32,
                precision=lax.Precision.HIGHEST)
    b_last = b[chunk - 1:chunk, :]
    k = k_ref[...]
    qe = (q_ref[...] * (GLA_DK ** -0.5) * jnp.exp(b)).astype(jnp.bfloat16)
    ke = (k * jnp.exp(-b)).astype(jnp.bfloat16)
    kl = jnp.where(valid, k * jnp.exp(b_last - b), 0.0).astype(jnp.bfloat16)
    a_last = jnp.exp(b_last)
    v = v_ref[...].astype(jnp.bfloat16)
    r = r_ref[...]
    for h in range(GLA_HEADS):
        ks = slice(h * GLA_DK, (h + 1) * GLA_DK)
        vs = slice(h * GLA_DV, (h + 1) * GLA_DV)
        att = jnp.where(causal, _dot_nt(qe[:, ks], ke[:, ks]), 0.0)
        st = st_ref[h]
        o = (jnp.dot(att.astype(jnp.bfloat16), v[:, vs], preferred_element_type=jnp.float32)
             + _dot_nt(qe[:, ks], st.astype(jnp.bfloat16)))
        st_ref[h] = st * a_last[:, ks] + _dot_tn(v[:, vs], kl[:, ks])
        o = o * lax.rsqrt(jnp.mean(o * o, axis=-1, keepdims=True) + RMS_EPS) * gn_ref[...]
        rh = r[:, vs]
        y_ref[:, vs] = o * (rh / (1.0 + jnp.exp(-rh)))

    @pl.when(c == pl.num_programs(1) - 1)
    def _():
        sout_ref[...] = st_ref[...]


def _gla(z, s0t, w_a2, b_a, gnorm, B, T, chunk, t_valid):
    nc = T // chunk
    wa = jnp.zeros((128, GLA_KW), jnp.float32).at[:GLA_GATE_RANK].set(w_a2).astype(jnp.bfloat16)
    row = lambda b, c: b * nc + c
    return pl.pallas_call(
        functools.partial(_gla_kernel, chunk=chunk, t_valid=t_valid),
        grid=(B, nc),
        in_specs=[pl.BlockSpec((chunk, GLA_KW), lambda b, c: (row(b, c), Z_GQ // GLA_KW)),
                  pl.BlockSpec((chunk, GLA_KW), lambda b, c: (row(b, c), Z_GK // GLA_KW)),
                  pl.BlockSpec((chunk, GLA_VW), lambda b, c: (row(b, c), Z_GV // GLA_VW)),
                  pl.BlockSpec((chunk, GLA_VW), lambda b, c: (row(b, c), Z_GR // GLA_VW)),
                  pl.BlockSpec((chunk, 128), lambda b, c: (row(b, c), Z_GA // 128)),
                  pl.BlockSpec((128, GLA_KW), lambda b, c: (0, 0)),
                  pl.BlockSpec((1, GLA_KW), lambda b, c: (0, 0)),
                  pl.BlockSpec((1, GLA_DV), lambda b, c: (0, 0)),
                  pl.BlockSpec((None, GLA_HEADS, GLA_DV, GLA_DK), lambda b, c: (b, 0, 0, 0))],
        out_specs=[pl.BlockSpec((chunk, GLA_VW), lambda b, c: (row(b, c), 0)),
                   pl.BlockSpec((None, GLA_HEADS, GLA_DV, GLA_DK), lambda b, c: (b, 0, 0, 0))],
        out_shape=[jax.ShapeDtypeStruct((B * T, GLA_VW), jnp.float32),
                   jax.ShapeDtypeStruct((B, GLA_HEADS, GLA_DV, GLA_DK), jnp.float32)],
        scratch_shapes=[pltpu.VMEM((GLA_HEADS, GLA_DV, GLA_DK), jnp.float32)],
        compiler_params=_cparams("parallel", "arbitrary"),
        name="gla_scan",
    )(z, z, z, z, z, wa, b_a.reshape(1, GLA_KW), gnorm.reshape(1, GLA_DV), s0t)


def _att_kernel(q_ref, kp_ref, vp_ref, kc_ref, vc_ref, bias_ref, o_ref, l_ref, *, kp_rows, mask_first):
    q = (q_ref[...] * (ATT_HEAD_DIM ** -0.5)).astype(jnp.bfloat16)
    kp = kp_ref[...].astype(jnp.bfloat16)
    vp = vp_ref[...].astype(jnp.bfloat16)
    kc = kc_ref[...].astype(jnp.bfloat16)
    vc = vc_ref[...].astype(jnp.bfloat16)
    qb = q.shape[0]
    first = pl.program_id(2) == 0
    for h in range(ATT_HPG):
        sl = slice(h * ATT_HEAD_DIM, (h + 1) * ATT_HEAD_DIM)
        s1 = _dot_nt(q[:, sl], kp[:, sl]) + bias_ref[h, :, 0:kp_rows]
        if mask_first:
            s1 = jnp.where(first, NEG, s1)
        s2 = _dot_nt(q[:, sl], kc[:, sl]) + bias_ref[h, :, kp_rows:]
        m = jnp.maximum(jnp.max(s1, axis=-1, keepdims=True), jnp.max(s2, axis=-1, keepdims=True))
        p1 = jnp.exp(s1 - m)
        p2 = jnp.exp(s2 - m)
        den = jnp.sum(p1, axis=-1, keepdims=True) + jnp.sum(p2, axis=-1, keepdims=True)
        o = (jnp.dot(p1.astype(jnp.bfloat16), vp[:, sl], preferred_element_type=jnp.float32)
             + jnp.dot(p2.astype(jnp.bfloat16), vc[:, sl], preferred_element_type=jnp.float32))
        o_ref[:, sl] = o / den
        l_ref[:, sl] = jnp.broadcast_to(m + jnp.log(den), (qb, ATT_HEAD_DIM))


def _rel_bucket(dist):
    d = np.asarray(dist, dtype=np.int64)
    max_exact = REL_BUCKETS // 2
    ratio = np.log(np.maximum(d, 1) / max_exact) / np.log(REL_MAX_DIST / max_exact)
    large = np.minimum(max_exact + (ratio * (REL_BUCKETS - max_exact)).astype(np.int64), REL_BUCKETS - 1)
    return np.where(d < max_exact, d, large).astype(np.int32)


def _att_bias(rel_table, gi, qb):
    _, dil = ATT_GROUPS[gi]
    steps = np.arange(ATT_BAND + 1)
    tab = rel_table[_rel_bucket(steps * dil)][:, gi * ATT_HPG:(gi + 1) * ATT_HPG].T.astype(jnp.float32)
    qi = np.arange(qb)[:, None]
    col = np.arange(ATT_BAND + qb)[None, :]
    j = qi + ATT_BAND - col
    ok = (j >= 0) & (j <= ATT_BAND)
    jc = np.clip(j, 0, ATT_BAND)
    return jnp.where(jnp.asarray(ok)[None], tab[:, jc], NEG)


def _att_call(args, in_specs, grid, out_rows, out_block_map, qb, mask_first, name):
    blk = pl.BlockSpec((None, qb, ATT_GW), out_block_map)
    shape = jax.ShapeDtypeStruct(out_rows, jnp.float32)
    return pl.pallas_call(
        functools.partial(_att_kernel, kp_rows=ATT_BAND, mask_first=mask_first),
        grid=grid,
        in_specs=in_specs,
        out_specs=[blk, blk],
        out_shape=[shape, shape],
        compiler_params=_cparams("parallel", "parallel", "parallel"),
        name=name,
    )(*args)


def _att_prompt(z, rel_table, gi, B, T):
    _, dil = ATT_GROUPS[gi]
    qb = ATT_BAND
    nm = T // dil // qb
    if dil == 1:
        zq = zkv = z.reshape(B, T, Z_WIDTH)
        per_q = per_kv = Z_WIDTH // ATT_GW
        cq, ck = Z_Q // ATT_GW + gi, Z_KV // ATT_GW + 2 * gi
    else:
        zq = z[:, Z_Q + gi * ATT_GW:Z_Q + (gi + 1) * ATT_GW].reshape(B, T // dil, dil * ATT_GW)
        zkv = z[:, Z_KV + 2 * gi * ATT_GW:Z_KV + 2 * (gi + 1) * ATT_GW].reshape(B, T // dil, dil * 2 * ATT_GW)
        per_q, per_kv, cq, ck = 1, 2, 0, 0
    cur = lambda per, col: pl.BlockSpec((None, qb, ATT_GW), lambda b, r, m: (b, m, r * per + col))
    prev = lambda per, col: pl.BlockSpec((None, qb, ATT_GW),
                                         lambda b, r, m: (b, jnp.maximum(m - 1, 0), r * per + col))
    bias = _att_bias(rel_table, gi, qb)
    o, l = _att_call(
        (zq, zkv, zkv, zkv, zkv, bias),
        [cur(per_q, cq), prev(per_kv, ck), prev(per_kv, ck + 1), cur(per_kv, ck), cur(per_kv, ck + 1),
         pl.BlockSpec(bias.shape, lambda b, r, m: (0, 0, 0))],
        (B, dil, nm), (B, T // dil, dil * ATT_GW), lambda b, r, m: (b, m, r), qb, True,
        "att_prompt_g%d" % gi)
    return o.reshape(B * T, ATT_GW), l.reshape(B * T, ATT_GW)


def _att_sample_kernel(q_ref, kvc_ref, c0_ref, c1_ref, c2_ref, bh_ref, bc_ref, o_ref, *, t_valid):
    o_ref[...] = jnp.zeros_like(o_ref)
    for t in range(t_valid):
        outs, lses = [], []
        for gi, c_ref in enumerate((c0_ref, c1_ref, c2_ref)):
            r = t if ATT_GROUPS[gi][1] > 1 else 0
            q = q_ref[t, gi] * (ATT_HEAD_DIM ** -0.5)
            s_h = jnp.sum(c_ref[:, r, 0] * q[None], axis=-1, keepdims=True) + bh_ref[gi, t]
            s_c = jnp.sum(kvc_ref[:, gi, 0] * q[None], axis=-1, keepdims=True) + bc_ref[gi, t]
            m = jnp.maximum(jnp.max(s_h, axis=0, keepdims=True), jnp.max(s_c, axis=0, keepdims=True))
            p_h = jnp.exp(s_h - m)
            p_c = jnp.exp(s_c - m)
            den = jnp.sum(p_h, axis=0, keepdims=True) + jnp.sum(p_c, axis=0, keepdims=True)
            num = (jnp.sum(p_h * c_ref[:, r, 1], axis=0, keepdims=True)
                   + jnp.sum(p_c * kvc_ref[:, gi, 1], axis=0, keepdims=True))
            outs.append(num / den)
            lses.append(m + jnp.log(den))
        lm = jnp.maximum(jnp.maximum(lses[0], lses[1]), lses[2])
        es = [jnp.exp(l - lm) for l in lses]
        y = (es[0] * outs[0] + es[1] * outs[1] + es[2] * outs[2]) / (es[0] + es[1] + es[2])
        o_ref[t] = y[0]


def _att_sample(z, caches, layer, rel_table, B, T, t_valid):
    hd = (ATT_HPG, ATT_HEAD_DIM)
    z3 = z.reshape(B, T, Z_WIDTH)
    q = z3[:, :, Z_Q:Z_Q + 3 * ATT_GW].reshape(B, T, 3, *hd)
    kvc = z3[:, :, Z_KV:Z_KV + 6 * ATT_GW].reshape(B, T, 3, 2, *hd)
    views, cspecs, bh, bc = [], [], [], []
    tq = np.arange(t_valid)[:, None]
    for gi, (width, dil) in enumerate(ATT_GROUPS):
        cache = caches[gi]
        assert cache.shape[2] == width and width == ATT_BAND * dil and (dil == 1 or t_valid <= dil)
        classes = 1 if dil == 1 else t_valid
        views.append(cache.reshape(cache.shape[0], B, ATT_BAND, dil, 2, *hd))
        cspecs.append(pl.BlockSpec((None, None, ATT_BAND, classes, 2, *hd),
                                   lambda b: (layer, b, 0, 0, 0, 0, 0)))
        steps = np.arange(ATT_BAND + 1)
        tab = rel_table[_rel_bucket(steps * dil)][:, gi * ATT_HPG:(gi + 1) * ATT_HPG].astype(jnp.float32)
        rows = np.arange(ATT_BAND)[None, :]
        new = np.arange(T)[None, :]
        if dil == 1:
            jh, jc = ATT_BAND + tq - rows, tq - new
        else:
            jh, jc = np.broadcast_to(ATT_BAND - rows, (t_valid, ATT_BAND)), np.where(new == tq, 0, -1)
        for j, dst in ((jh, bh), (jc, bc)):
            ok = (j >= 0) & (j <= ATT_BAND)
            dst.append(jnp.where(jnp.asarray(ok)[:, :, None], tab[np.clip(j, 0, ATT_BAND)], NEG))
    bh = jnp.stack(bh)[..., None]
    bc = jnp.stack(bc)[..., None]
    const = lambda a: pl.BlockSpec(a.shape, lambda b: (0,) * a.ndim, pipeline_mode=pl.Buffered(1))
    y = pl.pallas_call(
        functools.partial(_att_sample_kernel, t_valid=t_valid),
        grid=(B,),
        in_specs=[pl.BlockSpec((None, T, 3, *hd), lambda b: (b, 0, 0, 0, 0)),
                  pl.BlockSpec((None, T, 3, 2, *hd), lambda b: (b, 0, 0, 0, 0, 0)),
                  *cspecs, const(bh), const(bc)],
        out_specs=pl.BlockSpec((None, T, *hd), lambda b: (b, 0, 0, 0)),
        out_shape=jax.ShapeDtypeStruct((B, T, *hd), jnp.float32),
        compiler_params=_cparams("parallel"),
        name="att_sample",
    )(q, kvc, *views, bh, bc)
    return y.reshape(B * T, ATT_GW)


def _merge_kernel(yp_ref, yg_ref, *refs):
    att_refs, (gz_ref, x_ref, wb_ref, wo_ref, g2_ref, x1_ref, hn_ref) = refs[:-7], refs[-7:]
    if len(att_refs) == 1:
        ya = att_refs[0][...]
    else:
        o0_ref, o1_ref, o2_ref, l0_ref, l1_ref, l2_ref = att_refs
        l0, l1, l2 = l0_ref[...], l1_ref[...], l2_ref[...]
        lm = jnp.maximum(jnp.maximum(l0, l1), l2)
        e0, e1, e2 = jnp.exp(l0 - lm), jnp.exp(l1 - lm), jnp.exp(l2 - lm)
        ya = (e0 * o0_ref[...] + e1 * o1_ref[...] + e2 * o2_ref[...]) / (e0 + e1 + e2)
    acc = None
    for bi, y in enumerate((yp_ref[...], yg_ref[...], ya)):
        proj = jnp.dot(y.astype(jnp.bfloat16), wb_ref[bi], preferred_element_type=jnp.float32)
        gz = gz_ref[:, bi * D_MODEL:(bi + 1) * D_MODEL]
        term = proj / (1.0 + jnp.exp(-gz))
        acc = term if acc is None else acc + term
    x1 = x_ref[...] + jnp.dot(acc.astype(jnp.bfloat16), wo_ref[...], preferred_element_type=jnp.float32)
    x1_ref[...] = x1
    ms = jnp.mean(x1 * x1, axis=-1, keepdims=True)
    hn_ref[...] = (x1 * lax.rsqrt(ms + RMS_EPS) * g2_ref[...]).astype(hn_ref.dtype)


def _merge(yp, yg, att, z, x, w_branch, w_out, norm2_g, tm):
    n = x.shape[0]
    half = pl.BlockSpec((tm, 512), lambda i: (i, 0))
    full = pl.BlockSpec((tm, D_MODEL), lambda i: (i, 0))
    if isinstance(att, list):
        (o0, l0), (o1, l1), (o2, l2) = att
        att = (o0, o1, o2, l0, l1, l2)
    else:
        att = (att,)
    return pl.pallas_call(
        _merge_kernel,
        grid=(n // tm,),
        in_specs=[half] * (2 + len(att)) + [
            pl.BlockSpec((tm, 3 * D_MODEL), lambda i: (i, Z_GZ // (3 * D_MODEL))),
            full,
            pl.BlockSpec((3, 512, D_MODEL), lambda i: (0, 0, 0)),
            pl.BlockSpec((D_MODEL, D_MODEL), lambda i: (0, 0)),
            pl.BlockSpec((1, D_MODEL), lambda i: (0, 0))],
        out_specs=[full, full],
        out_shape=[jax.ShapeDtypeStruct((n, D_MODEL), jnp.float32),
                   jax.ShapeDtypeStruct((n, D_MODEL), jnp.bfloat16)],
        compiler_params=_cparams("parallel"),
        name="branch_merge",
    )(yp, yg, *att, z, x, w_branch.astype(jnp.bfloat16),
      w_out.astype(jnp.bfloat16), norm2_g.reshape(1, D_MODEL))


def _peer_select_kernel(q_ref, k1_ref, k2_ref, c1_ref, n2_ref, a2_ref, r2_ref):
    q = q_ref[...].astype(jnp.bfloat16)
    s1 = _dot_nt(k1_ref[...], q[:, :PEER_HALF])
    s2 = _dot_nt(k2_ref[...], q[:, PEER_HALF:])
    none = float(PEER_TOPK)

    def top(vals, count, ranked):
        out, work = [], vals
        rank = jnp.full(vals.shape, none, jnp.float32) if ranked else None
        for a in range(count):
            m = jnp.max(work, axis=0, keepdims=True)
            out.append(m)
            hit = work >= m
            if ranked:
                rank = jnp.where(hit, float(a), rank)
            work = jnp.where(hit, -jnp.inf, work)
        return out, rank

    v1, rank1 = top(s1, PEER_TOPK, True)
    v2, rank2 = top(s2, PEER_TOPK, True)
    rid = lax.broadcasted_iota(jnp.int32, (PEER_TOPK, s2.shape[1]), 0)
    v2all = jnp.zeros((PEER_TOPK, s2.shape[1]), jnp.float32)
    for b, vb in enumerate(v2):
        v2all = jnp.where(rid == b, vb, v2all)
    cand = jnp.concatenate([v1[0] + v2all] + [v1[a] + v2all[:8] for a in range(1, PEER_TOPK)], axis=0)
    best, _ = top(cand, PEER_TOPK, False)
    tau = best[PEER_TOPK - 1]
    den = jnp.zeros_like(tau)
    n2 = jnp.zeros_like(s1)
    for a in range(PEER_TOPK):
        pair = v1[a] + v2all
        keep = pair >= tau
        den = den + jnp.sum(jnp.where(keep, jnp.exp(pair - best[0]), 0.0), axis=0, keepdims=True)
        n2 = jnp.where(rank1 == float(a), jnp.sum(keep.astype(jnp.float32), axis=0, keepdims=True), n2)
    c1_ref[...] = jnp.where(rank1 < none, jnp.exp(s1 - v1[0]) / den, 0.0)
    n2_ref[...] = n2
    a2_ref[...] = jnp.where(rank2 < none, jnp.exp(s2 - v2[0]), 0.0).astype(a2_ref.dtype)
    r2_ref[...] = rank2.astype(r2_ref.dtype)


def _peer_select(q, k1, k2, tn):
    n = q.shape[0]
    keyspec = pl.BlockSpec((None, PEER_NKEYS, PEER_HALF), lambda i, h: (h, 0, 0))
    out = pl.BlockSpec((None, PEER_NKEYS, tn), lambda i, h: (h, 0, i))
    shape = lambda dt: jax.ShapeDtypeStruct((PEER_HEADS, PEER_NKEYS, n), dt)
    return pl.pallas_call(
        _peer_select_kernel,
        grid=(n // tn, PEER_HEADS),
        in_specs=[pl.BlockSpec((tn, 2 * PEER_HALF), lambda i, h: (i, h)), keyspec, keyspec],
        out_specs=[out] * 4,
        out_shape=[shape(jnp.float32), shape(jnp.float32), shape(jnp.bfloat16), shape(jnp.bfloat16)],
        compiler_params=_cparams("parallel", "parallel"),
        name="peer_select",
    )(q, k1.astype(jnp.bfloat16), k2.astype(jnp.bfloat16))


def _peer_expert_kernel(hn_ref, u_ref, v_ref, c1_ref, n2_ref, a2_ref, r2_ref, o_ref,
                        st_ref, p_ref, a2s_ref, r2s_ref):
    c = pl.program_id(1)

    @pl.when(c == 0)
    def _():
        o_ref[...] = jnp.zeros_like(o_ref)
        a2s_ref[...] = a2_ref[...]
        r2s_ref[...] = r2_ref[...]

    st_ref[...] = _dot_nt(u_ref[...], hn_ref[...])
    tn = st_ref.shape[1]
    sub = 16
    for ii in range(PEER_ROWS):
        for lb in range(tn // 128):
            lanes = slice(lb * 128, (lb + 1) * 128)
            row = lambda ref, h: jnp.broadcast_to(ref[h, ii:ii + 1, lanes], (sub, 128)).astype(jnp.bfloat16)
            n2 = [row(n2_ref, h) for h in range(PEER_HEADS)]
            c1 = [row(c1_ref, h) for h in range(PEER_HEADS)]
            for jb in range(PEER_NKEYS // sub):
                js = slice(jb * sub, (jb + 1) * sub)
                es = slice(ii * PEER_NKEYS + jb * sub, ii * PEER_NKEYS + (jb + 1) * sub)
                s = st_ref[es, lanes]
                act = (0.5 * s * (1.0 + lax.erf(s * (2.0 ** -0.5)))).astype(jnp.bfloat16)
                w = None
                for h in range(PEER_HEADS):
                    sel = jnp.where(r2s_ref[h, js, lanes] < n2[h], a2s_ref[h, js, lanes], jnp.bfloat16(0)) * c1[h]
                    w = sel if w is None else w + sel
                p_ref[es, lanes] = w * act
    o_ref[...] += _dot_tn(p_ref[...], v_ref[...])


def _peer_experts(hn, u_bf, v_bf, sel, tn):
    n = hn.shape[0]
    ce = PEER_ROWS * PEER_NKEYS
    c1, n2, a2, r2 = sel
    rowblk = pl.BlockSpec((PEER_HEADS, PEER_ROWS, tn), lambda i, c: (0, c, i))
    allblk = pl.BlockSpec((PEER_HEADS, PEER_NKEYS, tn), lambda i, c: (0, 0, i),
                          pipeline_mode=pl.Buffered(1))
    tok = pl.BlockSpec((tn, D_MODEL), lambda i, c: (i, 0))
    table = pl.BlockSpec((ce, D_MODEL), lambda i, c: (c, 0))
    return pl.pallas_call(
        _peer_expert_kernel,
        grid=(n // tn, PEER_EXPERTS // ce),
        in_specs=[tok, table, table, rowblk, rowblk, allblk, allblk],
        out_specs=tok,
        out_shape=jax.ShapeDtypeStruct((n, D_MODEL), jnp.float32),
        scratch_shapes=[pltpu.VMEM((ce, tn), jnp.float32),
                        pltpu.VMEM((ce, tn), jnp.bfloat16),
                        pltpu.VMEM((PEER_HEADS, PEER_NKEYS, tn), jnp.bfloat16),
                        pltpu.VMEM((PEER_HEADS, PEER_NKEYS, tn), jnp.bfloat16)],
        compiler_params=_cparams("parallel", "arbitrary"),
        name="peer_experts",
    )(hn, u_bf, v_bf, c1, n2, a2, r2)


def _arrange_w_in(w_in):
    sizes = (512, 256, 256, 512, 512, 16, 1536, 1536, 1536, 3072)
    offs = np.concatenate([[0], np.cumsum(sizes)])
    up, gq, gk, gv, gr, ga, aq, ak, av, gz = [w_in[:, offs[i]:offs[i + 1]] for i in range(10)]
    kv = []
    for g in range(3):
        kv += [ak[:, g * ATT_GW:(g + 1) * ATT_GW], av[:, g * ATT_GW:(g + 1) * ATT_GW]]
    pad = jnp.zeros((w_in.shape[0], Z_WIDTH - Z_GA - GLA_GATE_RANK), w_in.dtype)
    return jnp.concatenate([gz] + kv + [aq, up, gv, gr, gq, gk, ga, pad], axis=1).astype(jnp.bfloat16)


def _layer(x, hn, B, T, t_valid, pos0, pool_hist, gla_s0, caches, layer, rel_table, lw, tiles):
    (w_in_r, w_a2, b_a, w_pool, s_pool, gla_norm_g, w_branch, w_out, norm2_g,
     wq_bf, k1, k2, u_bf, v_bf) = lw
    tm, tt, chunk, tn_sel, tn = tiles
    z = _matmul(hn, w_in_r, tm, 1024, "in_proj")
    hist16 = jnp.pad(pool_hist, ((0, 0), (POOL_HALO - POOL_HIST, 0), (0, 0)))
    y_pool = _pool_mix(z, hist16, w_pool, s_pool, B, T, tt, pos0)
    y_gla, st = _gla(z, jnp.swapaxes(gla_s0, 2, 3), w_a2, b_a, gla_norm_g, B, T, chunk, t_valid)
    if caches is None:
        att = [_att_prompt(z, rel_table, gi, B, T) for gi in range(3)]
    else:
        att = _att_sample(z, caches, layer, rel_table, B, T, t_valid)
    x1, hn2 = _merge(y_pool, y_gla, att, z, x, w_branch, w_out, norm2_g, tm)
    q = _matmul(hn2, wq_bf, tm, 1024, "peer_query")
    sel = _peer_select(q, k1, k2, tn_sel)
    delta = _peer_experts(hn2, u_bf, v_bf, sel, tn)
    z3 = z.reshape(B, T, Z_WIDTH)
    u_new = z3[:, :t_valid, Z_POOL:Z_POOL + POOL_WIDTH]
    pool_tail = jnp.concatenate([pool_hist, u_new], axis=1)[:, -POOL_HIST:]
    kv_new = [z3[:, :t_valid, Z_KV + 2 * ATT_GW * g:Z_KV + 2 * ATT_GW * (g + 1)]
              .reshape(B, t_valid, 2, ATT_HPG, ATT_HEAD_DIM) for g in range(3)]
    return x1, delta, pool_tail, jnp.swapaxes(st, 2, 3), kv_new


def kernel(x_prompt, x_sample, state_pool, state_gla, cache_att1, cache_att2, cache_att3, rel_table,
           norm1_g, w_in, w_a2, b_a, w_pool, s_pool, gla_norm_g, w_branch, w_out, norm2_g,
           peer_wq, peer_k1, peer_k2, peer_u, peer_v, final_norm_g):
    bp, tp, _ = x_prompt.shape
    bs, ts, _ = x_sample.shape
    depth = w_in.shape[0]
    ts_pad = 8
    caches = (cache_att1, cache_att2, cache_att3)
    xp = x_prompt.reshape(bp * tp, D_MODEL)
    xs = jnp.pad(x_sample, ((0, 0), (0, ts_pad - ts), (0, 0))).reshape(bs * ts_pad, D_MODEL)
    tiles_p = (512, 512, GLA_CHUNK, 512, 1024)
    rows_s = min(256, bs * ts_pad)
    tiles_s = (rows_s, ts_pad, ts_pad, rows_s, rows_s)
    pool_p, pool_s, gla_p, gla_s = [], [], [], []
    att_p = [[] for _ in range(3)]
    att_s = [[] for _ in range(3)]
    hp = _rmsnorm(xp, norm1_g[0], jnp.bfloat16, tiles_p[0])
    hs = _rmsnorm(xs, norm1_g[0], jnp.bfloat16, tiles_s[0])
    for l in range(depth):
        lw = (_arrange_w_in(w_in[l]), w_a2[l], b_a[l], w_pool[l], s_pool[l], gla_norm_g[l], w_branch[l],
              w_out[l], norm2_g[l], peer_wq[l].astype(jnp.bfloat16), peer_k1[l], peer_k2[l],
              peer_u[l].astype(jnp.bfloat16), peer_v[l].astype(jnp.bfloat16))
        xp, dp, pt, gp, kvp = _layer(
            xp, hp, bp, tp, tp, 0, jnp.zeros((bp, POOL_HIST, POOL_WIDTH), jnp.float32),
            jnp.zeros((bp, GLA_HEADS, GLA_DK, GLA_DV), jnp.float32), None, l, rel_table, lw, tiles_p)
        xs, ds, ps, gs, kvs = _layer(
            xs, hs, bs, ts_pad, ts, PAST_LEN, state_pool[l], state_gla[l], caches, l, rel_table, lw, tiles_s)
        last = l == depth - 1
        gain = final_norm_g if last else norm1_g[l + 1]
        xp, hp = _add_rmsnorm(xp, dp, gain, jnp.float32 if last else jnp.bfloat16, tiles_p[0])
        xs, hs = _add_rmsnorm(xs, ds, gain, jnp.float32 if last else jnp.bfloat16, tiles_s[0])
        pool_p.append(pt)
        pool_s.append(ps)
        gla_p.append(gp)
        gla_s.append(gs)
        for g, (w, _) in enumerate(ATT_GROUPS):
            att_p[g].append(kvp[g][:, tp - min(w, tp):])
            att_s[g].append(kvs[g])
    y_prompt = hp.reshape(bp, tp, D_MODEL)
    y_sample = hs.reshape(bs, ts_pad, D_MODEL)[:, :ts]
    outs = [y_prompt, y_sample, jnp.stack(pool_p), jnp.stack(pool_s), jnp.stack(gla_p), jnp.stack(gla_s)]
    for g in range(3):
        outs += [jnp.stack(att_p[g]), jnp.stack(att_s[g])]
    return tuple(outs)
```

```python
import functools
import math

import numpy as np
import jax
import jax.numpy as jnp
from jax import lax
from jax.experimental import pallas as pl
from jax.experimental.pallas import tpu as pltpu

D_MODEL = 1024
RMS_EPS = 1e-6
PAST_LEN = 8192

POOL_WINDOWS = (2, 4, 8, 16)
POOL_GROUP = 128
POOL_WIDTH = 512
POOL_HIST = 15
POOL_HALO = 16

GLA_HEADS = 4
GLA_DK = 64
GLA_DV = 128
GLA_KW = 256
GLA_VW = 512
GLA_GATE_RANK = 16
GLA_TAU = 16.0
GLA_CHUNK = 64

ATT_GROUPS = ((128, 1), (512, 4), (2048, 16))
ATT_HPG = 8
ATT_HEAD_DIM = 64
ATT_GW = ATT_HPG * ATT_HEAD_DIM
ATT_BAND = 128
ATT_QBLOCK = 256
REL_BUCKETS = 32
REL_MAX_DIST = 2048
NEG = -1e30

PEER_HEADS = 8
PEER_NKEYS = 128
PEER_EXPERTS = PEER_NKEYS * PEER_NKEYS
PEER_HALF = 128
PEER_TOPK = 16
PEER_ROWS = 4

Z_GZ = 0
Z_KV = 3072
Z_Q = 6144
Z_POOL = 7680
Z_GV = 8192
Z_GR = 8704
Z_GQ = 9216
Z_GK = 9472
Z_GA = 9728
Z_WIDTH = 10240

VMEM_LIMIT = 48 * 1024 * 1024


def _cparams(*sem):
    return pltpu.CompilerParams(dimension_semantics=sem, vmem_limit_bytes=VMEM_LIMIT)


def _dot_nt(a, b):
    return lax.dot_general(a, b, (((1,), (1,)), ((), ())), preferred_element_type=jnp.float32)


def _dot_tn(a, b):
    return lax.dot_general(a, b, (((0,), (0,)), ((), ())), preferred_element_type=jnp.float32)


def _rmsnorm_kernel(x_ref, g_ref, o_ref):
    x = x_ref[...]
    ms = jnp.mean(x * x, axis=-1, keepdims=True)
    o_ref[...] = (x * lax.rsqrt(ms + RMS_EPS) * g_ref[...]).astype(o_ref.dtype)


def _rmsnorm(x, g, out_dtype, tm):
    n, d = x.shape
    return pl.pallas_call(
        _rmsnorm_kernel,
        grid=(n // tm,),
        in_specs=[pl.BlockSpec((tm, d), lambda i: (i, 0)),
                  pl.BlockSpec((1, d), lambda i: (0, 0))],
        out_specs=pl.BlockSpec((tm, d), lambda i: (i, 0)),
        out_shape=jax.ShapeDtypeStruct((n, d), out_dtype),
        compiler_params=_cparams("parallel"),
        name="rmsnorm",
    )(x, g.reshape(1, d))


def _add_rmsnorm_kernel(x_ref, d_ref, g_ref, s_ref, o_ref):
    x = x_ref[...] + d_ref[...]
    s_ref[...] = x
    ms = jnp.mean(x * x, axis=-1, keepdims=True)
    o_ref[...] = (x * lax.rsqrt(ms + RMS_EPS) * g_ref[...]).astype(o_ref.dtype)


def _add_rmsnorm(x, delta, g, out_dtype, tm):
    n, d = x.shape
    blk = pl.BlockSpec((tm, d), lambda i: (i, 0))
    return pl.pallas_call(
        _add_rmsnorm_kernel,
        grid=(n // tm,),
        in_specs=[blk, blk, pl.BlockSpec((1, d), lambda i: (0, 0))],
        out_specs=[blk, blk],
        out_shape=[jax.ShapeDtypeStruct((n, d), jnp.float32), jax.ShapeDtypeStruct((n, d), out_dtype)],
        compiler_params=_cparams("parallel"),
        name="add_rmsnorm",
    )(x, delta, g.reshape(1, d))


def _mm_kernel(a_ref, b_ref, o_ref):
    o_ref[...] = jnp.dot(a_ref[...], b_ref[...], preferred_element_type=jnp.float32)


def _matmul(a, b, tm, tn, name):
    m, k = a.shape
    n = b.shape[1]
    return pl.pallas_call(
        _mm_kernel,
        grid=(n // tn, m // tm),
        in_specs=[pl.BlockSpec((tm, k), lambda j, i: (i, 0)),
                  pl.BlockSpec((k, tn), lambda j, i: (0, j))],
        out_specs=pl.BlockSpec((tm, tn), lambda j, i: (i, j)),
        out_shape=jax.ShapeDtypeStruct((m, n), jnp.float32),
        compiler_params=_cparams("parallel", "parallel"),
        name=name,
    )(a, b)


def _pool_kernel(u_ref, prev_ref, hist_ref, w_ref, s_ref, o_ref, ext_ref, *, tt, pos0):
    t = pl.program_id(1)
    ext_ref[0:POOL_HALO, :] = jnp.where(t == 0, hist_ref[...], prev_ref[...])
    u = u_ref[...]
    ext_ref[POOL_HALO:, :] = u
    pos = (pos0 + t * tt + lax.broadcasted_iota(jnp.int32, (tt, 1), 0)).astype(jnp.float32)
    for gi, w in enumerate(POOL_WINDOWS):
        lo, hi = gi * POOL_GROUP, (gi + 1) * POOL_GROUP
        s = ext_ref[POOL_HALO:POOL_HALO + tt, lo:hi]
        for back in range(1, w):
            s = s + ext_ref[POOL_HALO - back:POOL_HALO - back + tt, lo:hi]
        cnt = jnp.minimum(float(w), pos + 1.0)
        d = s / cnt - u[:, lo:hi]
        y = jnp.dot(d.astype(jnp.bfloat16), w_ref[gi], preferred_element_type=jnp.float32)
        o_ref[:, lo:hi] = y * s_ref[:, lo:hi]


def _pool_mix(z, hist16, w_pool, s_pool, B, T, tt, pos0):
    nt = T // tt
    cb = Z_POOL // POOL_WIDTH

    def prev_map(b, t):
        return (jnp.maximum(b * T + t * tt - POOL_HALO, 0) // POOL_HALO, cb)

    return pl.pallas_call(
        functools.partial(_pool_kernel, tt=tt, pos0=pos0),
        grid=(B, nt),
        in_specs=[pl.BlockSpec((tt, POOL_WIDTH), lambda b, t: (b * nt + t, cb)),
                  pl.BlockSpec((POOL_HALO, POOL_WIDTH), prev_map),
                  pl.BlockSpec((None, POOL_HALO, POOL_WIDTH), lambda b, t: (b, 0, 0)),
                  pl.BlockSpec((4, POOL_GROUP, POOL_GROUP), lambda b, t: (0, 0, 0)),
                  pl.BlockSpec((1, POOL_WIDTH), lambda b, t: (0, 0))],
        out_specs=pl.BlockSpec((tt, POOL_WIDTH), lambda b, t: (b * nt + t, 0)),
        out_shape=jax.ShapeDtypeStruct((B * T, POOL_WIDTH), jnp.float32),
        scratch_shapes=[pltpu.VMEM((POOL_HALO + tt, POOL_WIDTH), jnp.float32)],
        compiler_params=_cparams("parallel", "parallel"),
        name="pool_mix",
    )(z, z, hist16, w_pool.astype(jnp.bfloat16), s_pool.reshape(1, POOL_WIDTH))


def _gla_kernel(q_ref, k_ref, v_ref, r_ref, ga_ref, wa_ref, ba_ref, gn_ref, s0_ref,
                y_ref, sout_ref, st_ref, *, chunk, t_valid):
    c = pl.program_id(1)

    @pl.when(c == 0)
    def _():
        st_ref[...] = s0_ref[...]

    x = jnp.dot(ga_ref[...].astype(jnp.bfloat16), wa_ref[...],
                preferred_element_type=jnp.float32) + ba_ref[...]
    g = -(jnp.maximum(-x, 0.0) + jnp.log1p(jnp.exp(-jnp.abs(x)))) / GLA_TAU
    row = c * chunk + lax.broadcasted_iota(jnp.int32, (chunk, 1), 0)
    valid = row < t_valid
    g = jnp.where(valid, g, 0.0)
    ri = lax.broadcasted_iota(jnp.int32, (chunk, chunk), 0)
    ci = lax.broadcasted_iota(jnp.int32, (chunk, chunk), 1)
    causal = ri >= ci
    b = jnp.dot(causal.astype(jnp.float32), g, preferred_element_type=jnp.float32,
                precision=lax.Precision.HIGHEST)
    b_last = b[chunk - 1:chunk, :]
    k = k_ref[...]
    qe = (q_ref[...] * (GLA_DK ** -0.5) * jnp.exp(b)).astype(jnp.bfloat16)
    ke = (k * jnp.exp(-b)).astype(jnp.bfloat16)
    kl = jnp.where(valid, k * jnp.exp(b_last - b), 0.0).astype(jnp.bfloat16)
    a_last = jnp.exp(b_last)
    v = v_ref[...].astype(jnp.bfloat16)
    r = r_ref[...]
    for h in range(GLA_HEADS):
        ks = slice(h * GLA_DK, (h + 1) * GLA_DK)
        vs = slice(h * GLA_DV, (h + 1) * GLA_DV)
        att = jnp.where(causal, _dot_nt(qe[:, ks], ke[:, ks]), 0.0)
        st = st_ref[h]
        o = (jnp.dot(att.astype(jnp.bfloat16), v[:, vs], preferred_element_type=jnp.float32)
             + _dot_nt(qe[:, ks], st.astype(jnp.bfloat16)))
        st_ref[h] = st * a_last[:, ks] + _dot_tn(v[:, vs], kl[:, ks])
        o = o * lax.rsqrt(jnp.mean(o * o, axis=-1, keepdims=True) + RMS_EPS) * gn_ref[...]
        rh = r[:, vs]
        y_ref[:, vs] = o * (rh / (1.0 + jnp.exp(-rh)))

    @pl.when(c == pl.num_programs(1) - 1)
    def _():
        sout_ref[...] = st_ref[...]


def _gla(z, s0t, w_a2, b_a, gnorm, B, T, chunk, t_valid):
    nc = T // chunk
    wa = jnp.zeros((128, GLA_KW), jnp.float32).at[:GLA_GATE_RANK].set(w_a2).astype(jnp.bfloat16)
    row = lambda b, c: b * nc + c
    return pl.pallas_call(
        functools.partial(_gla_kernel, chunk=chunk, t_valid=t_valid),
        grid=(B, nc),
        in_specs=[pl.BlockSpec((chunk, GLA_KW), lambda b, c: (row(b, c), Z_GQ // GLA_KW)),
                  pl.BlockSpec((chunk, GLA_KW), lambda b, c: (row(b, c), Z_GK // GLA_KW)),
                  pl.BlockSpec((chunk, GLA_VW), lambda b, c: (row(b, c), Z_GV // GLA_VW)),
                  pl.BlockSpec((chunk, GLA_VW), lambda b, c: (row(b, c), Z_GR // GLA_VW)),
                  pl.BlockSpec((chunk, 128), lambda b, c: (row(b, c), Z_GA // 128)),
                  pl.BlockSpec((128, GLA_KW), lambda b, c: (0, 0)),
                  pl.BlockSpec((1, GLA_KW), lambda b, c: (0, 0)),
                  pl.BlockSpec((1, GLA_DV), lambda b, c: (0, 0)),
                  pl.BlockSpec((None, GLA_HEADS, GLA_DV, GLA_DK), lambda b, c: (b, 0, 0, 0))],
        out_specs=[pl.BlockSpec((chunk, GLA_VW), lambda b, c: (row(b, c), 0)),
                   pl.BlockSpec((None, GLA_HEADS, GLA_DV, GLA_DK), lambda b, c: (b, 0, 0, 0))],
        out_shape=[jax.ShapeDtypeStruct((B * T, GLA_VW), jnp.float32),
                   jax.ShapeDtypeStruct((B, GLA_HEADS, GLA_DV, GLA_DK), jnp.float32)],
        scratch_shapes=[pltpu.VMEM((GLA_HEADS, GLA_DV, GLA_DK), jnp.float32)],
        compiler_params=_cparams("parallel", "arbitrary"),
        name="gla_scan",
    )(z, z, z, z, z, wa, b_a.reshape(1, GLA_KW), gnorm.reshape(1, GLA_DV), s0t)


def _att_kernel(q_ref, kp_ref, vp_ref, kc_ref, vc_ref, bias_ref, o_ref, l_ref, *, kp_rows, mask_first):
    q = (q_ref[...] * (ATT_HEAD_DIM ** -0.5)).astype(jnp.bfloat16)
    kp = kp_ref[...].astype(jnp.bfloat16)
    vp = vp_ref[...].astype(jnp.bfloat16)
    kc = kc_ref[...].astype(jnp.bfloat16)
    vc = vc_ref[...].astype(jnp.bfloat16)
    qb = q.shape[0]
    first = pl.program_id(2) == 0
    for h in range(ATT_HPG):
        sl = slice(h * ATT_HEAD_DIM, (h + 1) * ATT_HEAD_DIM)
        s1 = _dot_nt(q[:, sl], kp[:, sl]) + bias_ref[h, :, 0:kp_rows]
        if mask_first:
            s1 = jnp.where(first, NEG, s1)
        s2 = _dot_nt(q[:, sl], kc[:, sl]) + bias_ref[h, :, kp_rows:]
        m = jnp.maximum(jnp.max(s1, axis=-1, keepdims=True), jnp.max(s2, axis=-1, keepdims=True))
        p1 = jnp.exp(s1 - m)
        p2 = jnp.exp(s2 - m)
        den = jnp.sum(p1, axis=-1, keepdims=True) + jnp.sum(p2, axis=-1, keepdims=True)
        o = (jnp.dot(p1.astype(jnp.bfloat16), vp[:, sl], preferred_element_type=jnp.float32)
             + jnp.dot(p2.astype(jnp.bfloat16), vc[:, sl], preferred_element_type=jnp.float32))
        o_ref[:, sl] = o / den
        l_ref[:, sl] = jnp.broadcast_to(m + jnp.log(den), (qb, ATT_HEAD_DIM))


def _rel_bucket(dist):
    d = np.asarray(dist, dtype=np.int64)
    max_exact = REL_BUCKETS // 2
    ratio = np.log(np.maximum(d, 1) / max_exact) / np.log(REL_MAX_DIST / max_exact)
    large = np.minimum(max_exact + (ratio * (REL_BUCKETS - max_exact)).astype(np.int64), REL_BUCKETS - 1)
    return np.where(d < max_exact, d, large).astype(np.int32)


def _att_bias(rel_table, gi, qb):
    _, dil = ATT_GROUPS[gi]
    steps = np.arange(ATT_BAND + 1)
    tab = rel_table[_rel_bucket(steps * dil)][:, gi * ATT_HPG:(gi + 1) * ATT_HPG].T.astype(jnp.float32)
    qi = np.arange(qb)[:, None]
    col = np.arange(ATT_BAND + qb)[None, :]
    j = qi + ATT_BAND - col
    ok = (j >= 0) & (j <= ATT_BAND)
    jc = np.clip(j, 0, ATT_BAND)
    return jnp.where(jnp.asarray(ok)[None], tab[:, jc], NEG)


def _att_call(args, in_specs, grid, out_rows, out_block_map, qb, mask_first, name):
    blk = pl.BlockSpec((None, qb, ATT_GW), out_block_map)
    shape = jax.ShapeDtypeStruct(out_rows, jnp.float32)
    return pl.pallas_call(
        functools.partial(_att_kernel, kp_rows=ATT_BAND, mask_first=mask_first),
        grid=grid,
        in_specs=in_specs,
        out_specs=[blk, blk],
        out_shape=[shape, shape],
        compiler_params=_cparams("parallel", "parallel", "parallel"),
        name=name,
    )(*args)


def _att_prompt(z, rel_table, gi, B, T):
    _, dil = ATT_GROUPS[gi]
    qb = min(ATT_QBLOCK, T // dil)
    nm = T // dil // qb
    back = qb // ATT_BAND
    if dil == 1:
        zq = zkv = z.reshape(B, T, Z_WIDTH)
        per_q = per_kv = Z_WIDTH // ATT_GW
        cq, ck = Z_Q // ATT_GW + gi, Z_KV // ATT_GW + 2 * gi
    else:
        zq = z[:, Z_Q + gi * ATT_GW:Z_Q + (gi + 1) * ATT_GW].reshape(B, T // dil, dil * ATT_GW)
        zkv = z[:, Z_KV + 2 * gi * ATT_GW:Z_KV + 2 * (gi + 1) * ATT_GW].reshape(B, T // dil, dil * 2 * ATT_GW)
        per_q, per_kv, cq, ck = 1, 2, 0, 0
    cur = lambda per, col: pl.BlockSpec((None, qb, ATT_GW), lambda b, r, m: (b, m, r * per + col))
    prev = lambda per, col: pl.BlockSpec((None, ATT_BAND, ATT_GW),
                                         lambda b, r, m: (b, jnp.maximum(m * back - 1, 0), r * per + col))
    bias = _att_bias(rel_table, gi, qb)
    o, l = _att_call(
        (zq, zkv, zkv, zkv, zkv, bias),
        [cur(per_q, cq), prev(per_kv, ck), prev(per_kv, ck + 1), cur(per_kv, ck), cur(per_kv, ck + 1),
         pl.BlockSpec(bias.shape, lambda b, r, m: (0, 0, 0))],
        (B, dil, nm), (B, T // dil, dil * ATT_GW), lambda b, r, m: (b, m, r), qb, True,
        "att_prompt_g%d" % gi)
    return o.reshape(B * T, ATT_GW), l.reshape(B * T, ATT_GW)


def _att_sample_kernel(q_ref, kvc_ref, c0_ref, c1_ref, c2_ref, bh_ref, bc_ref, o_ref, *, t_valid):
    o_ref[...] = jnp.zeros_like(o_ref)
    for t in range(t_valid):
        outs, lses = [], []
        for gi, c_ref in enumerate((c0_ref, c1_ref, c2_ref)):
            r = t if ATT_GROUPS[gi][1] > 1 else 0
            q = q_ref[t, gi] * (ATT_HEAD_DIM ** -0.5)
            s_h = jnp.sum(c_ref[:, r, 0] * q[None], axis=-1, keepdims=True) + bh_ref[gi, t]
            s_c = jnp.sum(kvc_ref[:, gi, 0] * q[None], axis=-1, keepdims=True) + bc_ref[gi, t]
            m = jnp.maximum(jnp.max(s_h, axis=0, keepdims=True), jnp.max(s_c, axis=0, keepdims=True))
            p_h = jnp.exp(s_h - m)
            p_c = jnp.exp(s_c - m)
            den = jnp.sum(p_h, axis=0, keepdims=True) + jnp.sum(p_c, axis=0, keepdims=True)
            num = (jnp.sum(p_h * c_ref[:, r, 1], axis=0, keepdims=True)
                   + jnp.sum(p_c * kvc_ref[:, gi, 1], axis=0, keepdims=True))
            outs.append(num / den)
            lses.append(m + jnp.log(den))
        lm = jnp.maximum(jnp.maximum(lses[0], lses[1]), lses[2])
        es = [jnp.exp(l - lm) for l in lses]
        y = (es[0] * outs[0] + es[1] * outs[1] + es[2] * outs[2]) / (es[0] + es[1] + es[2])
        o_ref[t] = y[0]


def _att_sample(z, caches, layer, rel_table, B, T, t_valid):
    hd = (ATT_HPG, ATT_HEAD_DIM)
    z3 = z.reshape(B, T, Z_WIDTH)
    q = z3[:, :, Z_Q:Z_Q + 3 * ATT_GW].reshape(B, T, 3, *hd)
    kvc = z3[:, :, Z_KV:Z_KV + 6 * ATT_GW].reshape(B, T, 3, 2, *hd)
    views, cspecs, bh, bc = [], [], [], []
    tq = np.arange(t_valid)[:, None]
    for gi, (width, dil) in enumerate(ATT_GROUPS):
        cache = caches[gi]
        assert cache.shape[2] == width and width == ATT_BAND * dil and (dil == 1 or t_valid <= dil)
        classes = 1 if dil == 1 else t_valid
        views.append(cache.reshape(cache.shape[0], B, ATT_BAND, dil, 2, *hd)[:, :, :, :classes])
        cspecs.append(pl.BlockSpec((None, None, ATT_BAND, classes, 2, *hd),
                                   lambda b: (layer, b, 0, 0, 0, 0, 0)))
        steps = np.arange(ATT_BAND + 1)
        tab = rel_table[_rel_bucket(steps * dil)][:, gi * ATT_HPG:(gi + 1) * ATT_HPG].astype(jnp.float32)
        rows = np.arange(ATT_BAND)[None, :]
        new = np.arange(T)[None, :]
        if dil == 1:
            jh, jc = ATT_BAND + tq - rows, tq - new
        else:
            jh, jc = np.broadcast_to(ATT_BAND - rows, (t_valid, ATT_BAND)), np.where(new == tq, 0, -1)
        for j, dst in ((jh, bh), (jc, bc)):
            ok = (j >= 0) & (j <= ATT_BAND)
            dst.append(jnp.where(jnp.asarray(ok)[:, :, None], tab[np.clip(j, 0, ATT_BAND)], NEG))
    bh = jnp.stack(bh)[..., None]
    bc = jnp.stack(bc)[..., None]
    const = lambda a: pl.BlockSpec(a.shape, lambda b: (0,) * a.ndim, pipeline_mode=pl.Buffered(1))
    y = pl.pallas_call(
        functools.partial(_att_sample_kernel, t_valid=t_valid),
        grid=(B,),
        in_specs=[pl.BlockSpec((None, T, 3, *hd), lambda b: (b, 0, 0, 0, 0)),
                  pl.BlockSpec((None, T, 3, 2, *hd), lambda b: (b, 0, 0, 0, 0, 0)),
                  *cspecs, const(bh), const(bc)],
        out_specs=pl.BlockSpec((None, T, *hd), lambda b: (b, 0, 0, 0)),
        out_shape=jax.ShapeDtypeStruct((B, T, *hd), jnp.float32),
        compiler_params=_cparams("parallel"),
        name="att_sample",
    )(q, kvc, *views, bh, bc)
    return y.reshape(B * T, ATT_GW)


def _merge_kernel(yp_ref, yg_ref, *refs):
    att_refs, (gz_ref, x_ref, wb_ref, wo_ref, g2_ref, x1_ref, hn_ref) = refs[:-7], refs[-7:]
    if len(att_refs) == 1:
        ya = att_refs[0][...]
    else:
        o0_ref, o1_ref, o2_ref, l0_ref, l1_ref, l2_ref = att_refs
        l0, l1, l2 = l0_ref[...], l1_ref[...], l2_ref[...]
        lm = jnp.maximum(jnp.maximum(l0, l1), l2)
        e0, e1, e2 = jnp.exp(l0 - lm), jnp.exp(l1 - lm), jnp.exp(l2 - lm)
        ya = (e0 * o0_ref[...] + e1 * o1_ref[...] + e2 * o2_ref[...]) / (e0 + e1 + e2)
    acc = None
    for bi, y in enumerate((yp_ref[...], yg_ref[...], ya)):
        proj = jnp.dot(y.astype(jnp.bfloat16), wb_ref[bi], preferred_element_type=jnp.float32)
        gz = gz_ref[:, bi * D_MODEL:(bi + 1) * D_MODEL]
        term = proj / (1.0 + jnp.exp(-gz))
        acc = term if acc is None else acc + term
    x1 = x_ref[...] + jnp.dot(acc.astype(jnp.bfloat16), wo_ref[...], preferred_element_type=jnp.float32)
    x1_ref[...] = x1
    ms = jnp.mean(x1 * x1, axis=-1, keepdims=True)
    hn_ref[...] = (x1 * lax.rsqrt(ms + RMS_EPS) * g2_ref[...]).astype(hn_ref.dtype)


def _merge(yp, yg, att, z, x, w_branch, w_out, norm2_g, tm):
    n = x.shape[0]
    half = pl.BlockSpec((tm, 512), lambda i: (i, 0))
    full = pl.BlockSpec((tm, D_MODEL), lambda i: (i, 0))
    if isinstance(att, list):
        (o0, l0), (o1, l1), (o2, l2) = att
        att = (o0, o1, o2, l0, l1, l2)
    else:
        att = (att,)
    return pl.pallas_call(
        _merge_kernel,
        grid=(n // tm,),
        in_specs=[half] * (2 + len(att)) + [
            pl.BlockSpec((tm, 3 * D_MODEL), lambda i: (i, Z_GZ // (3 * D_MODEL))),
            full,
            pl.BlockSpec((3, 512, D_MODEL), lambda i: (0, 0, 0)),
            pl.BlockSpec((D_MODEL, D_MODEL), lambda i: (0, 0)),
            pl.BlockSpec((1, D_MODEL), lambda i: (0, 0))],
        out_specs=[full, full],
        out_shape=[jax.ShapeDtypeStruct((n, D_MODEL), jnp.float32),
                   jax.ShapeDtypeStruct((n, D_MODEL), jnp.bfloat16)],
        compiler_params=_cparams("parallel"),
        name="branch_merge",
    )(yp, yg, *att, z, x, w_branch.astype(jnp.bfloat16),
      w_out.astype(jnp.bfloat16), norm2_g.reshape(1, D_MODEL))


def _peer_select_kernel(q_ref, k1_ref, k2_ref, c1_ref, n2_ref, a2_ref, r2_ref):
    q = q_ref[...].astype(jnp.bfloat16)
    s1 = _dot_nt(k1_ref[...], q[:, :PEER_HALF])
    s2 = _dot_nt(k2_ref[...], q[:, PEER_HALF:])
    none = float(PEER_TOPK)

    def top(vals, count, ranked):
        out, work = [], vals
        rank = jnp.full(vals.shape, none, jnp.float32) if ranked else None
        for a in range(count):
            m = jnp.max(work, axis=0, keepdims=True)
            out.append(m)
            hit = work >= m
            if ranked:
                rank = jnp.where(hit, float(a), rank)
            work = jnp.where(hit, -jnp.inf, work)
        return out, rank

    v1, rank1 = top(s1, PEER_TOPK, True)
    v2, rank2 = top(s2, PEER_TOPK, True)
    rid = lax.broadcasted_iota(jnp.int32, (PEER_TOPK, s2.shape[1]), 0)
    v2all = jnp.zeros((PEER_TOPK, s2.shape[1]), jnp.float32)
    for b, vb in enumerate(v2):
        v2all = jnp.where(rid == b, vb, v2all)
    cand = jnp.concatenate([v1[0] + v2all] + [v1[a] + v2all[:8] for a in range(1, PEER_TOPK)], axis=0)
    best, _ = top(cand, PEER_TOPK, False)
    tau = best[PEER_TOPK - 1]
    den = jnp.zeros_like(tau)
    n2 = jnp.zeros_like(s1)
    for a in range(PEER_TOPK):
        pair = v1[a] + v2all
        keep = pair >= tau
        den = den + jnp.sum(jnp.where(keep, jnp.exp(pair - best[0]), 0.0), axis=0, keepdims=True)
        n2 = jnp.where(rank1 == float(a), jnp.sum(keep.astype(jnp.float32), axis=0, keepdims=True), n2)
    c1_ref[...] = jnp.where(rank1 < none, jnp.exp(s1 - v1[0]) / den, 0.0)
    n2_ref[...] = n2
    a2_ref[...] = jnp.where(rank2 < none, jnp.exp(s2 - v2[0]), 0.0).astype(a2_ref.dtype)
    r2_ref[...] = rank2.astype(r2_ref.dtype)


def _peer_select(q, k1, k2, tn):
    n = q.shape[0]
    keyspec = pl.BlockSpec((None, PEER_NKEYS, PEER_HALF), lambda i, h: (h, 0, 0))
    out = pl.BlockSpec((None, PEER_NKEYS, tn), lambda i, h: (h, 0, i))
    shape = lambda dt: jax.ShapeDtypeStruct((PEER_HEADS, PEER_NKEYS, n), dt)
    return pl.pallas_call(
        _peer_select_kernel,
        grid=(n // tn, PEER_HEADS),
        in_specs=[pl.BlockSpec((tn, 2 * PEER_HALF), lambda i, h: (i, h)), keyspec, keyspec],
        out_specs=[out] * 4,
        out_shape=[shape(jnp.float32), shape(jnp.float32), shape(jnp.bfloat16), shape(jnp.bfloat16)],
        compiler_params=_cparams("parallel", "parallel"),
        name="peer_select",
    )(q, k1.astype(jnp.bfloat16), k2.astype(jnp.bfloat16))


def _peer_weights(st_ref, p_ref, c1_ref, n2_ref, a2_ref, r2_ref, row0, ii):
    tn = st_ref.shape[1]
    sub = 16
    i = row0 + ii
    for lb in range(tn // 128):
        lanes = slice(lb * 128, (lb + 1) * 128)
        row = lambda ref, h: jnp.broadcast_to(ref[h, i:i + 1, lanes], (sub, 128)).astype(jnp.bfloat16)
        n2 = [row(n2_ref, h) for h in range(PEER_HEADS)]
        c1 = [row(c1_ref, h) for h in range(PEER_HEADS)]
        for jb in range(PEER_NKEYS // sub):
            js = slice(jb * sub, (jb + 1) * sub)
            es = slice(ii * PEER_NKEYS + jb * sub, ii * PEER_NKEYS + (jb + 1) * sub)
            s = st_ref[es, lanes]
            act = (0.5 * s * (1.0 + lax.erf(s * (2.0 ** -0.5)))).astype(jnp.bfloat16)
            w = None
            for h in range(PEER_HEADS):
                sel = jnp.where(r2_ref[h, js, lanes] < n2[h], a2_ref[h, js, lanes], jnp.bfloat16(0)) * c1[h]
                w = sel if w is None else w + sel
            p_ref[es, lanes] = w * act


def _peer_expert_kernel(hn_ref, u0_ref, ua_ref, ub_ref, va_ref, vb_ref, c1_ref, n2_ref, a2_ref, r2_ref, o_ref,
                        sta_ref, stb_ref, pa_ref, pb_ref, a2s_ref, r2s_ref):
    k = pl.program_id(1)
    last = pl.num_programs(1) - 1
    sel = (c1_ref, n2_ref, a2s_ref, r2s_ref)

    @pl.when(k == 0)
    def _():
        a2s_ref[...] = a2_ref[...]
        r2s_ref[...] = r2_ref[...]
        o_ref[...] = jnp.zeros_like(o_ref)
        pb_ref[...] = jnp.zeros_like(pb_ref)
        sta_ref[...] = _dot_nt(u0_ref[...], hn_ref[...])

    tq = o_ref.shape[0] // PEER_ROWS

    def stages(p_done, v_ref, u_ref, st_next, st_now, p_now, row0):
        for s in range(PEER_ROWS):
            toks = slice(s * tq, (s + 1) * tq)
            rows = slice(s * PEER_NKEYS, (s + 1) * PEER_NKEYS)
            o_ref[toks, :] += _dot_tn(p_done[:, toks], v_ref[...])
            st_next[rows, :] = _dot_nt(u_ref[rows, :], hn_ref[...])
            _peer_weights(st_now, p_now, *sel, row0, s)

    @pl.when(k < last)
    def _():
        stages(pb_ref, va_ref, ua_ref, stb_ref, sta_ref, pa_ref, 0)
        stages(pa_ref, vb_ref, ub_ref, sta_ref, stb_ref, pb_ref, PEER_ROWS)

    @pl.when(k == last)
    def _():
        o_ref[...] += _dot_tn(pb_ref[...], va_ref[...])


def _peer_experts(hn, u_bf, v_bf, sel, tn):
    n = hn.shape[0]
    ce = PEER_ROWS * PEER_NKEYS
    nch = PEER_EXPERTS // ce
    c1, n2, a2, r2 = sel
    rowblk = pl.BlockSpec((PEER_HEADS, 2 * PEER_ROWS, tn), lambda i, k: (0, jnp.minimum(k, nch // 2 - 1), i))
    allblk = pl.BlockSpec((PEER_HEADS, PEER_NKEYS, tn), lambda i, k: (0, 0, i),
                          pipeline_mode=pl.Buffered(1))
    tok = pl.BlockSpec((tn, D_MODEL), lambda i, k: (i, 0))
    table = lambda chunk: pl.BlockSpec((ce, D_MODEL), lambda i, k: (jnp.clip(chunk(k), 0, nch - 1), 0))
    return pl.pallas_call(
        _peer_expert_kernel,
        grid=(n // tn, nch // 2 + 1),
        in_specs=[tok,
                  table(lambda k: 0), table(lambda k: 2 * k + 1), table(lambda k: 2 * k + 2),
                  table(lambda k: 2 * k - 1), table(lambda k: 2 * k),
                  rowblk, rowblk, allblk, allblk],
        out_specs=tok,
        out_shape=jax.ShapeDtypeStruct((n, D_MODEL), jnp.float32),
        scratch_shapes=[pltpu.VMEM((ce, tn), jnp.float32), pltpu.VMEM((ce, tn), jnp.float32),
                        pltpu.VMEM((ce, tn), jnp.bfloat16), pltpu.VMEM((ce, tn), jnp.bfloat16),
                        pltpu.VMEM((PEER_HEADS, PEER_NKEYS, tn), jnp.bfloat16),
                        pltpu.VMEM((PEER_HEADS, PEER_NKEYS, tn), jnp.bfloat16)],
        compiler_params=_cparams("parallel", "arbitrary"),
        name="peer_experts",
    )(hn, u_bf, u_bf, u_bf, v_bf, v_bf, c1, n2, a2, r2)


def _arrange_w_in(w_in):
    sizes = (512, 256, 256, 512, 512, 16, 1536, 1536, 1536, 3072)
    offs = np.concatenate([[0], np.cumsum(sizes)])
    up, gq, gk, gv, gr, ga, aq, ak, av, gz = [w_in[:, offs[i]:offs[i + 1]] for i in range(10)]
    kv = []
    for g in range(3):
        kv += [ak[:, g * ATT_GW:(g + 1) * ATT_GW], av[:, g * ATT_GW:(g + 1) * ATT_GW]]
    pad = jnp.zeros((w_in.shape[0], Z_WIDTH - Z_GA - GLA_GATE_RANK), w_in.dtype)
    return jnp.concatenate([gz] + kv + [aq, up, gv, gr, gq, gk, ga, pad], axis=1).astype(jnp.bfloat16)


def _layer(x, hn, B, T, t_valid, pos0, pool_hist, gla_s0, caches, layer, rel_table, lw, tiles):
    (w_in_r, w_a2, b_a, w_pool, s_pool, gla_norm_g, w_branch, w_out, norm2_g,
     wq_bf, k1, k2, u_bf, v_bf) = lw
    tm, tt, chunk, tn_sel, tn = tiles
    z = _matmul(hn, w_in_r, tm, 1024, "in_proj")
    hist16 = jnp.pad(pool_hist, ((0, 0), (POOL_HALO - POOL_HIST, 0), (0, 0)))
    y_pool = _pool_mix(z, hist16, w_pool, s_pool, B, T, tt, pos0)
    y_gla, st = _gla(z, jnp.swapaxes(gla_s0, 2, 3), w_a2, b_a, gla_norm_g, B, T, chunk, t_valid)
    if caches is None:
        att = [_att_prompt(z, rel_table, gi, B, T) for gi in range(3)]
    else:
        att = _att_sample(z, caches, layer, rel_table, B, T, t_valid)
    x1, hn2 = _merge(y_pool, y_gla, att, z, x, w_branch, w_out, norm2_g, tm)
    q = _matmul(hn2, wq_bf, tm, 1024, "peer_query")
    sel = _peer_select(q, k1, k2, tn_sel)
    delta = _peer_experts(hn2, u_bf, v_bf, sel, tn)
    z3 = z.reshape(B, T, Z_WIDTH)
    u_new = z3[:, :t_valid, Z_POOL:Z_POOL + POOL_WIDTH]
    pool_tail = jnp.concatenate([pool_hist, u_new], axis=1)[:, -POOL_HIST:]
    kv_new = [z3[:, :t_valid, Z_KV + 2 * ATT_GW * g:Z_KV + 2 * ATT_GW * (g + 1)]
              .reshape(B, t_valid, 2, ATT_HPG, ATT_HEAD_DIM) for g in range(3)]
    return x1, delta, pool_tail, jnp.swapaxes(st, 2, 3), kv_new


def kernel(x_prompt, x_sample, state_pool, state_gla, cache_att1, cache_att2, cache_att3, rel_table,
           norm1_g, w_in, w_a2, b_a, w_pool, s_pool, gla_norm_g, w_branch, w_out, norm2_g,
           peer_wq, peer_k1, peer_k2, peer_u, peer_v, final_norm_g):
    bp, tp, _ = x_prompt.shape
    bs, ts, _ = x_sample.shape
    depth = w_in.shape[0]
    ts_pad = 8
    caches = (cache_att1, cache_att2, cache_att3)
    xp = x_prompt.reshape(bp * tp, D_MODEL)
    xs = jnp.pad(x_sample, ((0, 0), (0, ts_pad - ts), (0, 0))).reshape(bs * ts_pad, D_MODEL)
    tiles_p = (512, 512, GLA_CHUNK, 512, 1024)
    rows_s = min(256, bs * ts_pad)
    tiles_s = (rows_s, ts_pad, ts_pad, rows_s, rows_s)
    pool_p, pool_s, gla_p, gla_s = [], [], [], []
    att_p = [[] for _ in range(3)]
    att_s = [[] for _ in range(3)]
    hp = _rmsnorm(xp, norm1_g[0], jnp.bfloat16, tiles_p[0])
    hs = _rmsnorm(xs, norm1_g[0], jnp.bfloat16, tiles_s[0])
    for l in range(depth):
        lw = (_arrange_w_in(w_in[l]), w_a2[l], b_a[l], w_pool[l], s_pool[l], gla_norm_g[l], w_branch[l],
              w_out[l], norm2_g[l], peer_wq[l].astype(jnp.bfloat16), peer_k1[l], peer_k2[l],
              peer_u[l].astype(jnp.bfloat16), peer_v[l].astype(jnp.bfloat16))
        xp, dp, pt, gp, kvp = _layer(
            xp, hp, bp, tp, tp, 0, jnp.zeros((bp, POOL_HIST, POOL_WIDTH), jnp.float32),
            jnp.zeros((bp, GLA_HEADS, GLA_DK, GLA_DV), jnp.float32), None, l, rel_table, lw, tiles_p)
        xs, ds, ps, gs, kvs = _layer(
            xs, hs, bs, ts_pad, ts, PAST_LEN, state_pool[l], state_gla[l], caches, l, rel_table, lw, tiles_s)
        last = l == depth - 1
        gain = final_norm_g if last else norm1_g[l + 1]
        xp, hp = _add_rmsnorm(xp, dp, gain, jnp.float32 if last else jnp.bfloat16, tiles_p[0])
        xs, hs = _add_rmsnorm(xs, ds, gain, jnp.float32 if last else jnp.bfloat16, tiles_s[0])
        pool_p.append(pt)
        pool_s.append(ps)
        gla_p.append(gp)
        gla_s.append(gs)
        for g, (w, _) in enumerate(ATT_GROUPS):
            att_p[g].append(kvp[g][:, tp - min(w, tp):])
            att_s[g].append(kvs[g])
    y_prompt = hp.reshape(bp, tp, D_MODEL)
    y_sample = hs.reshape(bs, ts_pad, D_MODEL)[:, :ts]
    outs = [y_prompt, y_sample, jnp.stack(pool_p), jnp.stack(pool_s), jnp.stack(gla_p), jnp.stack(gla_s)]
    for g in range(3):
        outs += [jnp.stack(att_p[g]), jnp.stack(att_s[g])]
    return tuple(outs)
```

```python
import functools
import math

import numpy as np
import jax
import jax.numpy as jnp
from jax import lax
from jax.experimental import pallas as pl
from jax.experimental.pallas import tpu as pltpu

D_MODEL = 1024
RMS_EPS = 1e-6
PAST_LEN = 8192

POOL_WINDOWS = (2, 4, 8, 16)
POOL_GROUP = 128
POOL_WIDTH = 512
POOL_HIST = 15
POOL_HALO = 16

GLA_HEADS = 4
GLA_DK = 64
GLA_DV = 128
GLA_KW = 256
GLA_VW = 512
GLA_GATE_RANK = 16
GLA_TAU = 16.0
GLA_CHUNK = 64

ATT_GROUPS = ((128, 1), (512, 4), (2048, 16))
ATT_HPG = 8
ATT_HEAD_DIM = 64
ATT_GW = ATT_HPG * ATT_HEAD_DIM
ATT_BAND = 128
ATT_QBLOCK = 256
REL_BUCKETS = 32
REL_MAX_DIST = 2048
NEG = -1e30

PEER_HEADS = 8
PEER_NKEYS = 128
PEER_EXPERTS = PEER_NKEYS * PEER_NKEYS
PEER_HALF = 128
PEER_TOPK = 16
PEER_ROWS = 8

Z_GZ = 0
Z_KV = 3072
Z_Q = 6144
Z_POOL = 7680
Z_GV = 8192
Z_GR = 8704
Z_GQ = 9216
Z_GK = 9472
Z_GA = 9728
Z_WIDTH = 10240

VMEM_LIMIT = 48 * 1024 * 1024


def _cparams(*sem):
    return pltpu.CompilerParams(dimension_semantics=sem, vmem_limit_bytes=VMEM_LIMIT)


def _dot_nt(a, b):
    return lax.dot_general(a, b, (((1,), (1,)), ((), ())), preferred_element_type=jnp.float32)


def _dot_tn(a, b):
    return lax.dot_general(a, b, (((0,), (0,)), ((), ())), preferred_element_type=jnp.float32)


def _rmsnorm_kernel(x_ref, g_ref, o_ref):
    x = x_ref[...]
    ms = jnp.mean(x * x, axis=-1, keepdims=True)
    o_ref[...] = (x * lax.rsqrt(ms + RMS_EPS) * g_ref[...]).astype(o_ref.dtype)


def _rmsnorm(x, g, out_dtype, tm):
    n, d = x.shape
    return pl.pallas_call(
        _rmsnorm_kernel,
        grid=(n // tm,),
        in_specs=[pl.BlockSpec((tm, d), lambda i: (i, 0)),
                  pl.BlockSpec((1, d), lambda i: (0, 0))],
        out_specs=pl.BlockSpec((tm, d), lambda i: (i, 0)),
        out_shape=jax.ShapeDtypeStruct((n, d), out_dtype),
        compiler_params=_cparams("parallel"),
        name="rmsnorm",
    )(x, g.reshape(1, d))


def _add_rmsnorm_kernel(x_ref, d_ref, g_ref, s_ref, o_ref):
    x = x_ref[...] + d_ref[...]
    s_ref[...] = x
    ms = jnp.mean(x * x, axis=-1, keepdims=True)
    o_ref[...] = (x * lax.rsqrt(ms + RMS_EPS) * g_ref[...]).astype(o_ref.dtype)


def _add_rmsnorm(x, delta, g, out_dtype, tm):
    n, d = x.shape
    blk = pl.BlockSpec((tm, d), lambda i: (i, 0))
    return pl.pallas_call(
        _add_rmsnorm_kernel,
        grid=(n // tm,),
        in_specs=[blk, blk, pl.BlockSpec((1, d), lambda i: (0, 0))],
        out_specs=[blk, blk],
        out_shape=[jax.ShapeDtypeStruct((n, d), jnp.float32), jax.ShapeDtypeStruct((n, d), out_dtype)],
        compiler_params=_cparams("parallel"),
        name="add_rmsnorm",
    )(x, delta, g.reshape(1, d))


def _mm_kernel(a_ref, b_ref, o_ref):
    o_ref[...] = jnp.dot(a_ref[...], b_ref[...], preferred_element_type=jnp.float32)


def _matmul(a, b, tm, tn, name):
    m, k = a.shape
    n = b.shape[1]
    return pl.pallas_call(
        _mm_kernel,
        grid=(n // tn, m // tm),
        in_specs=[pl.BlockSpec((tm, k), lambda j, i: (i, 0)),
                  pl.BlockSpec((k, tn), lambda j, i: (0, j))],
        out_specs=pl.BlockSpec((tm, tn), lambda j, i: (i, j)),
        out_shape=jax.ShapeDtypeStruct((m, n), jnp.float32),
        compiler_params=_cparams("parallel", "parallel"),
        name=name,
    )(a, b)


def _pool_kernel(u_ref, prev_ref, hist_ref, w_ref, s_ref, o_ref, ext_ref, *, tt, pos0):
    t = pl.program_id(1)
    ext_ref[0:POOL_HALO, :] = jnp.where(t == 0, hist_ref[...], prev_ref[...])
    u = u_ref[...]
    ext_ref[POOL_HALO:, :] = u
    pos = (pos0 + t * tt + lax.broadcasted_iota(jnp.int32, (tt, 1), 0)).astype(jnp.float32)
    for gi, w in enumerate(POOL_WINDOWS):
        lo, hi = gi * POOL_GROUP, (gi + 1) * POOL_GROUP
        s = ext_ref[POOL_HALO:POOL_HALO + tt, lo:hi]
        for back in range(1, w):
            s = s + ext_ref[POOL_HALO - back:POOL_HALO - back + tt, lo:hi]
        cnt = jnp.minimum(float(w), pos + 1.0)
        d = s / cnt - u[:, lo:hi]
        y = jnp.dot(d.astype(jnp.bfloat16), w_ref[gi], preferred_element_type=jnp.float32)
        o_ref[:, lo:hi] = y * s_ref[:, lo:hi]


def _pool_mix(z, hist16, w_pool, s_pool, B, T, tt, pos0):
    nt = T // tt
    cb = Z_POOL // POOL_WIDTH

    def prev_map(b, t):
        return (jnp.maximum(b * T + t * tt - POOL_HALO, 0) // POOL_HALO, cb)

    return pl.pallas_call(
        functools.partial(_pool_kernel, tt=tt, pos0=pos0),
        grid=(B, nt),
        in_specs=[pl.BlockSpec((tt, POOL_WIDTH), lambda b, t: (b * nt + t, cb)),
                  pl.BlockSpec((POOL_HALO, POOL_WIDTH), prev_map),
                  pl.BlockSpec((None, POOL_HALO, POOL_WIDTH), lambda b, t: (b, 0, 0)),
                  pl.BlockSpec((4, POOL_GROUP, POOL_GROUP), lambda b, t: (0, 0, 0)),
                  pl.BlockSpec((1, POOL_WIDTH), lambda b, t: (0, 0))],
        out_specs=pl.BlockSpec((tt, POOL_WIDTH), lambda b, t: (b * nt + t, 0)),
        out_shape=jax.ShapeDtypeStruct((B * T, POOL_WIDTH), jnp.float32),
        scratch_shapes=[pltpu.VMEM((POOL_HALO + tt, POOL_WIDTH), jnp.float32)],
        compiler_params=_cparams("parallel", "parallel"),
        name="pool_mix",
    )(z, z, hist16, w_pool.astype(jnp.bfloat16), s_pool.reshape(1, POOL_WIDTH))


def _gla_kernel(q_ref, k_ref, v_ref, r_ref, ga_ref, wa_ref, ba_ref, gn_ref, s0_ref,
                y_ref, sout_ref, st_ref, *, chunk, t_valid):
    c = pl.program_id(1)

    @pl.when(c == 0)
    def _():
        st_ref[...] = s0_ref[...]

    x = jnp.dot(ga_ref[...].astype(jnp.bfloat16), wa_ref[...],
                preferred_element_type=jnp.float32) + ba_ref[...]
    g = -(jnp.maximum(-x, 0.0) + jnp.log1p(jnp.exp(-jnp.abs(x)))) / GLA_TAU
    row = c * chunk + lax.broadcasted_iota(jnp.int32, (chunk, 1), 0)
    valid = row < t_valid
    g = jnp.where(valid, g, 0.0)
    ri = lax.broadcasted_iota(jnp.int32, (chunk, chunk), 0)
    ci = lax.broadcasted_iota(jnp.int32, (chunk, chunk), 1)
    causal = ri >= ci
    b = jnp.dot(causal.astype(jnp.float32), g, preferred_element_type=jnp.float32,
                precision=lax.Precision.HIGHEST)
    b_last = b[chunk - 1:chunk, :]
    k = k_ref[...]
    qe = (q_ref[...] * (GLA_DK ** -0.5) * jnp.exp(b)).astype(jnp.bfloat16)
    ke = (k * jnp.exp(-b)).astype(jnp.bfloat16)
    kl = jnp.where(valid, k * jnp.exp(b_last - b), 0.0).astype(jnp.bfloat16)
    a_last = jnp.exp(b_last)
    v = v_ref[...].astype(jnp.bfloat16)
    r = r_ref[...]
    for h in range(GLA_HEADS):
        ks = slice(h * GLA_DK, (h + 1) * GLA_DK)
        vs = slice(h * GLA_DV, (h + 1) * GLA_DV)
        att = jnp.where(causal, _dot_nt(qe[:, ks], ke[:, ks]), 0.0)
        st = st_ref[h]
        o = (jnp.dot(att.astype(jnp.bfloat16), v[:, vs], preferred_element_type=jnp.float32)
             + _dot_nt(qe[:, ks], st.astype(jnp.bfloat16)))
        st_ref[h] = st * a_last[:, ks] + _dot_tn(v[:, vs], kl[:, ks])
        o = o * lax.rsqrt(jnp.mean(o * o, axis=-1, keepdims=True) + RMS_EPS) * gn_ref[...]
        rh = r[:, vs]
        y_ref[:, vs] = o * (rh / (1.0 + jnp.exp(-rh)))

    @pl.when(c == pl.num_programs(1) - 1)
    def _():
        sout_ref[...] = st_ref[...]


def _gla(z, s0t, w_a2, b_a, gnorm, B, T, chunk, t_valid):
    nc = T // chunk
    wa = jnp.zeros((128, GLA_KW), jnp.float32).at[:GLA_GATE_RANK].set(w_a2).astype(jnp.bfloat16)
    row = lambda b, c: b * nc + c
    return pl.pallas_call(
        functools.partial(_gla_kernel, chunk=chunk, t_valid=t_valid),
        grid=(B, nc),
        in_specs=[pl.BlockSpec((chunk, GLA_KW), lambda b, c: (row(b, c), Z_GQ // GLA_KW)),
                  pl.BlockSpec((chunk, GLA_KW), lambda b, c: (row(b, c), Z_GK // GLA_KW)),
                  pl.BlockSpec((chunk, GLA_VW), lambda b, c: (row(b, c), Z_GV // GLA_VW)),
                  pl.BlockSpec((chunk, GLA_VW), lambda b, c: (row(b, c), Z_GR // GLA_VW)),
                  pl.BlockSpec((chunk, 128), lambda b, c: (row(b, c), Z_GA // 128)),
                  pl.BlockSpec((128, GLA_KW), lambda b, c: (0, 0)),
                  pl.BlockSpec((1, GLA_KW), lambda b, c: (0, 0)),
                  pl.BlockSpec((1, GLA_DV), lambda b, c: (0, 0)),
                  pl.BlockSpec((None, GLA_HEADS, GLA_DV, GLA_DK), lambda b, c: (b, 0, 0, 0))],
        out_specs=[pl.BlockSpec((chunk, GLA_VW), lambda b, c: (row(b, c), 0)),
                   pl.BlockSpec((None, GLA_HEADS, GLA_DV, GLA_DK), lambda b, c: (b, 0, 0, 0))],
        out_shape=[jax.ShapeDtypeStruct((B * T, GLA_VW), jnp.float32),
                   jax.ShapeDtypeStruct((B, GLA_HEADS, GLA_DV, GLA_DK), jnp.float32)],
        scratch_shapes=[pltpu.VMEM((GLA_HEADS, GLA_DV, GLA_DK), jnp.float32)],
        compiler_params=_cparams("parallel", "arbitrary"),
        name="gla_scan",
    )(z, z, z, z, z, wa, b_a.reshape(1, GLA_KW), gnorm.reshape(1, GLA_DV), s0t)


def _att_kernel(q_ref, kp_ref, vp_ref, kc_ref, vc_ref, bias_ref, o_ref, l_ref, *, kp_rows, mask_first):
    q = (q_ref[...] * (ATT_HEAD_DIM ** -0.5)).astype(jnp.bfloat16)
    kp = kp_ref[...].astype(jnp.bfloat16)
    vp = vp_ref[...].astype(jnp.bfloat16)
    kc = kc_ref[...].astype(jnp.bfloat16)
    vc = vc_ref[...].astype(jnp.bfloat16)
    qb = q.shape[0]
    first = pl.program_id(2) == 0
    for h in range(ATT_HPG):
        sl = slice(h * ATT_HEAD_DIM, (h + 1) * ATT_HEAD_DIM)
        s1 = _dot_nt(q[:, sl], kp[:, sl]) + bias_ref[h, :, 0:kp_rows]
        if mask_first:
            s1 = jnp.where(first, NEG, s1)
        s2 = _dot_nt(q[:, sl], kc[:, sl]) + bias_ref[h, :, kp_rows:]
        m = jnp.maximum(jnp.max(s1, axis=-1, keepdims=True), jnp.max(s2, axis=-1, keepdims=True))
        p1 = jnp.exp(s1 - m)
        p2 = jnp.exp(s2 - m)
        den = jnp.sum(p1, axis=-1, keepdims=True) + jnp.sum(p2, axis=-1, keepdims=True)
        o = (jnp.dot(p1.astype(jnp.bfloat16), vp[:, sl], preferred_element_type=jnp.float32)
             + jnp.dot(p2.astype(jnp.bfloat16), vc[:, sl], preferred_element_type=jnp.float32))
        o_ref[:, sl] = o / den
        l_ref[:, sl] = jnp.broadcast_to(m + jnp.log(den), (qb, ATT_HEAD_DIM))


def _rel_bucket(dist):
    d = np.asarray(dist, dtype=np.int64)
    max_exact = REL_BUCKETS // 2
    ratio = np.log(np.maximum(d, 1) / max_exact) / np.log(REL_MAX_DIST / max_exact)
    large = np.minimum(max_exact + (ratio * (REL_BUCKETS - max_exact)).astype(np.int64), REL_BUCKETS - 1)
    return np.where(d < max_exact, d, large).astype(np.int32)


def _att_bias(rel_table, gi, qb):
    _, dil = ATT_GROUPS[gi]
    cols = ATT_BAND + qb
    back = np.arange(ATT_BAND, -1, -1)
    tab = rel_table[_rel_bucket(back * dil)][:, gi * ATT_HPG:(gi + 1) * ATT_HPG].T.astype(jnp.float32)
    v = jnp.concatenate([tab, jnp.full((ATT_HPG, cols - ATT_BAND), NEG, jnp.float32)], axis=1)
    return jnp.tile(v, (1, qb))[:, :qb * cols].reshape(ATT_HPG, qb, cols)


def _att_call(args, in_specs, grid, out_rows, out_block_map, qb, mask_first, name):
    blk = pl.BlockSpec((None, qb, ATT_GW), out_block_map)
    shape = jax.ShapeDtypeStruct(out_rows, jnp.float32)
    return pl.pallas_call(
        functools.partial(_att_kernel, kp_rows=ATT_BAND, mask_first=mask_first),
        grid=grid,
        in_specs=in_specs,
        out_specs=[blk, blk],
        out_shape=[shape, shape],
        compiler_params=_cparams("parallel", "parallel", "parallel"),
        name=name,
    )(*args)


def _att_prompt(z, rel_table, gi, B, T):
    _, dil = ATT_GROUPS[gi]
    qb = min(ATT_QBLOCK, T // dil)
    nm = T // dil // qb
    back = qb // ATT_BAND
    if dil == 1:
        zq = zkv = z.reshape(B, T, Z_WIDTH)
        per_q = per_kv = Z_WIDTH // ATT_GW
        cq, ck = Z_Q // ATT_GW + gi, Z_KV // ATT_GW + 2 * gi
    else:
        zq = z[:, Z_Q + gi * ATT_GW:Z_Q + (gi + 1) * ATT_GW].reshape(B, T // dil, dil * ATT_GW)
        zkv = z[:, Z_KV + 2 * gi * ATT_GW:Z_KV + 2 * (gi + 1) * ATT_GW].reshape(B, T // dil, dil * 2 * ATT_GW)
        per_q, per_kv, cq, ck = 1, 2, 0, 0
    cur = lambda per, col: pl.BlockSpec((None, qb, ATT_GW), lambda b, r, m: (b, m, r * per + col))
    prev = lambda per, col: pl.BlockSpec((None, ATT_BAND, ATT_GW),
                                         lambda b, r, m: (b, jnp.maximum(m * back - 1, 0), r * per + col))
    bias = _att_bias(rel_table, gi, qb)
    o, l = _att_call(
        (zq, zkv, zkv, zkv, zkv, bias),
        [cur(per_q, cq), prev(per_kv, ck), prev(per_kv, ck + 1), cur(per_kv, ck), cur(per_kv, ck + 1),
         pl.BlockSpec(bias.shape, lambda b, r, m: (0, 0, 0))],
        (B, dil, nm), (B, T // dil, dil * ATT_GW), lambda b, r, m: (b, m, r), qb, True,
        "att_prompt_g%d" % gi)
    return o.reshape(B * T, ATT_GW), l.reshape(B * T, ATT_GW)


def _att_sample_kernel(q_ref, kv_ref, c0_ref, c1_ref, c2_ref, b0_ref, b1_ref, b2_ref, bc_ref, o_ref):
    bf = jnp.bfloat16
    for h in range(ATT_HPG):
        hs = slice(h * ATT_HEAD_DIM, (h + 1) * ATT_HEAD_DIM)
        outs, lses = [], []
        for gi, (c_ref, bh_ref) in enumerate(((c0_ref, b0_ref), (c1_ref, b1_ref), (c2_ref, b2_ref))):
            head = lambda ref, blk: ref[:, blk * ATT_GW + hs.start:blk * ATT_GW + hs.stop]
            q = (head(q_ref, gi) * (ATT_HEAD_DIM ** -0.5)).astype(bf)
            kc = head(kv_ref, 2 * gi).astype(bf)
            vc = head(kv_ref, 2 * gi + 1).astype(bf)
            s_h = jnp.dot(q, c_ref[0, h].astype(bf), preferred_element_type=jnp.float32) + bh_ref[h]
            s_c = _dot_nt(q, kc) + bc_ref[gi, h]
            m = jnp.maximum(jnp.max(s_h, axis=-1, keepdims=True), jnp.max(s_c, axis=-1, keepdims=True))
            p_h = jnp.exp(s_h - m)
            p_c = jnp.exp(s_c - m)
            den = jnp.sum(p_h, axis=-1, keepdims=True) + jnp.sum(p_c, axis=-1, keepdims=True)
            num = (_dot_nt(p_h.astype(bf), c_ref[1, h].astype(bf))
                   + jnp.dot(p_c.astype(bf), vc, preferred_element_type=jnp.float32))
            outs.append(num / den)
            lses.append(m + jnp.log(den))
        lm = jnp.maximum(jnp.maximum(lses[0], lses[1]), lses[2])
        es = [jnp.exp(l - lm) for l in lses]
        o_ref[:, hs] = (es[0] * outs[0] + es[1] * outs[1] + es[2] * outs[2]) / (es[0] + es[1] + es[2])


def _att_sample(z, caches, layer, rel_table, B, T):
    z3 = z.reshape(B, T, Z_WIDTH)
    views, cspecs, bh, bc = [], [], [], []
    tq = np.arange(T)[:, None]
    for gi, (width, dil) in enumerate(ATT_GROUPS):
        cache = caches[gi]
        assert cache.shape[2] == width
        views.append(jnp.transpose(cache, (0, 1, 3, 4, 5, 2)))
        cspecs.append(pl.BlockSpec((None, None, 2, ATT_HPG, ATT_HEAD_DIM, width),
                                   lambda b: (layer, b, 0, 0, 0, 0)))
        steps = np.arange(ATT_BAND + 1)
        tab = rel_table[_rel_bucket(steps * dil)][:, gi * ATT_HPG:(gi + 1) * ATT_HPG].T.astype(jnp.float32)
        for dist, dst in ((width + tq - np.arange(width)[None, :], bh), (tq - np.arange(T)[None, :], bc)):
            ok = (dist >= 0) & (dist % dil == 0) & (dist // dil <= ATT_BAND)
            j = np.where(ok, dist // dil, 0)
            dst.append(jnp.where(jnp.asarray(ok)[None], tab[:, j], NEG))
    bc = jnp.stack(bc)
    const = lambda a: pl.BlockSpec(a.shape, lambda b: (0,) * a.ndim, pipeline_mode=pl.Buffered(1))
    y = pl.pallas_call(
        _att_sample_kernel,
        grid=(B,),
        in_specs=[pl.BlockSpec((None, T, 3 * ATT_GW), lambda b: (b, 0, Z_Q // (3 * ATT_GW))),
                  pl.BlockSpec((None, T, 6 * ATT_GW), lambda b: (b, 0, Z_KV // (6 * ATT_GW))),
                  *cspecs, *[const(a) for a in bh], const(bc)],
        out_specs=pl.BlockSpec((None, T, ATT_GW), lambda b: (b, 0, 0)),
        out_shape=jax.ShapeDtypeStruct((B, T, ATT_GW), jnp.float32),
        compiler_params=_cparams("parallel"),
        name="att_sample",
    )(z3, z3, *views, *bh, bc)
    return y.reshape(B * T, ATT_GW)


def _merge_kernel(yp_ref, yg_ref, *refs):
    att_refs, (gz_ref, x_ref, wb_ref, wo_ref, g2_ref, x1_ref, hn_ref) = refs[:-7], refs[-7:]
    if len(att_refs) == 1:
        ya = att_refs[0][...]
    else:
        o0_ref, o1_ref, o2_ref, l0_ref, l1_ref, l2_ref = att_refs
        l0, l1, l2 = l0_ref[...], l1_ref[...], l2_ref[...]
        lm = jnp.maximum(jnp.maximum(l0, l1), l2)
        e0, e1, e2 = jnp.exp(l0 - lm), jnp.exp(l1 - lm), jnp.exp(l2 - lm)
        ya = (e0 * o0_ref[...] + e1 * o1_ref[...] + e2 * o2_ref[...]) / (e0 + e1 + e2)
    acc = None
    for bi, y in enumerate((yp_ref[...], yg_ref[...], ya)):
        proj = jnp.dot(y.astype(jnp.bfloat16), wb_ref[bi], preferred_element_type=jnp.float32)
        gz = gz_ref[:, bi * D_MODEL:(bi + 1) * D_MODEL]
        term = proj / (1.0 + jnp.exp(-gz))
        acc = term if acc is None else acc + term
    x1 = x_ref[...] + jnp.dot(acc.astype(jnp.bfloat16), wo_ref[...], preferred_element_type=jnp.float32)
    x1_ref[...] = x1
    ms = jnp.mean(x1 * x1, axis=-1, keepdims=True)
    hn_ref[...] = (x1 * lax.rsqrt(ms + RMS_EPS) * g2_ref[...]).astype(hn_ref.dtype)


def _merge(yp, yg, att, z, x, w_branch, w_out, norm2_g, tm):
    n = x.shape[0]
    half = pl.BlockSpec((tm, 512), lambda i: (i, 0))
    full = pl.BlockSpec((tm, D_MODEL), lambda i: (i, 0))
    if isinstance(att, list):
        (o0, l0), (o1, l1), (o2, l2) = att
        att = (o0, o1, o2, l0, l1, l2)
    else:
        att = (att,)
    return pl.pallas_call(
        _merge_kernel,
        grid=(n // tm,),
        in_specs=[half] * (2 + len(att)) + [
            pl.BlockSpec((tm, 3 * D_MODEL), lambda i: (i, Z_GZ // (3 * D_MODEL))),
            full,
            pl.BlockSpec((3, 512, D_MODEL), lambda i: (0, 0, 0)),
            pl.BlockSpec((D_MODEL, D_MODEL), lambda i: (0, 0)),
            pl.BlockSpec((1, D_MODEL), lambda i: (0, 0))],
        out_specs=[full, full],
        out_shape=[jax.ShapeDtypeStruct((n, D_MODEL), jnp.float32),
                   jax.ShapeDtypeStruct((n, D_MODEL), jnp.bfloat16)],
        compiler_params=_cparams("parallel"),
        name="branch_merge",
    )(yp, yg, *att, z, x, w_branch.astype(jnp.bfloat16),
      w_out.astype(jnp.bfloat16), norm2_g.reshape(1, D_MODEL))


def _peer_select_kernel(q_ref, k1_ref, k2_ref, c1_ref, n2_ref, a2_ref, r2_ref):
    q = q_ref[...].astype(jnp.bfloat16)
    s1 = _dot_nt(k1_ref[...], q[:, :PEER_HALF])
    s2 = _dot_nt(k2_ref[...], q[:, PEER_HALF:])
    none = float(PEER_TOPK)

    def top(vals, count, ranked):
        out, work = [], vals
        rank = jnp.full(vals.shape, none, jnp.float32) if ranked else None
        for a in range(count):
            m = jnp.max(work, axis=0, keepdims=True)
            out.append(m)
            hit = work >= m
            if ranked:
                rank = jnp.where(hit, float(a), rank)
            work = jnp.where(hit, -jnp.inf, work)
        return out, rank

    v1, rank1 = top(s1, PEER_TOPK, True)
    v2, rank2 = top(s2, PEER_TOPK, True)
    rid = lax.broadcasted_iota(jnp.int32, (PEER_TOPK, s2.shape[1]), 0)
    v2all = jnp.zeros((PEER_TOPK, s2.shape[1]), jnp.float32)
    for b, vb in enumerate(v2):
        v2all = jnp.where(rid == b, vb, v2all)
    cand = jnp.concatenate([v1[0] + v2all] + [v1[a] + v2all[:8] for a in range(1, PEER_TOPK)], axis=0)
    best, _ = top(cand, PEER_TOPK, False)
    tau = best[PEER_TOPK - 1]
    den = jnp.zeros_like(tau)
    n2 = jnp.zeros_like(s1)
    for a in range(PEER_TOPK):
        pair = v1[a] + v2all
        keep = pair >= tau
        den = den + jnp.sum(jnp.where(keep, jnp.exp(pair - best[0]), 0.0), axis=0, keepdims=True)
        n2 = jnp.where(rank1 == float(a), jnp.sum(keep.astype(jnp.float32), axis=0, keepdims=True), n2)
    c1_ref[...] = jnp.where(rank1 < none, jnp.exp(s1 - v1[0]) / den, 0.0)
    n2_ref[...] = n2
    a2_ref[...] = jnp.where(rank2 < none, jnp.exp(s2 - v2[0]), 0.0).astype(a2_ref.dtype)
    r2_ref[...] = rank2.astype(r2_ref.dtype)


def _peer_select(q, k1, k2, tn):
    n = q.shape[0]
    keyspec = pl.BlockSpec((None, PEER_NKEYS, PEER_HALF), lambda i, h: (h, 0, 0))
    out = pl.BlockSpec((None, PEER_NKEYS, tn), lambda i, h: (h, 0, i))
    shape = lambda dt: jax.ShapeDtypeStruct((PEER_HEADS, PEER_NKEYS, n), dt)
    return pl.pallas_call(
        _peer_select_kernel,
        grid=(n // tn, PEER_HEADS),
        in_specs=[pl.BlockSpec((tn, 2 * PEER_HALF), lambda i, h: (i, h)), keyspec, keyspec],
        out_specs=[out] * 4,
        out_shape=[shape(jnp.float32), shape(jnp.float32), shape(jnp.bfloat16), shape(jnp.bfloat16)],
        compiler_params=_cparams("parallel", "parallel"),
        name="peer_select",
    )(q, k1.astype(jnp.bfloat16), k2.astype(jnp.bfloat16))


def _peer_expert_kernel(hn_ref, u_ref, v_ref, c1_ref, n2_ref, a2_ref, r2_ref, o_ref,
                        st_ref, p_ref, a2s_ref, r2s_ref):
    c = pl.program_id(1)

    @pl.when(c == 0)
    def _():
        o_ref[...] = jnp.zeros_like(o_ref)
        a2s_ref[...] = a2_ref[...]
        r2s_ref[...] = r2_ref[...]

    st_ref[...] = _dot_nt(u_ref[...], hn_ref[...])
    tn = st_ref.shape[1]
    sub = 16
    for ii in range(PEER_ROWS):
        for lb in range(tn // 128):
            lanes = slice(lb * 128, (lb + 1) * 128)
            row = lambda ref, h: jnp.broadcast_to(ref[h, ii:ii + 1, lanes], (sub, 128)).astype(jnp.bfloat16)
            n2 = [row(n2_ref, h) for h in range(PEER_HEADS)]
            c1 = [row(c1_ref, h) for h in range(PEER_HEADS)]
            for jb in range(PEER_NKEYS // sub):
                js = slice(jb * sub, (jb + 1) * sub)
                es = slice(ii * PEER_NKEYS + jb * sub, ii * PEER_NKEYS + (jb + 1) * sub)
                s = st_ref[es, lanes]
                act = (0.5 * s * (1.0 + lax.erf(s * (2.0 ** -0.5)))).astype(jnp.bfloat16)
                w = None
                for h in range(PEER_HEADS):
                    sel = jnp.where(r2s_ref[h, js, lanes] < n2[h], a2s_ref[h, js, lanes], jnp.bfloat16(0)) * c1[h]
                    w = sel if w is None else w + sel
                p_ref[es, lanes] = w * act
    o_ref[...] += _dot_tn(p_ref[...], v_ref[...])


def _peer_experts(hn, u_bf, v_bf, sel, tn):
    n = hn.shape[0]
    ce = PEER_ROWS * PEER_NKEYS
    c1, n2, a2, r2 = sel
    rowblk = pl.BlockSpec((PEER_HEADS, PEER_ROWS, tn), lambda i, c: (0, c, i))
    allblk = pl.BlockSpec((PEER_HEADS, PEER_NKEYS, tn), lambda i, c: (0, 0, i),
                          pipeline_mode=pl.Buffered(1))
    tok = pl.BlockSpec((tn, D_MODEL), lambda i, c: (i, 0))
    table = pl.BlockSpec((ce, D_MODEL), lambda i, c: (c, 0))
    return pl.pallas_call(
        _peer_expert_kernel,
        grid=(n // tn, PEER_EXPERTS // ce),
        in_specs=[tok, table, table, rowblk, rowblk, allblk, allblk],
        out_specs=tok,
        out_shape=jax.ShapeDtypeStruct((n, D_MODEL), jnp.float32),
        scratch_shapes=[pltpu.VMEM((ce, tn), jnp.float32),
                        pltpu.VMEM((ce, tn), jnp.bfloat16),
                        pltpu.VMEM((PEER_HEADS, PEER_NKEYS, tn), jnp.bfloat16),
                        pltpu.VMEM((PEER_HEADS, PEER_NKEYS, tn), jnp.bfloat16)],
        compiler_params=_cparams("parallel", "arbitrary"),
        name="peer_experts",
    )(hn, u_bf, v_bf, c1, n2, a2, r2)


def _arrange_w_in(w_in):
    sizes = (512, 256, 256, 512, 512, 16, 1536, 1536, 1536, 3072)
    offs = np.concatenate([[0], np.cumsum(sizes)])
    up, gq, gk, gv, gr, ga, aq, ak, av, gz = [w_in[:, offs[i]:offs[i + 1]] for i in range(10)]
    kv = []
    for g in range(3):
        kv += [ak[:, g * ATT_GW:(g + 1) * ATT_GW], av[:, g * ATT_GW:(g + 1) * ATT_GW]]
    pad = jnp.zeros((w_in.shape[0], Z_WIDTH - Z_GA - GLA_GATE_RANK), w_in.dtype)
    return jnp.concatenate([gz] + kv + [aq, up, gv, gr, gq, gk, ga, pad], axis=1).astype(jnp.bfloat16)


def _layer(x, hn, B, T, t_valid, pos0, pool_hist, gla_s0, caches, layer, rel_table, lw, tiles):
    (w_in_r, w_a2, b_a, w_pool, s_pool, gla_norm_g, w_branch, w_out, norm2_g,
     wq_bf, k1, k2, u_bf, v_bf) = lw
    tm, tt, chunk, tn_sel, tn = tiles
    z = _matmul(hn, w_in_r, tm, 1024, "in_proj")
    hist16 = jnp.pad(pool_hist, ((0, 0), (POOL_HALO - POOL_HIST, 0), (0, 0)))
    y_pool = _pool_mix(z, hist16, w_pool, s_pool, B, T, tt, pos0)
    y_gla, st = _gla(z, jnp.swapaxes(gla_s0, 2, 3), w_a2, b_a, gla_norm_g, B, T, chunk, t_valid)
    if caches is None:
        att = [_att_prompt(z, rel_table, gi, B, T) for gi in range(3)]
    else:
        att = _att_sample(z, caches, layer, rel_table, B, T)
    x1, hn2 = _merge(y_pool, y_gla, att, z, x, w_branch, w_out, norm2_g, tm)
    q = _matmul(hn2, wq_bf, tm, 1024, "peer_query")
    sel = _peer_select(q, k1, k2, tn_sel)
    delta = _peer_experts(hn2, u_bf, v_bf, sel, tn)
    z3 = z.reshape(B, T, Z_WIDTH)
    u_new = z3[:, :t_valid, Z_POOL:Z_POOL + POOL_WIDTH]
    pool_tail = jnp.concatenate([pool_hist, u_new], axis=1)[:, -POOL_HIST:]
    kv_new = [z3[:, :t_valid, Z_KV + 2 * ATT_GW * g:Z_KV + 2 * ATT_GW * (g + 1)]
              .reshape(B, t_valid, 2, ATT_HPG, ATT_HEAD_DIM) for g in range(3)]
    return x1, delta, pool_tail, jnp.swapaxes(st, 2, 3), kv_new


def kernel(x_prompt, x_sample, state_pool, state_gla, cache_att1, cache_att2, cache_att3, rel_table,
           norm1_g, w_in, w_a2, b_a, w_pool, s_pool, gla_norm_g, w_branch, w_out, norm2_g,
           peer_wq, peer_k1, peer_k2, peer_u, peer_v, final_norm_g):
    bp, tp, _ = x_prompt.shape
    bs, ts, _ = x_sample.shape
    depth = w_in.shape[0]
    ts_pad = 8
    caches = (cache_att1, cache_att2, cache_att3)
    xp = x_prompt.reshape(bp * tp, D_MODEL)
    xs = jnp.pad(x_sample, ((0, 0), (0, ts_pad - ts), (0, 0))).reshape(bs * ts_pad, D_MODEL)
    tiles_p = (512, 512, GLA_CHUNK, 512, 1024)
    rows_s = min(256, bs * ts_pad)
    tiles_s = (rows_s, ts_pad, ts_pad, rows_s, rows_s)
    pool_p, pool_s, gla_p, gla_s = [], [], [], []
    att_p = [[] for _ in range(3)]
    att_s = [[] for _ in range(3)]
    hp = _rmsnorm(xp, norm1_g[0], jnp.bfloat16, tiles_p[0])
    hs = _rmsnorm(xs, norm1_g[0], jnp.bfloat16, tiles_s[0])
    for l in range(depth):
        lw = (_arrange_w_in(w_in[l]), w_a2[l], b_a[l], w_pool[l], s_pool[l], gla_norm_g[l], w_branch[l],
              w_out[l], norm2_g[l], peer_wq[l].astype(jnp.bfloat16), peer_k1[l], peer_k2[l],
              peer_u[l].astype(jnp.bfloat16), peer_v[l].astype(jnp.bfloat16))
        xp, dp, pt, gp, kvp = _layer(
            xp, hp, bp, tp, tp, 0, jnp.zeros((bp, POOL_HIST, POOL_WIDTH), jnp.float32),
            jnp.zeros((bp, GLA_HEADS, GLA_DK, GLA_DV), jnp.float32), None, l, rel_table, lw, tiles_p)
        xs, ds, ps, gs, kvs = _layer(
            xs, hs, bs, ts_pad, ts, PAST_LEN, state_pool[l], state_gla[l], caches, l, rel_table, lw, tiles_s)
        last = l == depth - 1
        gain = final_norm_g if last else norm1_g[l + 1]
        xp, hp = _add_rmsnorm(xp, dp, gain, jnp.float32 if last else jnp.bfloat16, tiles_p[0])
        xs, hs = _add_rmsnorm(xs, ds, gain, jnp.float32 if last else jnp.bfloat16, tiles_s[0])
        pool_p.append(pt)
        pool_s.append(ps)
        gla_p.append(gp)
        gla_s.append(gs)
        for g, (w, _) in enumerate(ATT_GROUPS):
            att_p[g].append(kvp[g][:, tp - min(w, tp):])
            att_s[g].append(kvs[g])
    y_prompt = hp.reshape(bp, tp, D_MODEL)
    y_sample = hs.reshape(bs, ts_pad, D_MODEL)[:, :ts]
    outs = [y_prompt, y_sample, jnp.stack(pool_p), jnp.stack(pool_s), jnp.stack(gla_p), jnp.stack(gla_s)]
    for g in range(3):
        outs += [jnp.stack(att_p[g]), jnp.stack(att_s[g])]
    return tuple(outs)
```

```python
import functools
import math

import numpy as np
import jax
import jax.numpy as jnp
from jax import lax
from jax.experimental import pallas as pl
from jax.experimental.pallas import tpu as pltpu

D_MODEL = 1024
RMS_EPS = 1e-6
PAST_LEN = 8192

POOL_WINDOWS = (2, 4, 8, 16)
POOL_GROUP = 128
POOL_WIDTH = 512
POOL_HIST = 15
POOL_HALO = 16

GLA_HEADS = 4
GLA_DK = 64
GLA_DV = 128
GLA_KW = 256
GLA_VW = 512
GLA_GATE_RANK = 16
GLA_TAU = 16.0
GLA_CHUNK = 64

ATT_GROUPS = ((128, 1), (512, 4), (2048, 16))
ATT_HPG = 8
ATT_HEAD_DIM = 64
ATT_GW = ATT_HPG * ATT_HEAD_DIM
ATT_BAND = 128
ATT_QBLOCK = 256
REL_BUCKETS = 32
REL_MAX_DIST = 2048
NEG = -1e30

PEER_HEADS = 8
PEER_NKEYS = 128
PEER_EXPERTS = PEER_NKEYS * PEER_NKEYS
PEER_HALF = 128
PEER_TOPK = 16
PEER_ROWS = 8

Z_GZ = 0
Z_KV = 3072
Z_Q = 6144
Z_POOL = 7680
Z_GV = 8192
Z_GR = 8704
Z_GQ = 9216
Z_GK = 9472
Z_GA = 9728
Z_WIDTH = 10240

VMEM_LIMIT = 48 * 1024 * 1024


def _cparams(*sem):
    return pltpu.CompilerParams(dimension_semantics=sem, vmem_limit_bytes=VMEM_LIMIT)


def _dot_nt(a, b):
    return lax.dot_general(a, b, (((1,), (1,)), ((), ())), preferred_element_type=jnp.float32)


def _dot_tn(a, b):
    return lax.dot_general(a, b, (((0,), (0,)), ((), ())), preferred_element_type=jnp.float32)


def _rmsnorm_kernel(x_ref, g_ref, o_ref):
    x = x_ref[...]
    ms = jnp.mean(x * x, axis=-1, keepdims=True)
    o_ref[...] = (x * lax.rsqrt(ms + RMS_EPS) * g_ref[...]).astype(o_ref.dtype)


def _rmsnorm(x, g, out_dtype, tm):
    n, d = x.shape
    return pl.pallas_call(
        _rmsnorm_kernel,
        grid=(n // tm,),
        in_specs=[pl.BlockSpec((tm, d), lambda i: (i, 0)),
                  pl.BlockSpec((1, d), lambda i: (0, 0))],
        out_specs=pl.BlockSpec((tm, d), lambda i: (i, 0)),
        out_shape=jax.ShapeDtypeStruct((n, d), out_dtype),
        compiler_params=_cparams("parallel"),
        name="rmsnorm",
    )(x, g.reshape(1, d))


def _add_rmsnorm_kernel(x_ref, d_ref, g_ref, s_ref, o_ref):
    x = x_ref[...] + d_ref[...]
    s_ref[...] = x
    ms = jnp.mean(x * x, axis=-1, keepdims=True)
    o_ref[...] = (x * lax.rsqrt(ms + RMS_EPS) * g_ref[...]).astype(o_ref.dtype)


def _add_rmsnorm(x, delta, g, out_dtype, tm):
    n, d = x.shape
    blk = pl.BlockSpec((tm, d), lambda i: (i, 0))
    return pl.pallas_call(
        _add_rmsnorm_kernel,
        grid=(n // tm,),
        in_specs=[blk, blk, pl.BlockSpec((1, d), lambda i: (0, 0))],
        out_specs=[blk, blk],
        out_shape=[jax.ShapeDtypeStruct((n, d), jnp.float32), jax.ShapeDtypeStruct((n, d), out_dtype)],
        compiler_params=_cparams("parallel"),
        name="add_rmsnorm",
    )(x, delta, g.reshape(1, d))


def _mm_kernel(a_ref, b_ref, o_ref):
    o_ref[...] = jnp.dot(a_ref[...], b_ref[...], preferred_element_type=jnp.float32)


def _matmul(a, b, tm, tn, name):
    m, k = a.shape
    n = b.shape[1]
    return pl.pallas_call(
        _mm_kernel,
        grid=(n // tn, m // tm),
        in_specs=[pl.BlockSpec((tm, k), lambda j, i: (i, 0)),
                  pl.BlockSpec((k, tn), lambda j, i: (0, j))],
        out_specs=pl.BlockSpec((tm, tn), lambda j, i: (i, j)),
        out_shape=jax.ShapeDtypeStruct((m, n), jnp.float32),
        compiler_params=_cparams("parallel", "parallel"),
        name=name,
    )(a, b)


def _pool_kernel(u_ref, prev_ref, hist_ref, w_ref, s_ref, o_ref, ext_ref, *, tt, pos0):
    t = pl.program_id(1)
    ext_ref[0:POOL_HALO, :] = jnp.where(t == 0, hist_ref[...], prev_ref[...])
    u = u_ref[...]
    ext_ref[POOL_HALO:, :] = u
    pos = (pos0 + t * tt + lax.broadcasted_iota(jnp.int32, (tt, 1), 0)).astype(jnp.float32)
    for gi, w in enumerate(POOL_WINDOWS):
        lo, hi = gi * POOL_GROUP, (gi + 1) * POOL_GROUP
        s = ext_ref[POOL_HALO:POOL_HALO + tt, lo:hi]
        for back in range(1, w):
            s = s + ext_ref[POOL_HALO - back:POOL_HALO - back + tt, lo:hi]
        cnt = jnp.minimum(float(w), pos + 1.0)
        d = s / cnt - u[:, lo:hi]
        y = jnp.dot(d.astype(jnp.bfloat16), w_ref[gi], preferred_element_type=jnp.float32)
        o_ref[:, lo:hi] = y * s_ref[:, lo:hi]


def _pool_mix(z, hist16, w_pool, s_pool, B, T, tt, pos0):
    nt = T // tt
    cb = Z_POOL // POOL_WIDTH

    def prev_map(b, t):
        return (jnp.maximum(b * T + t * tt - POOL_HALO, 0) // POOL_HALO, cb)

    return pl.pallas_call(
        functools.partial(_pool_kernel, tt=tt, pos0=pos0),
        grid=(B, nt),
        in_specs=[pl.BlockSpec((tt, POOL_WIDTH), lambda b, t: (b * nt + t, cb)),
                  pl.BlockSpec((POOL_HALO, POOL_WIDTH), prev_map),
                  pl.BlockSpec((None, POOL_HALO, POOL_WIDTH), lambda b, t: (b, 0, 0)),
                  pl.BlockSpec((4, POOL_GROUP, POOL_GROUP), lambda b, t: (0, 0, 0)),
                  pl.BlockSpec((1, POOL_WIDTH), lambda b, t: (0, 0))],
        out_specs=pl.BlockSpec((tt, POOL_WIDTH), lambda b, t: (b * nt + t, 0)),
        out_shape=jax.ShapeDtypeStruct((B * T, POOL_WIDTH), jnp.float32),
        scratch_shapes=[pltpu.VMEM((POOL_HALO + tt, POOL_WIDTH), jnp.float32)],
        compiler_params=_cparams("parallel", "parallel"),
        name="pool_mix",
    )(z, z, hist16, w_pool.astype(jnp.bfloat16), s_pool.reshape(1, POOL_WIDTH))


def _gla_kernel(q_ref, k_ref, v_ref, r_ref, ga_ref, wa_ref, ba_ref, gn_ref, s0_ref,
                y_ref, sout_ref, st_ref, *, chunk, t_valid):
    c = pl.program_id(1)

    @pl.when(c == 0)
    def _():
        st_ref[...] = s0_ref[...]

    x = jnp.dot(ga_ref[...].astype(jnp.bfloat16), wa_ref[...],
                preferred_element_type=jnp.float32) + ba_ref[...]
    g = -(jnp.maximum(-x, 0.0) + jnp.log1p(jnp.exp(-jnp.abs(x)))) / GLA_TAU
    row = c * chunk + lax.broadcasted_iota(jnp.int32, (chunk, 1), 0)
    valid = row < t_valid
    g = jnp.where(valid, g, 0.0)
    ri = lax.broadcasted_iota(jnp.int32, (chunk, chunk), 0)
    ci = lax.broadcasted_iota(jnp.int32, (chunk, chunk), 1)
    causal = ri >= ci
    b = jnp.dot(causal.astype(jnp.float32), g, preferred_element_type=jnp.float32,
                precision=lax.Precision.HIGHEST)
    b_last = b[chunk - 1:chunk, :]
    k = k_ref[...]
    qe = (q_ref[...] * (GLA_DK ** -0.5) * jnp.exp(b)).astype(jnp.bfloat16)
    ke = (k * jnp.exp(-b)).astype(jnp.bfloat16)
    kl = jnp.where(valid, k * jnp.exp(b_last - b), 0.0).astype(jnp.bfloat16)
    a_last = jnp.exp(b_last)
    v = v_ref[...].astype(jnp.bfloat16)
    r = r_ref[...]
    for h in range(GLA_HEADS):
        ks = slice(h * GLA_DK, (h + 1) * GLA_DK)
        vs = slice(h * GLA_DV, (h + 1) * GLA_DV)
        att = jnp.where(causal, _dot_nt(qe[:, ks], ke[:, ks]), 0.0)
        st = st_ref[h]
        o = (jnp.dot(att.astype(jnp.bfloat16), v[:, vs], preferred_element_type=jnp.float32)
             + _dot_nt(qe[:, ks], st.astype(jnp.bfloat16)))
        st_ref[h] = st * a_last[:, ks] + _dot_tn(v[:, vs], kl[:, ks])
        o = o * lax.rsqrt(jnp.mean(o * o, axis=-1, keepdims=True) + RMS_EPS) * gn_ref[...]
        rh = r[:, vs]
        y_ref[:, vs] = o * (rh / (1.0 + jnp.exp(-rh)))

    @pl.when(c == pl.num_programs(1) - 1)
    def _():
        sout_ref[...] = st_ref[...]


def _gla(z, s0t, w_a2, b_a, gnorm, B, T, chunk, t_valid):
    nc = T // chunk
    wa = jnp.zeros((128, GLA_KW), jnp.float32).at[:GLA_GATE_RANK].set(w_a2).astype(jnp.bfloat16)
    row = lambda b, c: b * nc + c
    return pl.pallas_call(
        functools.partial(_gla_kernel, chunk=chunk, t_valid=t_valid),
        grid=(B, nc),
        in_specs=[pl.BlockSpec((chunk, GLA_KW), lambda b, c: (row(b, c), Z_GQ // GLA_KW)),
                  pl.BlockSpec((chunk, GLA_KW), lambda b, c: (row(b, c), Z_GK // GLA_KW)),
                  pl.BlockSpec((chunk, GLA_VW), lambda b, c: (row(b, c), Z_GV // GLA_VW)),
                  pl.BlockSpec((chunk, GLA_VW), lambda b, c: (row(b, c), Z_GR // GLA_VW)),
                  pl.BlockSpec((chunk, 128), lambda b, c: (row(b, c), Z_GA // 128)),
                  pl.BlockSpec((128, GLA_KW), lambda b, c: (0, 0)),
                  pl.BlockSpec((1, GLA_KW), lambda b, c: (0, 0)),
                  pl.BlockSpec((1, GLA_DV), lambda b, c: (0, 0)),
                  pl.BlockSpec((None, GLA_HEADS, GLA_DV, GLA_DK), lambda b, c: (b, 0, 0, 0))],
        out_specs=[pl.BlockSpec((chunk, GLA_VW), lambda b, c: (row(b, c), 0)),
                   pl.BlockSpec((None, GLA_HEADS, GLA_DV, GLA_DK), lambda b, c: (b, 0, 0, 0))],
        out_shape=[jax.ShapeDtypeStruct((B * T, GLA_VW), jnp.float32),
                   jax.ShapeDtypeStruct((B, GLA_HEADS, GLA_DV, GLA_DK), jnp.float32)],
        scratch_shapes=[pltpu.VMEM((GLA_HEADS, GLA_DV, GLA_DK), jnp.float32)],
        compiler_params=_cparams("parallel", "arbitrary"),
        name="gla_scan",
    )(z, z, z, z, z, wa, b_a.reshape(1, GLA_KW), gnorm.reshape(1, GLA_DV), s0t)


def _att_kernel(q_ref, kp_ref, vp_ref, kc_ref, vc_ref, bias_ref, o_ref, l_ref, *, kp_rows, mask_first):
    q = (q_ref[...] * (ATT_HEAD_DIM ** -0.5)).astype(jnp.bfloat16)
    kp = kp_ref[...].astype(jnp.bfloat16)
    vp = vp_ref[...].astype(jnp.bfloat16)
    kc = kc_ref[...].astype(jnp.bfloat16)
    vc = vc_ref[...].astype(jnp.bfloat16)
    qb = q.shape[0]
    first = pl.program_id(2) == 0
    for h in range(ATT_HPG):
        sl = slice(h * ATT_HEAD_DIM, (h + 1) * ATT_HEAD_DIM)
        s1 = _dot_nt(q[:, sl], kp[:, sl]) + bias_ref[h, :, 0:kp_rows]
        if mask_first:
            s1 = jnp.where(first, NEG, s1)
        s2 = _dot_nt(q[:, sl], kc[:, sl]) + bias_ref[h, :, kp_rows:]
        m = jnp.maximum(jnp.max(s1, axis=-1, keepdims=True), jnp.max(s2, axis=-1, keepdims=True))
        p1 = jnp.exp(s1 - m)
        p2 = jnp.exp(s2 - m)
        den = jnp.sum(p1, axis=-1, keepdims=True) + jnp.sum(p2, axis=-1, keepdims=True)
        o = (jnp.dot(p1.astype(jnp.bfloat16), vp[:, sl], preferred_element_type=jnp.float32)
             + jnp.dot(p2.astype(jnp.bfloat16), vc[:, sl], preferred_element_type=jnp.float32))
        o_ref[:, sl] = o / den
        l_ref[:, sl] = jnp.broadcast_to(m + jnp.log(den), (qb, ATT_HEAD_DIM))


def _rel_bucket(dist):
    d = np.asarray(dist, dtype=np.int64)
    max_exact = REL_BUCKETS // 2
    ratio = np.log(np.maximum(d, 1) / max_exact) / np.log(REL_MAX_DIST / max_exact)
    large = np.minimum(max_exact + (ratio * (REL_BUCKETS - max_exact)).astype(np.int64), REL_BUCKETS - 1)
    return np.where(d < max_exact, d, large).astype(np.int32)


def _att_bias(rel_table, gi, qb):
    _, dil = ATT_GROUPS[gi]
    cols = ATT_BAND + qb
    back = np.arange(ATT_BAND, -1, -1)
    tab = rel_table[_rel_bucket(back * dil)][:, gi * ATT_HPG:(gi + 1) * ATT_HPG].T.astype(jnp.float32)
    v = jnp.concatenate([tab, jnp.full((ATT_HPG, cols - ATT_BAND), NEG, jnp.float32)], axis=1)
    return jnp.tile(v, (1, qb))[:, :qb * cols].reshape(ATT_HPG, qb, cols)


def _att_call(args, in_specs, grid, out_rows, out_block_map, qb, mask_first, name):
    blk = pl.BlockSpec((None, qb, ATT_GW), out_block_map)
    shape = jax.ShapeDtypeStruct(out_rows, jnp.float32)
    return pl.pallas_call(
        functools.partial(_att_kernel, kp_rows=ATT_BAND, mask_first=mask_first),
        grid=grid,
        in_specs=in_specs,
        out_specs=[blk, blk],
        out_shape=[shape, shape],
        compiler_params=_cparams("parallel", "parallel", "parallel"),
        name=name,
    )(*args)


def _att_prompt(z, rel_table, gi, B, T):
    _, dil = ATT_GROUPS[gi]
    qb = min(ATT_QBLOCK, T // dil)
    nm = T // dil // qb
    back = qb // ATT_BAND
    if dil == 1:
        zq = zkv = z.reshape(B, T, Z_WIDTH)
        per_q = per_kv = Z_WIDTH // ATT_GW
        cq, ck = Z_Q // ATT_GW + gi, Z_KV // ATT_GW + 2 * gi
    else:
        zq = z[:, Z_Q + gi * ATT_GW:Z_Q + (gi + 1) * ATT_GW].reshape(B, T // dil, dil * ATT_GW)
        zkv = z[:, Z_KV + 2 * gi * ATT_GW:Z_KV + 2 * (gi + 1) * ATT_GW].reshape(B, T // dil, dil * 2 * ATT_GW)
        per_q, per_kv, cq, ck = 1, 2, 0, 0
    cur = lambda per, col: pl.BlockSpec((None, qb, ATT_GW), lambda b, r, m: (b, m, r * per + col))
    prev = lambda per, col: pl.BlockSpec((None, ATT_BAND, ATT_GW),
                                         lambda b, r, m: (b, jnp.maximum(m * back - 1, 0), r * per + col))
    bias = _att_bias(rel_table, gi, qb)
    o, l = _att_call(
        (zq, zkv, zkv, zkv, zkv, bias),
        [cur(per_q, cq), prev(per_kv, ck), prev(per_kv, ck + 1), cur(per_kv, ck), cur(per_kv, ck + 1),
         pl.BlockSpec(bias.shape, lambda b, r, m: (0, 0, 0))],
        (B, dil, nm), (B, T // dil, dil * ATT_GW), lambda b, r, m: (b, m, r), qb, True,
        "att_prompt_g%d" % gi)
    return o.reshape(B * T, ATT_GW), l.reshape(B * T, ATT_GW)


def _att_sample_kernel(q_ref, kv_ref, c0_ref, c1_ref, c2_ref, b0_ref, b1_ref, b2_ref, bc_ref, o_ref):
    bf = jnp.bfloat16
    for h in range(ATT_HPG):
        hs = slice(h * ATT_HEAD_DIM, (h + 1) * ATT_HEAD_DIM)
        outs, lses = [], []
        for gi, (c_ref, bh_ref) in enumerate(((c0_ref, b0_ref), (c1_ref, b1_ref), (c2_ref, b2_ref))):
            head = lambda ref, blk: ref[:, blk * ATT_GW + hs.start:blk * ATT_GW + hs.stop]
            q = (head(q_ref, gi) * (ATT_HEAD_DIM ** -0.5)).astype(bf)
            kc = head(kv_ref, 2 * gi).astype(bf)
            vc = head(kv_ref, 2 * gi + 1).astype(bf)
            s_h = jnp.dot(q, c_ref[0, h].astype(bf), preferred_element_type=jnp.float32) + bh_ref[h]
            s_c = _dot_nt(q, kc) + bc_ref[gi, h]
            m = jnp.maximum(jnp.max(s_h, axis=-1, keepdims=True), jnp.max(s_c, axis=-1, keepdims=True))
            p_h = jnp.exp(s_h - m)
            p_c = jnp.exp(s_c - m)
            den = jnp.sum(p_h, axis=-1, keepdims=True) + jnp.sum(p_c, axis=-1, keepdims=True)
            num = (_dot_nt(p_h.astype(bf), c_ref[1, h].astype(bf))
                   + jnp.dot(p_c.astype(bf), vc, preferred_element_type=jnp.float32))
            outs.append(num / den)
            lses.append(m + jnp.log(den))
        lm = jnp.maximum(jnp.maximum(lses[0], lses[1]), lses[2])
        es = [jnp.exp(l - lm) for l in lses]
        o_ref[:, hs] = (es[0] * outs[0] + es[1] * outs[1] + es[2] * outs[2]) / (es[0] + es[1] + es[2])


def _att_sample(z, caches, layer, rel_table, B, T):
    z3 = z.reshape(B, T, Z_WIDTH)
    views, cspecs, bh, bc = [], [], [], []
    tq = np.arange(T)[:, None]
    for gi, (width, dil) in enumerate(ATT_GROUPS):
        cache = caches[gi]
        assert cache.shape[2] == width
        views.append(jnp.transpose(cache, (0, 1, 3, 4, 5, 2)))
        cspecs.append(pl.BlockSpec((None, None, 2, ATT_HPG, ATT_HEAD_DIM, width),
                                   lambda b: (layer, b, 0, 0, 0, 0)))
        steps = np.arange(ATT_BAND + 1)
        tab = rel_table[_rel_bucket(steps * dil)][:, gi * ATT_HPG:(gi + 1) * ATT_HPG].T.astype(jnp.float32)
        for dist, dst in ((width + tq - np.arange(width)[None, :], bh), (tq - np.arange(T)[None, :], bc)):
            ok = (dist >= 0) & (dist % dil == 0) & (dist // dil <= ATT_BAND)
            j = np.where(ok, dist // dil, 0)
            dst.append(jnp.where(jnp.asarray(ok)[None], tab[:, j], NEG))
    bc = jnp.stack(bc)
    const = lambda a: pl.BlockSpec(a.shape, lambda b: (0,) * a.ndim, pipeline_mode=pl.Buffered(1))
    y = pl.pallas_call(
        _att_sample_kernel,
        grid=(B,),
        in_specs=[pl.BlockSpec((None, T, 3 * ATT_GW), lambda b: (b, 0, Z_Q // (3 * ATT_GW))),
                  pl.BlockSpec((None, T, 6 * ATT_GW), lambda b: (b, 0, Z_KV // (6 * ATT_GW))),
                  *cspecs, *[const(a) for a in bh], const(bc)],
        out_specs=pl.BlockSpec((None, T, ATT_GW), lambda b: (b, 0, 0)),
        out_shape=jax.ShapeDtypeStruct((B, T, ATT_GW), jnp.float32),
        compiler_params=_cparams("parallel"),
        name="att_sample",
    )(z3, z3, *views, *bh, bc)
    return y.reshape(B * T, ATT_GW)


def _merge_kernel(yp_ref, yg_ref, *refs):
    att_refs, (gz_ref, x_ref, wb_ref, wo_ref, g2_ref, x1_ref, hn_ref) = refs[:-7], refs[-7:]
    if len(att_refs) == 1:
        ya = att_refs[0][...]
    else:
        o0_ref, o1_ref, o2_ref, l0_ref, l1_ref, l2_ref = att_refs
        l0, l1, l2 = l0_ref[...], l1_ref[...], l2_ref[...]
        lm = jnp.maximum(jnp.maximum(l0, l1), l2)
        e0, e1, e2 = jnp.exp(l0 - lm), jnp.exp(l1 - lm), jnp.exp(l2 - lm)
        ya = (e0 * o0_ref[...] + e1 * o1_ref[...] + e2 * o2_ref[...]) / (e0 + e1 + e2)
    acc = None
    for bi, y in enumerate((yp_ref[...], yg_ref[...], ya)):
        proj = jnp.dot(y.astype(jnp.bfloat16), wb_ref[bi], preferred_element_type=jnp.float32)
        gz = gz_ref[:, bi * D_MODEL:(bi + 1) * D_MODEL]
        term = proj / (1.0 + jnp.exp(-gz))
        acc = term if acc is None else acc + term
    x1 = x_ref[...] + jnp.dot(acc.astype(jnp.bfloat16), wo_ref[...], preferred_element_type=jnp.float32)
    x1_ref[...] = x1
    ms = jnp.mean(x1 * x1, axis=-1, keepdims=True)
    hn_ref[...] = (x1 * lax.rsqrt(ms + RMS_EPS) * g2_ref[...]).astype(hn_ref.dtype)


def _merge(yp, yg, att, z, x, w_branch, w_out, norm2_g, tm):
    n = x.shape[0]
    half = pl.BlockSpec((tm, 512), lambda i: (i, 0))
    full = pl.BlockSpec((tm, D_MODEL), lambda i: (i, 0))
    if isinstance(att, list):
        (o0, l0), (o1, l1), (o2, l2) = att
        att = (o0, o1, o2, l0, l1, l2)
    else:
        att = (att,)
    return pl.pallas_call(
        _merge_kernel,
        grid=(n // tm,),
        in_specs=[half] * (2 + len(att)) + [
            pl.BlockSpec((tm, 3 * D_MODEL), lambda i: (i, Z_GZ // (3 * D_MODEL))),
            full,
            pl.BlockSpec((3, 512, D_MODEL), lambda i: (0, 0, 0)),
            pl.BlockSpec((D_MODEL, D_MODEL), lambda i: (0, 0)),
            pl.BlockSpec((1, D_MODEL), lambda i: (0, 0))],
        out_specs=[full, full],
        out_shape=[jax.ShapeDtypeStruct((n, D_MODEL), jnp.float32),
                   jax.ShapeDtypeStruct((n, D_MODEL), jnp.bfloat16)],
        compiler_params=_cparams("parallel"),
        name="branch_merge",
    )(yp, yg, *att, z, x, w_branch.astype(jnp.bfloat16),
      w_out.astype(jnp.bfloat16), norm2_g.reshape(1, D_MODEL))


def _peer_select_kernel(q_ref, k1_ref, k2_ref, c1_ref, n2_ref, a2_ref, r2_ref):
    q = q_ref[...].astype(jnp.bfloat16)
    s1 = _dot_nt(k1_ref[...], q[:, :PEER_HALF])
    s2 = _dot_nt(k2_ref[...], q[:, PEER_HALF:])
    none = float(PEER_TOPK)

    def top(vals, count, ranked):
        out, work = [], vals
        rank = jnp.full(vals.shape, none, jnp.float32) if ranked else None
        for a in range(count):
            m = jnp.max(work, axis=0, keepdims=True)
            out.append(m)
            hit = work >= m
            if ranked:
                rank = jnp.where(hit, float(a), rank)
            work = jnp.where(hit, -jnp.inf, work)
        return out, rank

    v1, rank1 = top(s1, PEER_TOPK, True)
    v2, rank2 = top(s2, PEER_TOPK, True)
    rid = lax.broadcasted_iota(jnp.int32, (PEER_TOPK, s2.shape[1]), 0)
    v2all = jnp.zeros((PEER_TOPK, s2.shape[1]), jnp.float32)
    for b, vb in enumerate(v2):
        v2all = jnp.where(rid == b, vb, v2all)
    cand = jnp.concatenate([v1[0] + v2all] + [v1[a] + v2all[:8] for a in range(1, PEER_TOPK)], axis=0)
    best, _ = top(cand, PEER_TOPK, False)
    tau = best[PEER_TOPK - 1]
    den = jnp.zeros_like(tau)
    n2 = jnp.zeros_like(s1)
    for a in range(PEER_TOPK):
        pair = v1[a] + v2all
        keep = pair >= tau
        den = den + jnp.sum(jnp.where(keep, jnp.exp(pair - best[0]), 0.0), axis=0, keepdims=True)
        n2 = jnp.where(rank1 == float(a), jnp.sum(keep.astype(jnp.float32), axis=0, keepdims=True), n2)
    c1_ref[...] = jnp.where(rank1 < none, jnp.exp(s1 - v1[0]) / den, 0.0)
    n2_ref[...] = n2
    a2_ref[...] = jnp.where(rank2 < none, jnp.exp(s2 - v2[0]), 0.0).astype(a2_ref.dtype)
    r2_ref[...] = rank2.astype(r2_ref.dtype)


def _peer_select(q, k1, k2, tn):
    n = q.shape[0]
    keyspec = pl.BlockSpec((None, PEER_NKEYS, PEER_HALF), lambda i, h: (h, 0, 0))
    out = pl.BlockSpec((None, PEER_NKEYS, tn), lambda i, h: (h, 0, i))
    shape = lambda dt: jax.ShapeDtypeStruct((PEER_HEADS, PEER_NKEYS, n), dt)
    return pl.pallas_call(
        _peer_select_kernel,
        grid=(n // tn, PEER_HEADS),
        in_specs=[pl.BlockSpec((tn, 2 * PEER_HALF), lambda i, h: (i, h)), keyspec, keyspec],
        out_specs=[out] * 4,
        out_shape=[shape(jnp.float32), shape(jnp.float32), shape(jnp.bfloat16), shape(jnp.bfloat16)],
        compiler_params=_cparams("parallel", "parallel"),
        name="peer_select",
    )(q, k1.astype(jnp.bfloat16), k2.astype(jnp.bfloat16))


def _peer_expert_kernel(hn_ref, u_ref, v_ref, c1_ref, n2_ref, a2_ref, r2_ref, o_ref,
                        st_ref, p_ref, a2s_ref, r2s_ref):
    c = pl.program_id(1)

    @pl.when(c == 0)
    def _():
        o_ref[...] = jnp.zeros_like(o_ref)
        a2s_ref[...] = a2_ref[...]
        r2s_ref[...] = r2_ref[...]

    st_ref[...] = _dot_nt(u_ref[...], hn_ref[...])
    tn = st_ref.shape[1]
    sub = 16
    for ii in range(PEER_ROWS):
        for lb in range(tn // 128):
            lanes = slice(lb * 128, (lb + 1) * 128)
            row = lambda ref, h: jnp.broadcast_to(ref[h, ii:ii + 1, lanes], (sub, 128)).astype(jnp.bfloat16)
            n2 = [row(n2_ref, h) for h in range(PEER_HEADS)]
            c1 = [row(c1_ref, h) for h in range(PEER_HEADS)]
            for jb in range(PEER_NKEYS // sub):
                js = slice(jb * sub, (jb + 1) * sub)
                es = slice(ii * PEER_NKEYS + jb * sub, ii * PEER_NKEYS + (jb + 1) * sub)
                s = st_ref[es, lanes].astype(jnp.bfloat16)
                act = (0.5 * s) * (1.0 + lax.erf(s * (2.0 ** -0.5)))
                w = None
                for h in range(PEER_HEADS):
                    sel = jnp.where(r2s_ref[h, js, lanes] < n2[h], a2s_ref[h, js, lanes], jnp.bfloat16(0)) * c1[h]
                    w = sel if w is None else w + sel
                p_ref[es, lanes] = w * act
    o_ref[...] += _dot_tn(p_ref[...], v_ref[...])


def _peer_experts(hn, u_bf, v_bf, sel, tn):
    n = hn.shape[0]
    ce = PEER_ROWS * PEER_NKEYS
    c1, n2, a2, r2 = sel
    rowblk = pl.BlockSpec((PEER_HEADS, PEER_ROWS, tn), lambda i, c: (0, c, i))
    allblk = pl.BlockSpec((PEER_HEADS, PEER_NKEYS, tn), lambda i, c: (0, 0, i),
                          pipeline_mode=pl.Buffered(1))
    tok = pl.BlockSpec((tn, D_MODEL), lambda i, c: (i, 0))
    table = pl.BlockSpec((ce, D_MODEL), lambda i, c: (c, 0))
    return pl.pallas_call(
        _peer_expert_kernel,
        grid=(n // tn, PEER_EXPERTS // ce),
        in_specs=[tok, table, table, rowblk, rowblk, allblk, allblk],
        out_specs=tok,
        out_shape=jax.ShapeDtypeStruct((n, D_MODEL), jnp.float32),
        scratch_shapes=[pltpu.VMEM((ce, tn), jnp.float32),
                        pltpu.VMEM((ce, tn), jnp.bfloat16),
                        pltpu.VMEM((PEER_HEADS, PEER_NKEYS, tn), jnp.bfloat16),
                        pltpu.VMEM((PEER_HEADS, PEER_NKEYS, tn), jnp.bfloat16)],
        compiler_params=_cparams("parallel", "arbitrary"),
        name="peer_experts",
    )(hn, u_bf, v_bf, c1, n2, a2, r2)


def _arrange_w_in(w_in):
    sizes = (512, 256, 256, 512, 512, 16, 1536, 1536, 1536, 3072)
    offs = np.concatenate([[0], np.cumsum(sizes)])
    up, gq, gk, gv, gr, ga, aq, ak, av, gz = [w_in[:, offs[i]:offs[i + 1]] for i in range(10)]
    kv = []
    for g in range(3):
        kv += [ak[:, g * ATT_GW:(g + 1) * ATT_GW], av[:, g * ATT_GW:(g + 1) * ATT_GW]]
    pad = jnp.zeros((w_in.shape[0], Z_WIDTH - Z_GA - GLA_GATE_RANK), w_in.dtype)
    return jnp.concatenate([gz] + kv + [aq, up, gv, gr, gq, gk, ga, pad], axis=1).astype(jnp.bfloat16)


def _layer(x, hn, B, T, t_valid, pos0, pool_hist, gla_s0, caches, layer, rel_table, lw, tiles):
    (w_in_r, w_a2, b_a, w_pool, s_pool, gla_norm_g, w_branch, w_out, norm2_g,
     wq_bf, k1, k2, u_bf, v_bf) = lw
    tm, tt, chunk, tn_sel, tn = tiles
    z = _matmul(hn, w_in_r, tm, 2048, "in_proj")
    hist16 = jnp.pad(pool_hist, ((0, 0), (POOL_HALO - POOL_HIST, 0), (0, 0)))
    y_pool = _pool_mix(z, hist16, w_pool, s_pool, B, T, tt, pos0)
    y_gla, st = _gla(z, jnp.swapaxes(gla_s0, 2, 3), w_a2, b_a, gla_norm_g, B, T, chunk, t_valid)
    if caches is None:
        att = [_att_prompt(z, rel_table, gi, B, T) for gi in range(3)]
    else:
        att = _att_sample(z, caches, layer, rel_table, B, T)
    x1, hn2 = _merge(y_pool, y_gla, att, z, x, w_branch, w_out, norm2_g, tm)
    q = _matmul(hn2, wq_bf, tm, 1024, "peer_query")
    sel = _peer_select(q, k1, k2, tn_sel)
    delta = _peer_experts(hn2, u_bf, v_bf, sel, tn)
    z3 = z.reshape(B, T, Z_WIDTH)
    u_new = z3[:, :t_valid, Z_POOL:Z_POOL + POOL_WIDTH]
    pool_tail = jnp.concatenate([pool_hist, u_new], axis=1)[:, -POOL_HIST:]
    kv_new = [z3[:, :t_valid, Z_KV + 2 * ATT_GW * g:Z_KV + 2 * ATT_GW * (g + 1)]
              .reshape(B, t_valid, 2, ATT_HPG, ATT_HEAD_DIM) for g in range(3)]
    return x1, delta, pool_tail, jnp.swapaxes(st, 2, 3), kv_new


def kernel(x_prompt, x_sample, state_pool, state_gla, cache_att1, cache_att2, cache_att3, rel_table,
           norm1_g, w_in, w_a2, b_a, w_pool, s_pool, gla_norm_g, w_branch, w_out, norm2_g,
           peer_wq, peer_k1, peer_k2, peer_u, peer_v, final_norm_g):
    bp, tp, _ = x_prompt.shape
    bs, ts, _ = x_sample.shape
    depth = w_in.shape[0]
    ts_pad = 8
    caches = (cache_att1, cache_att2, cache_att3)
    xp = x_prompt.reshape(bp * tp, D_MODEL)
    xs = jnp.pad(x_sample, ((0, 0), (0, ts_pad - ts), (0, 0))).reshape(bs * ts_pad, D_MODEL)
    tiles_p = (512, 512, GLA_CHUNK, 512, 1024)
    rows_s = min(256, bs * ts_pad)
    tiles_s = (rows_s, ts_pad, ts_pad, rows_s, rows_s)
    pool_p, pool_s, gla_p, gla_s = [], [], [], []
    att_p = [[] for _ in range(3)]
    att_s = [[] for _ in range(3)]
    hp = _rmsnorm(xp, norm1_g[0], jnp.bfloat16, tiles_p[0])
    hs = _rmsnorm(xs, norm1_g[0], jnp.bfloat16, tiles_s[0])
    for l in range(depth):
        lw = (_arrange_w_in(w_in[l]), w_a2[l], b_a[l], w_pool[l], s_pool[l], gla_norm_g[l], w_branch[l],
              w_out[l], norm2_g[l], peer_wq[l].astype(jnp.bfloat16), peer_k1[l], peer_k2[l],
              peer_u[l].astype(jnp.bfloat16), peer_v[l].astype(jnp.bfloat16))
        xp, dp, pt, gp, kvp = _layer(
            xp, hp, bp, tp, tp, 0, jnp.zeros((bp, POOL_HIST, POOL_WIDTH), jnp.float32),
            jnp.zeros((bp, GLA_HEADS, GLA_DK, GLA_DV), jnp.float32), None, l, rel_table, lw, tiles_p)
        xs, ds, ps, gs, kvs = _layer(
            xs, hs, bs, ts_pad, ts, PAST_LEN, state_pool[l], state_gla[l], caches, l, rel_table, lw, tiles_s)
        last = l == depth - 1
        gain = final_norm_g if last else norm1_g[l + 1]
        xp, hp = _add_rmsnorm(xp, dp, gain, jnp.float32 if last else jnp.bfloat16, tiles_p[0])
        xs, hs = _add_rmsnorm(xs, ds, gain, jnp.float32 if last else jnp.bfloat16, tiles_s[0])
        pool_p.append(pt)
        pool_s.append(ps)
        gla_p.append(gp)
        gla_s.append(gs)
        for g, (w, _) in enumerate(ATT_GROUPS):
            att_p[g].append(kvp[g][:, tp - min(w, tp):])
            att_s[g].append(kvs[g])
    y_prompt = hp.reshape(bp, tp, D_MODEL)
    y_sample = hs.reshape(bs, ts_pad, D_MODEL)[:, :ts]
    outs = [y_prompt, y_sample, jnp.stack(pool_p), jnp.stack(pool_s), jnp.stack(gla_p), jnp.stack(gla_s)]
    for g in range(3):
        outs += [jnp.stack(att_p[g]), jnp.stack(att_s[g])]
    return tuple(outs)
```

```python
import functools
import math

import numpy as np
import jax
import jax.numpy as jnp
from jax import lax
from jax.experimental import pallas as pl
from jax.experimental.pallas import tpu as pltpu

D_MODEL = 1024
RMS_EPS = 1e-6
PAST_LEN = 8192

POOL_WINDOWS = (2, 4, 8, 16)
POOL_GROUP = 128
POOL_WIDTH = 512
POOL_HIST = 15
POOL_HALO = 16

GLA_HEADS = 4
GLA_DK = 64
GLA_DV = 128
GLA_KW = 256
GLA_VW = 512
GLA_GATE_RANK = 16
GLA_TAU = 16.0
GLA_CHUNK = 64

ATT_GROUPS = ((128, 1), (512, 4), (2048, 16))
ATT_HPG = 8
ATT_HEAD_DIM = 64
ATT_GW = ATT_HPG * ATT_HEAD_DIM
ATT_BAND = 128
ATT_QBLOCK = 256
REL_BUCKETS = 32
REL_MAX_DIST = 2048
NEG = -1e30

PEER_HEADS = 8
PEER_NKEYS = 128
PEER_EXPERTS = PEER_NKEYS * PEER_NKEYS
PEER_HALF = 128
PEER_TOPK = 16
PEER_ROWS = 8

Z_GZ = 0
Z_KV = 3072
Z_Q = 6144
Z_POOL = 7680
Z_GV = 8192
Z_GR = 8704
Z_GQ = 9216
Z_GK = 9472
Z_GA = 9728
Z_WIDTH = 10240

VMEM_LIMIT = 48 * 1024 * 1024


def _cparams(*sem):
    return pltpu.CompilerParams(dimension_semantics=sem, vmem_limit_bytes=VMEM_LIMIT)


def _dot_nt(a, b):
    return lax.dot_general(a, b, (((1,), (1,)), ((), ())), preferred_element_type=jnp.float32)


def _dot_tn(a, b):
    return lax.dot_general(a, b, (((0,), (0,)), ((), ())), preferred_element_type=jnp.float32)


def _rmsnorm_kernel(x_ref, g_ref, o_ref):
    x = x_ref[...]
    ms = jnp.mean(x * x, axis=-1, keepdims=True)
    o_ref[...] = (x * lax.rsqrt(ms + RMS_EPS) * g_ref[...]).astype(o_ref.dtype)


def _rmsnorm(x, g, out_dtype, tm):
    n, d = x.shape
    return pl.pallas_call(
        _rmsnorm_kernel,
        grid=(n // tm,),
        in_specs=[pl.BlockSpec((tm, d), lambda i: (i, 0)),
                  pl.BlockSpec((1, d), lambda i: (0, 0))],
        out_specs=pl.BlockSpec((tm, d), lambda i: (i, 0)),
        out_shape=jax.ShapeDtypeStruct((n, d), out_dtype),
        compiler_params=_cparams("parallel"),
        name="rmsnorm",
    )(x, g.reshape(1, d))


def _add_rmsnorm_kernel(x_ref, d_ref, g_ref, s_ref, o_ref):
    x = x_ref[...] + d_ref[...]
    s_ref[...] = x
    ms = jnp.mean(x * x, axis=-1, keepdims=True)
    o_ref[...] = (x * lax.rsqrt(ms + RMS_EPS) * g_ref[...]).astype(o_ref.dtype)


def _add_rmsnorm(x, delta, g, out_dtype, tm):
    n, d = x.shape
    blk = pl.BlockSpec((tm, d), lambda i: (i, 0))
    return pl.pallas_call(
        _add_rmsnorm_kernel,
        grid=(n // tm,),
        in_specs=[blk, blk, pl.BlockSpec((1, d), lambda i: (0, 0))],
        out_specs=[blk, blk],
        out_shape=[jax.ShapeDtypeStruct((n, d), jnp.float32), jax.ShapeDtypeStruct((n, d), out_dtype)],
        compiler_params=_cparams("parallel"),
        name="add_rmsnorm",
    )(x, delta, g.reshape(1, d))


def _mm_kernel(a_ref, b_ref, o_ref):
    o_ref[...] = jnp.dot(a_ref[...], b_ref[...], preferred_element_type=jnp.float32)


def _mm_nt_kernel(a_ref, b_ref, o_ref):
    o_ref[...] = _dot_nt(a_ref[...], b_ref[...])


def _matmul(a, b, tm, tn, name, b_is_transposed=False):
    m, k = a.shape
    n = b.shape[0] if b_is_transposed else b.shape[1]
    b_spec = (pl.BlockSpec((tn, k), lambda j, i: (j, 0)) if b_is_transposed
              else pl.BlockSpec((k, tn), lambda j, i: (0, j)))
    return pl.pallas_call(
        _mm_nt_kernel if b_is_transposed else _mm_kernel,
        grid=(n // tn, m // tm),
        in_specs=[pl.BlockSpec((tm, k), lambda j, i: (i, 0)), b_spec],
        out_specs=pl.BlockSpec((tm, tn), lambda j, i: (i, j)),
        out_shape=jax.ShapeDtypeStruct((m, n), jnp.float32),
        compiler_params=_cparams("parallel", "parallel"),
        name=name,
    )(a, b)


def _pool_kernel(u_ref, prev_ref, hist_ref, w_ref, s_ref, o_ref, ext_ref, *, tt, pos0):
    t = pl.program_id(1)
    ext_ref[0:POOL_HALO, :] = jnp.where(t == 0, hist_ref[...], prev_ref[...])
    u = u_ref[...]
    ext_ref[POOL_HALO:, :] = u
    pos = (pos0 + t * tt + lax.broadcasted_iota(jnp.int32, (tt, 1), 0)).astype(jnp.float32)
    for gi, w in enumerate(POOL_WINDOWS):
        lo, hi = gi * POOL_GROUP, (gi + 1) * POOL_GROUP
        s = ext_ref[POOL_HALO:POOL_HALO + tt, lo:hi]
        for back in range(1, w):
            s = s + ext_ref[POOL_HALO - back:POOL_HALO - back + tt, lo:hi]
        cnt = jnp.minimum(float(w), pos + 1.0)
        d = s / cnt - u[:, lo:hi]
        y = jnp.dot(d.astype(jnp.bfloat16), w_ref[gi], preferred_element_type=jnp.float32)
        o_ref[:, lo:hi] = y * s_ref[:, lo:hi]


def _pool_mix(z, hist16, w_pool, s_pool, B, T, tt, pos0):
    nt = T // tt
    cb = Z_POOL // POOL_WIDTH

    def prev_map(b, t):
        return (jnp.maximum(b * T + t * tt - POOL_HALO, 0) // POOL_HALO, cb)

    return pl.pallas_call(
        functools.partial(_pool_kernel, tt=tt, pos0=pos0),
        grid=(B, nt),
        in_specs=[pl.BlockSpec((tt, POOL_WIDTH), lambda b, t: (b * nt + t, cb)),
                  pl.BlockSpec((POOL_HALO, POOL_WIDTH), prev_map),
                  pl.BlockSpec((None, POOL_HALO, POOL_WIDTH), lambda b, t: (b, 0, 0)),
                  pl.BlockSpec((4, POOL_GROUP, POOL_GROUP), lambda b, t: (0, 0, 0)),
                  pl.BlockSpec((1, POOL_WIDTH), lambda b, t: (0, 0))],
        out_specs=pl.BlockSpec((tt, POOL_WIDTH), lambda b, t: (b * nt + t, 0)),
        out_shape=jax.ShapeDtypeStruct((B * T, POOL_WIDTH), jnp.float32),
        scratch_shapes=[pltpu.VMEM((POOL_HALO + tt, POOL_WIDTH), jnp.float32)],
        compiler_params=_cparams("parallel", "parallel"),
        name="pool_mix",
    )(z, z, hist16, w_pool.astype(jnp.bfloat16), s_pool.reshape(1, POOL_WIDTH))


def _gla_kernel(q_ref, k_ref, v_ref, r_ref, ga_ref, wa_ref, ba_ref, gn_ref, s0_ref,
                y_ref, sout_ref, st_ref, *, chunk, t_valid):
    c = pl.program_id(1)

    @pl.when(c == 0)
    def _():
        st_ref[...] = s0_ref[...]

    x = jnp.dot(ga_ref[...].astype(jnp.bfloat16), wa_ref[...],
                preferred_element_type=jnp.float32) + ba_ref[...]
    g = -(jnp.maximum(-x, 0.0) + jnp.log1p(jnp.exp(-jnp.abs(x)))) / GLA_TAU
    row = c * chunk + lax.broadcasted_iota(jnp.int32, (chunk, 1), 0)
    valid = row < t_valid
    g = jnp.where(valid, g, 0.0)
    ri = lax.broadcasted_iota(jnp.int32, (chunk, chunk), 0)
    ci = lax.broadcasted_iota(jnp.int32, (chunk, chunk), 1)
    causal = ri >= ci
    b = jnp.dot(causal.astype(jnp.float32), g, preferred_element_type=jnp.float32,
                precision=lax.Precision.HIGHEST)
    b_last = b[chunk - 1:chunk, :]
    k = k_ref[...]
    qe = (q_ref[...] * (GLA_DK ** -0.5) * jnp.exp(b)).astype(jnp.bfloat16)
    ke = (k * jnp.exp(-b)).astype(jnp.bfloat16)
    kl = jnp.where(valid, k * jnp.exp(b_last - b), 0.0).astype(jnp.bfloat16)
    a_last = jnp.exp(b_last)
    v = v_ref[...].astype(jnp.bfloat16)
    r = r_ref[...]
    for h in range(GLA_HEADS):
        ks = slice(h * GLA_DK, (h + 1) * GLA_DK)
        vs = slice(h * GLA_DV, (h + 1) * GLA_DV)
        att = jnp.where(causal, _dot_nt(qe[:, ks], ke[:, ks]), 0.0)
        st = st_ref[h]
        o = (jnp.dot(att.astype(jnp.bfloat16), v[:, vs], preferred_element_type=jnp.float32)
             + _dot_nt(qe[:, ks], st.astype(jnp.bfloat16)))
        st_ref[h] = st * a_last[:, ks] + _dot_tn(v[:, vs], kl[:, ks])
        o = o * lax.rsqrt(jnp.mean(o * o, axis=-1, keepdims=True) + RMS_EPS) * gn_ref[...]
        rh = r[:, vs]
        y_ref[:, vs] = o * (rh / (1.0 + jnp.exp(-rh)))

    @pl.when(c == pl.num_programs(1) - 1)
    def _():
        sout_ref[...] = st_ref[...]


def _gla(z, s0t, w_a2, b_a, gnorm, B, T, chunk, t_valid):
    nc = T // chunk
    wa = jnp.zeros((128, GLA_KW), jnp.float32).at[:GLA_GATE_RANK].set(w_a2).astype(jnp.bfloat16)
    row = lambda b, c: b * nc + c
    return pl.pallas_call(
        functools.partial(_gla_kernel, chunk=chunk, t_valid=t_valid),
        grid=(B, nc),
        in_specs=[pl.BlockSpec((chunk, GLA_KW), lambda b, c: (row(b, c), Z_GQ // GLA_KW)),
                  pl.BlockSpec((chunk, GLA_KW), lambda b, c: (row(b, c), Z_GK // GLA_KW)),
                  pl.BlockSpec((chunk, GLA_VW), lambda b, c: (row(b, c), Z_GV // GLA_VW)),
                  pl.BlockSpec((chunk, GLA_VW), lambda b, c: (row(b, c), Z_GR // GLA_VW)),
                  pl.BlockSpec((chunk, 128), lambda b, c: (row(b, c), Z_GA // 128)),
                  pl.BlockSpec((128, GLA_KW), lambda b, c: (0, 0)),
                  pl.BlockSpec((1, GLA_KW), lambda b, c: (0, 0)),
                  pl.BlockSpec((1, GLA_DV), lambda b, c: (0, 0)),
                  pl.BlockSpec((None, GLA_HEADS, GLA_DV, GLA_DK), lambda b, c: (b, 0, 0, 0))],
        out_specs=[pl.BlockSpec((chunk, GLA_VW), lambda b, c: (row(b, c), 0)),
                   pl.BlockSpec((None, GLA_HEADS, GLA_DV, GLA_DK), lambda b, c: (b, 0, 0, 0))],
        out_shape=[jax.ShapeDtypeStruct((B * T, GLA_VW), jnp.float32),
                   jax.ShapeDtypeStruct((B, GLA_HEADS, GLA_DV, GLA_DK), jnp.float32)],
        scratch_shapes=[pltpu.VMEM((GLA_HEADS, GLA_DV, GLA_DK), jnp.float32)],
        compiler_params=_cparams("parallel", "arbitrary"),
        name="gla_scan",
    )(z, z, z, z, z, wa, b_a.reshape(1, GLA_KW), gnorm.reshape(1, GLA_DV), s0t)


def _att_kernel(q_ref, kp_ref, vp_ref, kc_ref, vc_ref, bias_ref, o_ref, l_ref, *, kp_rows, mask_first):
    q = (q_ref[...] * (ATT_HEAD_DIM ** -0.5)).astype(jnp.bfloat16)
    kp = kp_ref[...].astype(jnp.bfloat16)
    vp = vp_ref[...].astype(jnp.bfloat16)
    kc = kc_ref[...].astype(jnp.bfloat16)
    vc = vc_ref[...].astype(jnp.bfloat16)
    qb = q.shape[0]
    first = pl.program_id(2) == 0
    for h in range(ATT_HPG):
        sl = slice(h * ATT_HEAD_DIM, (h + 1) * ATT_HEAD_DIM)
        s1 = _dot_nt(q[:, sl], kp[:, sl]) + bias_ref[h, :, 0:kp_rows]
        if mask_first:
            s1 = jnp.where(first, NEG, s1)
        s2 = _dot_nt(q[:, sl], kc[:, sl]) + bias_ref[h, :, kp_rows:]
        m = jnp.maximum(jnp.max(s1, axis=-1, keepdims=True), jnp.max(s2, axis=-1, keepdims=True))
        p1 = jnp.exp(s1 - m)
        p2 = jnp.exp(s2 - m)
        den = jnp.sum(p1, axis=-1, keepdims=True) + jnp.sum(p2, axis=-1, keepdims=True)
        o = (jnp.dot(p1.astype(jnp.bfloat16), vp[:, sl], preferred_element_type=jnp.float32)
             + jnp.dot(p2.astype(jnp.bfloat16), vc[:, sl], preferred_element_type=jnp.float32))
        o_ref[:, sl] = o / den
        l_ref[:, sl] = jnp.broadcast_to(m + jnp.log(den), (qb, ATT_HEAD_DIM))


def _rel_bucket(dist):
    d = np.asarray(dist, dtype=np.int64)
    max_exact = REL_BUCKETS // 2
    ratio = np.log(np.maximum(d, 1) / max_exact) / np.log(REL_MAX_DIST / max_exact)
    large = np.minimum(max_exact + (ratio * (REL_BUCKETS - max_exact)).astype(np.int64), REL_BUCKETS - 1)
    return np.where(d < max_exact, d, large).astype(np.int32)


def _att_bias(rel_table, gi, qb):
    _, dil = ATT_GROUPS[gi]
    cols = ATT_BAND + qb
    back = np.arange(ATT_BAND, -1, -1)
    tab = rel_table[_rel_bucket(back * dil)][:, gi * ATT_HPG:(gi + 1) * ATT_HPG].T.astype(jnp.float32)
    v = jnp.concatenate([tab, jnp.full((ATT_HPG, cols - ATT_BAND), NEG, jnp.float32)], axis=1)
    return jnp.tile(v, (1, qb))[:, :qb * cols].reshape(ATT_HPG, qb, cols)


def _att_call(args, in_specs, grid, out_rows, out_block_map, qb, mask_first, name):
    blk = pl.BlockSpec((None, qb, ATT_GW), out_block_map)
    shape = jax.ShapeDtypeStruct(out_rows, jnp.float32)
    return pl.pallas_call(
        functools.partial(_att_kernel, kp_rows=ATT_BAND, mask_first=mask_first),
        grid=grid,
        in_specs=in_specs,
        out_specs=[blk, blk],
        out_shape=[shape, shape],
        compiler_params=_cparams("parallel", "parallel", "parallel"),
        name=name,
    )(*args)


def _att_prompt(z, rel_table, gi, B, T):
    _, dil = ATT_GROUPS[gi]
    qb = min(ATT_QBLOCK, T // dil)
    nm = T // dil // qb
    back = qb // ATT_BAND
    if dil == 1:
        zq = zkv = z.reshape(B, T, Z_WIDTH)
        per_q = per_kv = Z_WIDTH // ATT_GW
        cq, ck = Z_Q // ATT_GW + gi, Z_KV // ATT_GW + 2 * gi
    else:
        zq = z[:, Z_Q + gi * ATT_GW:Z_Q + (gi + 1) * ATT_GW].reshape(B, T // dil, dil * ATT_GW)
        zkv = z[:, Z_KV + 2 * gi * ATT_GW:Z_KV + 2 * (gi + 1) * ATT_GW].reshape(B, T // dil, dil * 2 * ATT_GW)
        per_q, per_kv, cq, ck = 1, 2, 0, 0
    cur = lambda per, col: pl.BlockSpec((None, qb, ATT_GW), lambda b, r, m: (b, m, r * per + col))
    prev = lambda per, col: pl.BlockSpec((None, ATT_BAND, ATT_GW),
                                         lambda b, r, m: (b, jnp.maximum(m * back - 1, 0), r * per + col))
    bias = _att_bias(rel_table, gi, qb)
    o, l = _att_call(
        (zq, zkv, zkv, zkv, zkv, bias),
        [cur(per_q, cq), prev(per_kv, ck), prev(per_kv, ck + 1), cur(per_kv, ck), cur(per_kv, ck + 1),
         pl.BlockSpec(bias.shape, lambda b, r, m: (0, 0, 0))],
        (B, dil, nm), (B, T // dil, dil * ATT_GW), lambda b, r, m: (b, m, r), qb, True,
        "att_prompt_g%d" % gi)
    return o.reshape(B * T, ATT_GW), l.reshape(B * T, ATT_GW)


def _att_sample_kernel(q_ref, kv_ref, c0_ref, c1_ref, c2_ref, b0_ref, b1_ref, b2_ref, bc_ref, o_ref):
    bf = jnp.bfloat16
    T = q_ref.shape[0]
    rows = ATT_HPG * T
    own = (lax.broadcasted_iota(jnp.int32, (rows, ATT_GW), 0) // T
           == lax.broadcasted_iota(jnp.int32, (rows, ATT_GW), 1) // ATT_HEAD_DIM)
    nums, dens, lses = [], [], []
    for gi, (c_ref, bh_ref) in enumerate(((c0_ref, b0_ref), (c1_ref, b1_ref), (c2_ref, b2_ref))):
        width = c_ref.shape[-1]
        q = q_ref[:, gi * ATT_GW:(gi + 1) * ATT_GW] * (ATT_HEAD_DIM ** -0.5)
        qblk = jnp.where(own, jnp.concatenate([q] * ATT_HPG, axis=0), 0.0).astype(bf)
        kc = kv_ref[:, 2 * gi * ATT_GW:(2 * gi + 1) * ATT_GW].astype(bf)
        vc = kv_ref[:, (2 * gi + 1) * ATT_GW:(2 * gi + 2) * ATT_GW].astype(bf)
        kt = c_ref[0].reshape(ATT_GW, width).astype(bf)
        vt = c_ref[1].reshape(ATT_GW, width).astype(bf)
        s_h = jnp.dot(qblk, kt, preferred_element_type=jnp.float32) + bh_ref[...]
        s_c = _dot_nt(qblk, kc) + bc_ref[gi]
        m = jnp.maximum(jnp.max(s_h, axis=-1, keepdims=True), jnp.max(s_c, axis=-1, keepdims=True))
        p_h = jnp.exp(s_h - m)
        p_c = jnp.exp(s_c - m)
        den = jnp.sum(p_h, axis=-1, keepdims=True) + jnp.sum(p_c, axis=-1, keepdims=True)
        nums.append(_dot_nt(p_h.astype(bf), vt) + jnp.dot(p_c.astype(bf), vc, preferred_element_type=jnp.float32))
        dens.append(den)
        lses.append(m + jnp.log(den))
    lm = jnp.maximum(jnp.maximum(lses[0], lses[1]), lses[2])
    es = [jnp.exp(l - lm) for l in lses]
    tot = es[0] + es[1] + es[2]
    y = sum(nums[g] * (es[g] / (dens[g] * tot)) for g in range(3))
    y = jnp.where(own, y, 0.0)
    o_ref[...] = jnp.sum(y.reshape(ATT_HPG, T, ATT_GW), axis=0)


def _att_sample(z, caches, layer, rel_table, B, T):
    z3 = z.reshape(B, T, Z_WIDTH)
    views, cspecs, bh, bc = [], [], [], []
    tq = np.arange(T)[:, None]
    for gi, (width, dil) in enumerate(ATT_GROUPS):
        cache = caches[gi]
        assert cache.shape[2] == width
        views.append(jnp.transpose(cache, (0, 1, 3, 4, 5, 2)))
        cspecs.append(pl.BlockSpec((None, None, 2, ATT_HPG, ATT_HEAD_DIM, width),
                                   lambda b: (layer, b, 0, 0, 0, 0)))
        steps = np.arange(ATT_BAND + 1)
        tab = rel_table[_rel_bucket(steps * dil)][:, gi * ATT_HPG:(gi + 1) * ATT_HPG].T.astype(jnp.float32)
        for dist, dst in ((width + tq - np.arange(width)[None, :], bh), (tq - np.arange(T)[None, :], bc)):
            ok = (dist >= 0) & (dist % dil == 0) & (dist // dil <= ATT_BAND)
            j = np.where(ok, dist // dil, 0)
            bias = jnp.where(jnp.asarray(ok)[None], tab[:, j], NEG)
            dst.append(bias.reshape(ATT_HPG * T, bias.shape[-1]))
    bc = jnp.stack(bc)
    const = lambda a: pl.BlockSpec(a.shape, lambda b: (0,) * a.ndim, pipeline_mode=pl.Buffered(1))
    y = pl.pallas_call(
        _att_sample_kernel,
        grid=(B,),
        in_specs=[pl.BlockSpec((None, T, 3 * ATT_GW), lambda b: (b, 0, Z_Q // (3 * ATT_GW))),
                  pl.BlockSpec((None, T, 6 * ATT_GW), lambda b: (b, 0, Z_KV // (6 * ATT_GW))),
                  *cspecs, *[const(a) for a in bh], const(bc)],
        out_specs=pl.BlockSpec((None, T, ATT_GW), lambda b: (b, 0, 0)),
        out_shape=jax.ShapeDtypeStruct((B, T, ATT_GW), jnp.float32),
        compiler_params=_cparams("parallel"),
        name="att_sample",
    )(z3, z3, *views, *bh, bc)
    return y.reshape(B * T, ATT_GW)


def _merge_kernel(yp_ref, yg_ref, *refs):
    att_refs, (gz_ref, x_ref, wb_ref, wo_ref, g2_ref, x1_ref, hn_ref) = refs[:-7], refs[-7:]
    if len(att_refs) == 1:
        ya = att_refs[0][...]
    else:
        o0_ref, o1_ref, o2_ref, l0_ref, l1_ref, l2_ref = att_refs
        l0, l1, l2 = l0_ref[...], l1_ref[...], l2_ref[...]
        lm = jnp.maximum(jnp.maximum(l0, l1), l2)
        e0, e1, e2 = jnp.exp(l0 - lm), jnp.exp(l1 - lm), jnp.exp(l2 - lm)
        ya = (e0 * o0_ref[...] + e1 * o1_ref[...] + e2 * o2_ref[...]) / (e0 + e1 + e2)
    acc = None
    for bi, y in enumerate((yp_ref[...], yg_ref[...], ya)):
        proj = jnp.dot(y.astype(jnp.bfloat16), wb_ref[bi], preferred_element_type=jnp.float32)
        gz = gz_ref[:, bi * D_MODEL:(bi + 1) * D_MODEL]
        term = proj / (1.0 + jnp.exp(-gz))
        acc = term if acc is None else acc + term
    x1 = x_ref[...] + jnp.dot(acc.astype(jnp.bfloat16), wo_ref[...], preferred_element_type=jnp.float32)
    x1_ref[...] = x1
    ms = jnp.mean(x1 * x1, axis=-1, keepdims=True)
    hn_ref[...] = (x1 * lax.rsqrt(ms + RMS_EPS) * g2_ref[...]).astype(hn_ref.dtype)


def _merge(yp, yg, att, z, x, w_branch, w_out, norm2_g, tm):
    n = x.shape[0]
    half = pl.BlockSpec((tm, 512), lambda i: (i, 0))
    full = pl.BlockSpec((tm, D_MODEL), lambda i: (i, 0))
    if isinstance(att, list):
        (o0, l0), (o1, l1), (o2, l2) = att
        att = (o0, o1, o2, l0, l1, l2)
    else:
        att = (att,)
    return pl.pallas_call(
        _merge_kernel,
        grid=(n // tm,),
        in_specs=[half] * (2 + len(att)) + [
            pl.BlockSpec((tm, 3 * D_MODEL), lambda i: (i, Z_GZ // (3 * D_MODEL))),
            full,
            pl.BlockSpec((3, 512, D_MODEL), lambda i: (0, 0, 0)),
            pl.BlockSpec((D_MODEL, D_MODEL), lambda i: (0, 0)),
            pl.BlockSpec((1, D_MODEL), lambda i: (0, 0))],
        out_specs=[full, full],
        out_shape=[jax.ShapeDtypeStruct((n, D_MODEL), jnp.float32),
                   jax.ShapeDtypeStruct((n, D_MODEL), jnp.bfloat16)],
        compiler_params=_cparams("parallel"),
        name="branch_merge",
    )(yp, yg, *att, z, x, w_branch.astype(jnp.bfloat16),
      w_out.astype(jnp.bfloat16), norm2_g.reshape(1, D_MODEL))


def _peer_select_kernel(q_ref, k1_ref, k2_ref, c1_ref, n2_ref, a2_ref, r2_ref):
    q = q_ref[...].astype(jnp.bfloat16)
    s1 = _dot_nt(k1_ref[...], q[:, :PEER_HALF])
    s2 = _dot_nt(k2_ref[...], q[:, PEER_HALF:])
    none = float(PEER_TOPK)

    def top(vals, count, ranked):
        out, work = [], vals
        rank = jnp.full(vals.shape, none, jnp.float32) if ranked else None
        for a in range(count):
            m = jnp.max(work, axis=0, keepdims=True)
            out.append(m)
            hit = work >= m
            if ranked:
                rank = jnp.where(hit, float(a), rank)
            work = jnp.where(hit, -jnp.inf, work)
        return out, rank

    v1, rank1 = top(s1, PEER_TOPK, True)
    v2, rank2 = top(s2, PEER_TOPK, True)
    rid = lax.broadcasted_iota(jnp.int32, (PEER_TOPK, s2.shape[1]), 0)
    v2all = jnp.zeros((PEER_TOPK, s2.shape[1]), jnp.float32)
    for b, vb in enumerate(v2):
        v2all = jnp.where(rid == b, vb, v2all)
    cand = jnp.concatenate([v1[0] + v2all] + [v1[a] + v2all[:8] for a in range(1, PEER_TOPK)], axis=0)
    best, _ = top(cand, PEER_TOPK, False)
    tau = best[PEER_TOPK - 1]
    den = jnp.zeros_like(tau)
    n2 = jnp.zeros_like(s1)
    for a in range(PEER_TOPK):
        pair = v1[a] + v2all
        keep = pair >= tau
        den = den + jnp.sum(jnp.where(keep, jnp.exp(pair - best[0]), 0.0), axis=0, keepdims=True)
        n2 = jnp.where(rank1 == float(a), jnp.sum(keep.astype(jnp.float32), axis=0, keepdims=True), n2)
    c1_ref[...] = jnp.where(rank1 < none, jnp.exp(s1 - v1[0]) / den, 0.0)
    n2_ref[...] = n2
    a2_ref[...] = jnp.where(rank2 < none, jnp.exp(s2 - v2[0]), 0.0).astype(a2_ref.dtype)
    r2_ref[...] = rank2.astype(r2_ref.dtype)


def _peer_select(q, k1, k2, tn):
    n = q.shape[0]
    keyspec = pl.BlockSpec((None, PEER_NKEYS, PEER_HALF), lambda i, h: (h, 0, 0))
    out = pl.BlockSpec((None, PEER_NKEYS, tn), lambda i, h: (h, 0, i))
    shape = lambda dt: jax.ShapeDtypeStruct((PEER_HEADS, PEER_NKEYS, n), dt)
    return pl.pallas_call(
        _peer_select_kernel,
        grid=(n // tn, PEER_HEADS),
        in_specs=[pl.BlockSpec((tn, 2 * PEER_HALF), lambda i, h: (i, h)), keyspec, keyspec],
        out_specs=[out] * 4,
        out_shape=[shape(jnp.float32), shape(jnp.float32), shape(jnp.bfloat16), shape(jnp.bfloat16)],
        compiler_params=_cparams("parallel", "parallel"),
        name="peer_select",
    )(q, k1.astype(jnp.bfloat16), k2.astype(jnp.bfloat16))


def _peer_expert_kernel(hn_ref, u_ref, v_ref, c1_ref, n2_ref, a2_ref, r2_ref, o_ref,
                        st_ref, p_ref, a2s_ref, r2s_ref):
    c = pl.program_id(1)

    @pl.when(c == 0)
    def _():
        o_ref[...] = jnp.zeros_like(o_ref)
        a2s_ref[...] = a2_ref[...]
        r2s_ref[...] = r2_ref[...]

    st_ref[...] = _dot_nt(u_ref[...], hn_ref[...])
    tn = st_ref.shape[1]
    sub = 16
    for ii in range(PEER_ROWS):
        for lb in range(tn // 128):
            lanes = slice(lb * 128, (lb + 1) * 128)
            row = lambda ref, h: jnp.broadcast_to(ref[h, ii:ii + 1, lanes], (sub, 128)).astype(jnp.bfloat16)
            n2 = [row(n2_ref, h) for h in range(PEER_HEADS)]
            c1 = [row(c1_ref, h) for h in range(PEER_HEADS)]
            for jb in range(PEER_NKEYS // sub):
                js = slice(jb * sub, (jb + 1) * sub)
                es = slice(ii * PEER_NKEYS + jb * sub, ii * PEER_NKEYS + (jb + 1) * sub)
                s = st_ref[es, lanes]
                act = (0.5 * s * (1.0 + lax.erf(s * (2.0 ** -0.5)))).astype(jnp.bfloat16)
                w = None
                for h in range(PEER_HEADS):
                    sel = jnp.where(r2s_ref[h, js, lanes] < n2[h], a2s_ref[h, js, lanes], jnp.bfloat16(0)) * c1[h]
                    w = sel if w is None else w + sel
                p_ref[es, lanes] = w * act
    o_ref[...] += _dot_tn(p_ref[...], v_ref[...])


def _peer_experts(hn, u_bf, v_bf, sel, tn):
    n = hn.shape[0]
    ce = PEER_ROWS * PEER_NKEYS
    c1, n2, a2, r2 = sel
    rowblk = pl.BlockSpec((PEER_HEADS, PEER_ROWS, tn), lambda i, c: (0, c, i))
    allblk = pl.BlockSpec((PEER_HEADS, PEER_NKEYS, tn), lambda i, c: (0, 0, i),
                          pipeline_mode=pl.Buffered(1))
    tok = pl.BlockSpec((tn, D_MODEL), lambda i, c: (i, 0))
    table = pl.BlockSpec((ce, D_MODEL), lambda i, c: (c, 0))
    return pl.pallas_call(
        _peer_expert_kernel,
        grid=(n // tn, PEER_EXPERTS // ce),
        in_specs=[tok, table, table, rowblk, rowblk, allblk, allblk],
        out_specs=tok,
        out_shape=jax.ShapeDtypeStruct((n, D_MODEL), jnp.float32),
        scratch_shapes=[pltpu.VMEM((ce, tn), jnp.float32),
                        pltpu.VMEM((ce, tn), jnp.bfloat16),
                        pltpu.VMEM((PEER_HEADS, PEER_NKEYS, tn), jnp.bfloat16),
                        pltpu.VMEM((PEER_HEADS, PEER_NKEYS, tn), jnp.bfloat16)],
        compiler_params=_cparams("parallel", "arbitrary"),
        name="peer_experts",
    )(hn, u_bf, v_bf, c1, n2, a2, r2)


def _arrange_w_in(w_in):
    sizes = (512, 256, 256, 512, 512, 16, 1536, 1536, 1536, 3072)
    offs = np.concatenate([[0], np.cumsum(sizes)])
    wt = w_in.T
    up, gq, gk, gv, gr, ga, aq, ak, av, gz = [wt[offs[i]:offs[i + 1]] for i in range(10)]
    kv = []
    for g in range(3):
        kv += [ak[g * ATT_GW:(g + 1) * ATT_GW], av[g * ATT_GW:(g + 1) * ATT_GW]]
    pad = jnp.zeros((Z_WIDTH - Z_GA - GLA_GATE_RANK, w_in.shape[0]), w_in.dtype)
    return jnp.concatenate([gz] + kv + [aq, up, gv, gr, gq, gk, ga, pad], axis=0).astype(jnp.bfloat16)


def _layer(x, hn, B, T, t_valid, pos0, pool_hist, gla_s0, caches, layer, rel_table, lw, tiles):
    (w_in_r, w_a2, b_a, w_pool, s_pool, gla_norm_g, w_branch, w_out, norm2_g,
     wq_bf, k1, k2, u_bf, v_bf) = lw
    tm, tt, chunk, tn_sel, tn = tiles
    z = _matmul(hn, w_in_r, tm, 2048, "in_proj", b_is_transposed=True)
    hist16 = jnp.pad(pool_hist, ((0, 0), (POOL_HALO - POOL_HIST, 0), (0, 0)))
    y_pool = _pool_mix(z, hist16, w_pool, s_pool, B, T, tt, pos0)
    y_gla, st = _gla(z, jnp.swapaxes(gla_s0, 2, 3), w_a2, b_a, gla_norm_g, B, T, chunk, t_valid)
    if caches is None:
        att = [_att_prompt(z, rel_table, gi, B, T) for gi in range(3)]
    else:
        att = _att_sample(z, caches, layer, rel_table, B, T)
    x1, hn2 = _merge(y_pool, y_gla, att, z, x, w_branch, w_out, norm2_g, tm)
    q = _matmul(hn2, wq_bf, tm, 1024, "peer_query")
    sel = _peer_select(q, k1, k2, tn_sel)
    delta = _peer_experts(hn2, u_bf, v_bf, sel, tn)
    z3 = z.reshape(B, T, Z_WIDTH)
    u_new = z3[:, :t_valid, Z_POOL:Z_POOL + POOL_WIDTH]
    pool_tail = jnp.concatenate([pool_hist, u_new], axis=1)[:, -POOL_HIST:]
    kv_new = [z3[:, :t_valid, Z_KV + 2 * ATT_GW * g:Z_KV + 2 * ATT_GW * (g + 1)]
              .reshape(B, t_valid, 2, ATT_HPG, ATT_HEAD_DIM) for g in range(3)]
    return x1, delta, pool_tail, jnp.swapaxes(st, 2, 3), kv_new


def kernel(x_prompt, x_sample, state_pool, state_gla, cache_att1, cache_att2, cache_att3, rel_table,
           norm1_g, w_in, w_a2, b_a, w_pool, s_pool, gla_norm_g, w_branch, w_out, norm2_g,
           peer_wq, peer_k1, peer_k2, peer_u, peer_v, final_norm_g):
    bp, tp, _ = x_prompt.shape
    bs, ts, _ = x_sample.shape
    depth = w_in.shape[0]
    ts_pad = 8
    caches = (cache_att1, cache_att2, cache_att3)
    xp = x_prompt.reshape(bp * tp, D_MODEL)
    xs = jnp.pad(x_sample, ((0, 0), (0, ts_pad - ts), (0, 0))).reshape(bs * ts_pad, D_MODEL)
    tiles_p = (512, 512, GLA_CHUNK, 512, 1024)
    rows_s = min(256, bs * ts_pad)
    tiles_s = (rows_s, ts_pad, ts_pad, rows_s, rows_s)
    pool_p, pool_s, gla_p, gla_s = [], [], [], []
    att_p = [[] for _ in range(3)]
    att_s = [[] for _ in range(3)]
    hp = _rmsnorm(xp, norm1_g[0], jnp.bfloat16, tiles_p[0])
    hs = _rmsnorm(xs, norm1_g[0], jnp.bfloat16, tiles_s[0])
    for l in range(depth):
        lw = (_arrange_w_in(w_in[l]), w_a2[l], b_a[l], w_pool[l], s_pool[l], gla_norm_g[l], w_branch[l],
              w_out[l], norm2_g[l], peer_wq[l].astype(jnp.bfloat16), peer_k1[l], peer_k2[l],
              peer_u[l].astype(jnp.bfloat16), peer_v[l].astype(jnp.bfloat16))
        xp, dp, pt, gp, kvp = _layer(
            xp, hp, bp, tp, tp, 0, jnp.zeros((bp, POOL_HIST, POOL_WIDTH), jnp.float32),
            jnp.zeros((bp, GLA_HEADS, GLA_DK, GLA_DV), jnp.float32), None, l, rel_table, lw, tiles_p)
        xs, ds, ps, gs, kvs = _layer(
            xs, hs, bs, ts_pad, ts, PAST_LEN, state_pool[l], state_gla[l], caches, l, rel_table, lw, tiles_s)
        last = l == depth - 1
        gain = final_norm_g if last else norm1_g[l + 1]
        xp, hp = _add_rmsnorm(xp, dp, gain, jnp.float32 if last else jnp.bfloat16, tiles_p[0])
        xs, hs = _add_rmsnorm(xs, ds, gain, jnp.float32 if last else jnp.bfloat16, tiles_s[0])
        pool_p.append(pt)
        pool_s.append(ps)
        gla_p.append(gp)
        gla_s.append(gs)
        for g, (w, _) in enumerate(ATT_GROUPS):
            att_p[g].append(kvp[g][:, tp - min(w, tp):])
            att_s[g].append(kvs[g])
    y_prompt = hp.reshape(bp, tp, D_MODEL)
    y_sample = hs.reshape(bs, ts_pad, D_MODEL)[:, :ts]
    outs = [y_prompt, y_sample, jnp.stack(pool_p), jnp.stack(pool_s), jnp.stack(gla_p), jnp.stack(gla_s)]
    for g in range(3):
        outs += [jnp.stack(att_p[g]), jnp.stack(att_s[g])]
    return tuple(outs)
```

```python
import functools
import math

import numpy as np
import jax
import jax.numpy as jnp
from jax import lax
from jax.experimental import pallas as pl
from jax.experimental.pallas import tpu as pltpu

D_MODEL = 1024
RMS_EPS = 1e-6
PAST_LEN = 8192

POOL_WINDOWS = (2, 4, 8, 16)
POOL_GROUP = 128
POOL_WIDTH = 512
POOL_HIST = 15
POOL_HALO = 16

GLA_HEADS = 4
GLA_DK = 64
GLA_DV = 128
GLA_KW = 256
GLA_VW = 512
GLA_GATE_RANK = 16
GLA_TAU = 16.0
GLA_CHUNK = 64

ATT_GROUPS = ((128, 1), (512, 4), (2048, 16))
ATT_HPG = 8
ATT_HEAD_DIM = 64
ATT_GW = ATT_HPG * ATT_HEAD_DIM
ATT_BAND = 128
ATT_QBLOCK = 256
REL_BUCKETS = 32
REL_MAX_DIST = 2048
NEG = -1e30

PEER_HEADS = 8
PEER_NKEYS = 128
PEER_EXPERTS = PEER_NKEYS * PEER_NKEYS
PEER_HALF = 128
PEER_TOPK = 16
PEER_ROWS = 8

Z_GZ = 0
Z_KV = 3072
Z_Q = 6144
Z_POOL = 7680
Z_GV = 8192
Z_GR = 8704
Z_GQ = 9216
Z_GK = 9472
Z_GA = 9728
Z_WIDTH = 10240

VMEM_LIMIT = 48 * 1024 * 1024


def _cparams(*sem):
    return pltpu.CompilerParams(dimension_semantics=sem, vmem_limit_bytes=VMEM_LIMIT)


def _dot_nt(a, b):
    return lax.dot_general(a, b, (((1,), (1,)), ((), ())), preferred_element_type=jnp.float32)


def _dot_tn(a, b):
    return lax.dot_general(a, b, (((0,), (0,)), ((), ())), preferred_element_type=jnp.float32)


def _rmsnorm_kernel(x_ref, g_ref, o_ref):
    x = x_ref[...]
    ms = jnp.mean(x * x, axis=-1, keepdims=True)
    o_ref[...] = (x * lax.rsqrt(ms + RMS_EPS) * g_ref[...]).astype(o_ref.dtype)


def _rmsnorm(x, g, out_dtype, tm):
    n, d = x.shape
    return pl.pallas_call(
        _rmsnorm_kernel,
        grid=(n // tm,),
        in_specs=[pl.BlockSpec((tm, d), lambda i: (i, 0)),
                  pl.BlockSpec((1, d), lambda i: (0, 0))],
        out_specs=pl.BlockSpec((tm, d), lambda i: (i, 0)),
        out_shape=jax.ShapeDtypeStruct((n, d), out_dtype),
        compiler_params=_cparams("parallel"),
        name="rmsnorm",
    )(x, g.reshape(1, d))


def _add_rmsnorm_kernel(x_ref, d_ref, g_ref, s_ref, o_ref):
    x = x_ref[...] + d_ref[...]
    s_ref[...] = x
    ms = jnp.mean(x * x, axis=-1, keepdims=True)
    o_ref[...] = (x * lax.rsqrt(ms + RMS_EPS) * g_ref[...]).astype(o_ref.dtype)


def _add_rmsnorm(x, delta, g, out_dtype, tm):
    n, d = x.shape
    blk = pl.BlockSpec((tm, d), lambda i: (i, 0))
    return pl.pallas_call(
        _add_rmsnorm_kernel,
        grid=(n // tm,),
        in_specs=[blk, blk, pl.BlockSpec((1, d), lambda i: (0, 0))],
        out_specs=[blk, blk],
        out_shape=[jax.ShapeDtypeStruct((n, d), jnp.float32), jax.ShapeDtypeStruct((n, d), out_dtype)],
        compiler_params=_cparams("parallel"),
        name="add_rmsnorm",
    )(x, delta, g.reshape(1, d))


def _mm_kernel(a_ref, b_ref, o_ref):
    o_ref[...] = jnp.dot(a_ref[...], b_ref[...], preferred_element_type=jnp.float32)


def _mm_nt_kernel(a_ref, b_ref, o_ref):
    o_ref[...] = _dot_nt(a_ref[...], b_ref[...])


def _matmul(a, b, tm, tn, name, b_is_transposed=False):
    m, k = a.shape
    n = b.shape[0] if b_is_transposed else b.shape[1]
    b_spec = (pl.BlockSpec((tn, k), lambda j, i: (j, 0)) if b_is_transposed
              else pl.BlockSpec((k, tn), lambda j, i: (0, j)))
    return pl.pallas_call(
        _mm_nt_kernel if b_is_transposed else _mm_kernel,
        grid=(n // tn, m // tm),
        in_specs=[pl.BlockSpec((tm, k), lambda j, i: (i, 0)), b_spec],
        out_specs=pl.BlockSpec((tm, tn), lambda j, i: (i, j)),
        out_shape=jax.ShapeDtypeStruct((m, n), jnp.float32),
        compiler_params=_cparams("parallel", "parallel"),
        name=name,
    )(a, b)


def _pool_kernel(u_ref, prev_ref, hist_ref, w_ref, s_ref, o_ref, ext_ref, *, tt, pos0):
    t = pl.program_id(1)
    ext_ref[0:POOL_HALO, :] = jnp.where(t == 0, hist_ref[...], prev_ref[...])
    u = u_ref[...]
    ext_ref[POOL_HALO:, :] = u
    pos = (pos0 + t * tt + lax.broadcasted_iota(jnp.int32, (tt, 1), 0)).astype(jnp.float32)
    for gi, w in enumerate(POOL_WINDOWS):
        lo, hi = gi * POOL_GROUP, (gi + 1) * POOL_GROUP
        s = ext_ref[POOL_HALO:POOL_HALO + tt, lo:hi]
        for back in range(1, w):
            s = s + ext_ref[POOL_HALO - back:POOL_HALO - back + tt, lo:hi]
        cnt = jnp.minimum(float(w), pos + 1.0)
        d = s / cnt - u[:, lo:hi]
        y = jnp.dot(d.astype(jnp.bfloat16), w_ref[gi], preferred_element_type=jnp.float32)
        o_ref[:, lo:hi] = y * s_ref[:, lo:hi]


def _pool_mix(z, hist16, w_pool, s_pool, B, T, tt, pos0):
    nt = T // tt
    cb = Z_POOL // POOL_WIDTH

    def prev_map(b, t):
        return (jnp.maximum(b * T + t * tt - POOL_HALO, 0) // POOL_HALO, cb)

    return pl.pallas_call(
        functools.partial(_pool_kernel, tt=tt, pos0=pos0),
        grid=(B, nt),
        in_specs=[pl.BlockSpec((tt, POOL_WIDTH), lambda b, t: (b * nt + t, cb)),
                  pl.BlockSpec((POOL_HALO, POOL_WIDTH), prev_map),
                  pl.BlockSpec((None, POOL_HALO, POOL_WIDTH), lambda b, t: (b, 0, 0)),
                  pl.BlockSpec((4, POOL_GROUP, POOL_GROUP), lambda b, t: (0, 0, 0)),
                  pl.BlockSpec((1, POOL_WIDTH), lambda b, t: (0, 0))],
        out_specs=pl.BlockSpec((tt, POOL_WIDTH), lambda b, t: (b * nt + t, 0)),
        out_shape=jax.ShapeDtypeStruct((B * T, POOL_WIDTH), jnp.float32),
        scratch_shapes=[pltpu.VMEM((POOL_HALO + tt, POOL_WIDTH), jnp.float32)],
        compiler_params=_cparams("parallel", "parallel"),
        name="pool_mix",
    )(z, z, hist16, w_pool.astype(jnp.bfloat16), s_pool.reshape(1, POOL_WIDTH))


def _gla_kernel(q_ref, k_ref, v_ref, r_ref, ga_ref, wa_ref, ba_ref, gn_ref, s0_ref,
                y_ref, sout_ref, st_ref, *, chunk, t_valid):
    c = pl.program_id(1)

    @pl.when(c == 0)
    def _():
        st_ref[...] = s0_ref[...]

    x = jnp.dot(ga_ref[...].astype(jnp.bfloat16), wa_ref[...],
                preferred_element_type=jnp.float32) + ba_ref[...]
    g = -(jnp.maximum(-x, 0.0) + jnp.log1p(jnp.exp(-jnp.abs(x)))) / GLA_TAU
    row = c * chunk + lax.broadcasted_iota(jnp.int32, (chunk, 1), 0)
    valid = row < t_valid
    g = jnp.where(valid, g, 0.0)
    ri = lax.broadcasted_iota(jnp.int32, (chunk, chunk), 0)
    ci = lax.broadcasted_iota(jnp.int32, (chunk, chunk), 1)
    causal = ri >= ci
    b = jnp.dot(causal.astype(jnp.float32), g, preferred_element_type=jnp.float32,
                precision=lax.Precision.HIGHEST)
    b_last = b[chunk - 1:chunk, :]
    k = k_ref[...]
    qe = (q_ref[...] * (GLA_DK ** -0.5) * jnp.exp(b)).astype(jnp.bfloat16)
    ke = (k * jnp.exp(-b)).astype(jnp.bfloat16)
    kl = jnp.where(valid, k * jnp.exp(b_last - b), 0.0).astype(jnp.bfloat16)
    a_last = jnp.exp(b_last)
    v = v_ref[...].astype(jnp.bfloat16)
    r = r_ref[...]
    for h in range(GLA_HEADS):
        ks = slice(h * GLA_DK, (h + 1) * GLA_DK)
        vs = slice(h * GLA_DV, (h + 1) * GLA_DV)
        att = jnp.where(causal, _dot_nt(qe[:, ks], ke[:, ks]), 0.0)
        st = st_ref[h]
        o = (jnp.dot(att.astype(jnp.bfloat16), v[:, vs], preferred_element_type=jnp.float32)
             + _dot_nt(qe[:, ks], st.astype(jnp.bfloat16)))
        st_ref[h] = st * a_last[:, ks] + _dot_tn(v[:, vs], kl[:, ks])
        o = o * lax.rsqrt(jnp.mean(o * o, axis=-1, keepdims=True) + RMS_EPS) * gn_ref[...]
        rh = r[:, vs]
        y_ref[:, vs] = o * (rh / (1.0 + jnp.exp(-rh)))

    @pl.when(c == pl.num_programs(1) - 1)
    def _():
        sout_ref[...] = st_ref[...]


def _gla(z, s0t, w_a2, b_a, gnorm, B, T, chunk, t_valid):
    nc = T // chunk
    wa = jnp.zeros((128, GLA_KW), jnp.float32).at[:GLA_GATE_RANK].set(w_a2).astype(jnp.bfloat16)
    row = lambda b, c: b * nc + c
    return pl.pallas_call(
        functools.partial(_gla_kernel, chunk=chunk, t_valid=t_valid),
        grid=(B, nc),
        in_specs=[pl.BlockSpec((chunk, GLA_KW), lambda b, c: (row(b, c), Z_GQ // GLA_KW)),
                  pl.BlockSpec((chunk, GLA_KW), lambda b, c: (row(b, c), Z_GK // GLA_KW)),
                  pl.BlockSpec((chunk, GLA_VW), lambda b, c: (row(b, c), Z_GV // GLA_VW)),
                  pl.BlockSpec((chunk, GLA_VW), lambda b, c: (row(b, c), Z_GR // GLA_VW)),
                  pl.BlockSpec((chunk, 128), lambda b, c: (row(b, c), Z_GA // 128)),
                  pl.BlockSpec((128, GLA_KW), lambda b, c: (0, 0)),
                  pl.BlockSpec((1, GLA_KW), lambda b, c: (0, 0)),
                  pl.BlockSpec((1, GLA_DV), lambda b, c: (0, 0)),
                  pl.BlockSpec((None, GLA_HEADS, GLA_DV, GLA_DK), lambda b, c: (b, 0, 0, 0))],
        out_specs=[pl.BlockSpec((chunk, GLA_VW), lambda b, c: (row(b, c), 0)),
                   pl.BlockSpec((None, GLA_HEADS, GLA_DV, GLA_DK), lambda b, c: (b, 0, 0, 0))],
        out_shape=[jax.ShapeDtypeStruct((B * T, GLA_VW), jnp.float32),
                   jax.ShapeDtypeStruct((B, GLA_HEADS, GLA_DV, GLA_DK), jnp.float32)],
        scratch_shapes=[pltpu.VMEM((GLA_HEADS, GLA_DV, GLA_DK), jnp.float32)],
        compiler_params=_cparams("parallel", "arbitrary"),
        name="gla_scan",
    )(z, z, z, z, z, wa, b_a.reshape(1, GLA_KW), gnorm.reshape(1, GLA_DV), s0t)


def _att_kernel(q_ref, kp_ref, vp_ref, kc_ref, vc_ref, bias_ref, o_ref, l_ref, k_ref, v_ref):
    kp_rows, qb = kp_ref.shape[0], q_ref.shape[0]
    k_ref[0:kp_rows, :] = kp_ref[...].astype(jnp.bfloat16)
    k_ref[kp_rows:, :] = kc_ref[...].astype(jnp.bfloat16)
    v_ref[0:kp_rows, :] = vp_ref[...].astype(jnp.bfloat16)
    v_ref[kp_rows:, :] = vc_ref[...].astype(jnp.bfloat16)
    q = (q_ref[...] * (ATT_HEAD_DIM ** -0.5)).astype(jnp.bfloat16)
    for h in range(ATT_HPG):
        sl = slice(h * ATT_HEAD_DIM, (h + 1) * ATT_HEAD_DIM)
        s = _dot_nt(q[:, sl], k_ref[:, sl]) + bias_ref[h]
        m = jnp.max(s, axis=-1, keepdims=True)
        p = jnp.exp(s - m)
        den = jnp.sum(p, axis=-1, keepdims=True)
        o = jnp.dot(p.astype(jnp.bfloat16), v_ref[:, sl], preferred_element_type=jnp.float32)
        o_ref[:, sl] = o / den
        l_ref[:, sl] = jnp.broadcast_to(m + jnp.log(den), (qb, ATT_HEAD_DIM))


def _rel_bucket(dist):
    d = np.asarray(dist, dtype=np.int64)
    max_exact = REL_BUCKETS // 2
    ratio = np.log(np.maximum(d, 1) / max_exact) / np.log(REL_MAX_DIST / max_exact)
    large = np.minimum(max_exact + (ratio * (REL_BUCKETS - max_exact)).astype(np.int64), REL_BUCKETS - 1)
    return np.where(d < max_exact, d, large).astype(np.int32)


def _att_bias(rel_table, gi, qb):
    _, dil = ATT_GROUPS[gi]
    cols = ATT_BAND + qb
    back = np.arange(ATT_BAND, -1, -1)
    tab = rel_table[_rel_bucket(back * dil)][:, gi * ATT_HPG:(gi + 1) * ATT_HPG].T.astype(jnp.float32)
    v = jnp.concatenate([tab, jnp.full((ATT_HPG, cols - ATT_BAND), NEG, jnp.float32)], axis=1)
    return jnp.tile(v, (1, qb))[:, :qb * cols].reshape(ATT_HPG, qb, cols)


def _att_prompt(z, rel_table, gi, B, T):
    _, dil = ATT_GROUPS[gi]
    qb = min(ATT_QBLOCK, T // dil)
    nm = T // dil // qb
    back = qb // ATT_BAND
    if dil == 1:
        zq = zkv = z.reshape(B, T, Z_WIDTH)
        per_q = per_kv = Z_WIDTH // ATT_GW
        cq, ck = Z_Q // ATT_GW + gi, Z_KV // ATT_GW + 2 * gi
    else:
        zq = z[:, Z_Q + gi * ATT_GW:Z_Q + (gi + 1) * ATT_GW].reshape(B, T // dil, dil * ATT_GW)
        zkv = z[:, Z_KV + 2 * gi * ATT_GW:Z_KV + 2 * (gi + 1) * ATT_GW].reshape(B, T // dil, dil * 2 * ATT_GW)
        per_q, per_kv, cq, ck = 1, 2, 0, 0
    cur = lambda per, col: pl.BlockSpec((None, qb, ATT_GW), lambda b, r, m: (b, m, r * per + col))
    prev = lambda per, col: pl.BlockSpec((None, ATT_BAND, ATT_GW),
                                         lambda b, r, m: (b, jnp.maximum(m * back - 1, 0), r * per + col))
    bias = _att_bias(rel_table, gi, qb)
    bias = jnp.stack([bias, bias.at[:, :, :ATT_BAND].set(NEG)])
    out = pl.BlockSpec((None, qb, ATT_GW), lambda b, r, m: (b, m, r))
    shape = jax.ShapeDtypeStruct((B, T // dil, dil * ATT_GW), jnp.float32)
    o, l = pl.pallas_call(
        _att_kernel,
        grid=(B, dil, nm),
        in_specs=[cur(per_q, cq), prev(per_kv, ck), prev(per_kv, ck + 1), cur(per_kv, ck), cur(per_kv, ck + 1),
                  pl.BlockSpec((None,) + bias.shape[1:], lambda b, r, m: (jnp.where(m == 0, 1, 0), 0, 0, 0))],
        out_specs=[out, out],
        out_shape=[shape, shape],
        scratch_shapes=[pltpu.VMEM((ATT_BAND + qb, ATT_GW), jnp.bfloat16),
                        pltpu.VMEM((ATT_BAND + qb, ATT_GW), jnp.bfloat16)],
        compiler_params=_cparams("parallel", "parallel", "parallel"),
        name="att_prompt_g%d" % gi,
    )(zq, zkv, zkv, zkv, zkv, bias)
    return o.reshape(B * T, ATT_GW), l.reshape(B * T, ATT_GW)


def _att_sample_kernel(q_ref, kv_ref, c0_ref, c1_ref, c2_ref, b0_ref, b1_ref, b2_ref, bc_ref, o_ref):
    bf = jnp.bfloat16
    T = q_ref.shape[0]
    rows = ATT_HPG * T
    own = (lax.broadcasted_iota(jnp.int32, (rows, ATT_GW), 0) // T
           == lax.broadcasted_iota(jnp.int32, (rows, ATT_GW), 1) // ATT_HEAD_DIM)
    nums, dens, lses = [], [], []
    for gi, (c_ref, bh_ref) in enumerate(((c0_ref, b0_ref), (c1_ref, b1_ref), (c2_ref, b2_ref))):
        width = c_ref.shape[-1]
        q = q_ref[:, gi * ATT_GW:(gi + 1) * ATT_GW] * (ATT_HEAD_DIM ** -0.5)
        qblk = jnp.where(own, jnp.concatenate([q] * ATT_HPG, axis=0), 0.0).astype(bf)
        kc = kv_ref[:, 2 * gi * ATT_GW:(2 * gi + 1) * ATT_GW].astype(bf)
        vc = kv_ref[:, (2 * gi + 1) * ATT_GW:(2 * gi + 2) * ATT_GW].astype(bf)
        kt = c_ref[0].reshape(ATT_GW, width).astype(bf)
        vt = c_ref[1].reshape(ATT_GW, width).astype(bf)
        s_h = jnp.dot(qblk, kt, preferred_element_type=jnp.float32) + bh_ref[...]
        s_c = _dot_nt(qblk, kc) + bc_ref[gi]
        m = jnp.maximum(jnp.max(s_h, axis=-1, keepdims=True), jnp.max(s_c, axis=-1, keepdims=True))
        p_h = jnp.exp(s_h - m)
        p_c = jnp.exp(s_c - m)
        den = jnp.sum(p_h, axis=-1, keepdims=True) + jnp.sum(p_c, axis=-1, keepdims=True)
        nums.append(_dot_nt(p_h.astype(bf), vt) + jnp.dot(p_c.astype(bf), vc, preferred_element_type=jnp.float32))
        dens.append(den)
        lses.append(m + jnp.log(den))
    lm = jnp.maximum(jnp.maximum(lses[0], lses[1]), lses[2])
    es = [jnp.exp(l - lm) for l in lses]
    tot = es[0] + es[1] + es[2]
    y = sum(nums[g] * (es[g] / (dens[g] * tot)) for g in range(3))
    y = jnp.where(own, y, 0.0)
    o_ref[...] = jnp.sum(y.reshape(ATT_HPG, T, ATT_GW), axis=0)


def _att_sample(z, caches, layer, rel_table, B, T):
    z3 = z.reshape(B, T, Z_WIDTH)
    views, cspecs, bh, bc = [], [], [], []
    tq = np.arange(T)[:, None]
    for gi, (width, dil) in enumerate(ATT_GROUPS):
        cache = caches[gi]
        assert cache.shape[2] == width
        views.append(jnp.transpose(cache, (0, 1, 3, 4, 5, 2)))
        cspecs.append(pl.BlockSpec((None, None, 2, ATT_HPG, ATT_HEAD_DIM, width),
                                   lambda b: (layer, b, 0, 0, 0, 0)))
        steps = np.arange(ATT_BAND + 1)
        tab = rel_table[_rel_bucket(steps * dil)][:, gi * ATT_HPG:(gi + 1) * ATT_HPG].T.astype(jnp.float32)
        for dist, dst in ((width + tq - np.arange(width)[None, :], bh), (tq - np.arange(T)[None, :], bc)):
            ok = (dist >= 0) & (dist % dil == 0) & (dist // dil <= ATT_BAND)
            j = np.where(ok, dist // dil, 0)
            bias = jnp.where(jnp.asarray(ok)[None], tab[:, j], NEG)
            dst.append(bias.reshape(ATT_HPG * T, bias.shape[-1]))
    bc = jnp.stack(bc)
    const = lambda a: pl.BlockSpec(a.shape, lambda b: (0,) * a.ndim, pipeline_mode=pl.Buffered(1))
    y = pl.pallas_call(
        _att_sample_kernel,
        grid=(B,),
        in_specs=[pl.BlockSpec((None, T, 3 * ATT_GW), lambda b: (b, 0, Z_Q // (3 * ATT_GW))),
                  pl.BlockSpec((None, T, 6 * ATT_GW), lambda b: (b, 0, Z_KV // (6 * ATT_GW))),
                  *cspecs, *[const(a) for a in bh], const(bc)],
        out_specs=pl.BlockSpec((None, T, ATT_GW), lambda b: (b, 0, 0)),
        out_shape=jax.ShapeDtypeStruct((B, T, ATT_GW), jnp.float32),
        compiler_params=_cparams("parallel"),
        name="att_sample",
    )(z3, z3, *views, *bh, bc)
    return y.reshape(B * T, ATT_GW)


def _merge_kernel(yp_ref, yg_ref, *refs):
    att_refs, (gz_ref, x_ref, wb_ref, wo_ref, g2_ref, x1_ref, hn_ref) = refs[:-7], refs[-7:]
    if len(att_refs) == 1:
        ya = att_refs[0][...]
    else:
        o0_ref, o1_ref, o2_ref, l0_ref, l1_ref, l2_ref = att_refs
        l0, l1, l2 = l0_ref[...], l1_ref[...], l2_ref[...]
        lm = jnp.maximum(jnp.maximum(l0, l1), l2)
        e0, e1, e2 = jnp.exp(l0 - lm), jnp.exp(l1 - lm), jnp.exp(l2 - lm)
        ya = (e0 * o0_ref[...] + e1 * o1_ref[...] + e2 * o2_ref[...]) / (e0 + e1 + e2)
    acc = None
    for bi, y in enumerate((yp_ref[...], yg_ref[...], ya)):
        proj = jnp.dot(y.astype(jnp.bfloat16), wb_ref[bi], preferred_element_type=jnp.float32)
        gz = gz_ref[:, bi * D_MODEL:(bi + 1) * D_MODEL]
        term = proj / (1.0 + jnp.exp(-gz))
        acc = term if acc is None else acc + term
    x1 = x_ref[...] + jnp.dot(acc.astype(jnp.bfloat16), wo_ref[...], preferred_element_type=jnp.float32)
    x1_ref[...] = x1
    ms = jnp.mean(x1 * x1, axis=-1, keepdims=True)
    hn_ref[...] = (x1 * lax.rsqrt(ms + RMS_EPS) * g2_ref[...]).astype(hn_ref.dtype)


def _merge(yp, yg, att, z, x, w_branch, w_out, norm2_g, tm):
    n = x.shape[0]
    half = pl.BlockSpec((tm, 512), lambda i: (i, 0))
    full = pl.BlockSpec((tm, D_MODEL), lambda i: (i, 0))
    if isinstance(att, list):
        (o0, l0), (o1, l1), (o2, l2) = att
        att = (o0, o1, o2, l0, l1, l2)
    else:
        att = (att,)
    return pl.pallas_call(
        _merge_kernel,
        grid=(n // tm,),
        in_specs=[half] * (2 + len(att)) + [
            pl.BlockSpec((tm, 3 * D_MODEL), lambda i: (i, Z_GZ // (3 * D_MODEL))),
            full,
            pl.BlockSpec((3, 512, D_MODEL), lambda i: (0, 0, 0)),
            pl.BlockSpec((D_MODEL, D_MODEL), lambda i: (0, 0)),
            pl.BlockSpec((1, D_MODEL), lambda i: (0, 0))],
        out_specs=[full, full],
        out_shape=[jax.ShapeDtypeStruct((n, D_MODEL), jnp.float32),
                   jax.ShapeDtypeStruct((n, D_MODEL), jnp.bfloat16)],
        compiler_params=_cparams("parallel"),
        name="branch_merge",
    )(yp, yg, *att, z, x, w_branch.astype(jnp.bfloat16),
      w_out.astype(jnp.bfloat16), norm2_g.reshape(1, D_MODEL))


def _peer_select_kernel(q_ref, k1_ref, k2_ref, c1_ref, n2_ref, a2_ref, r2_ref):
    q = q_ref[...].astype(jnp.bfloat16)
    s1 = _dot_nt(k1_ref[...], q[:, :PEER_HALF])
    s2 = _dot_nt(k2_ref[...], q[:, PEER_HALF:])
    none = float(PEER_TOPK)

    def top(vals, count, ranked):
        out, work = [], vals
        rank = jnp.full(vals.shape, none, jnp.float32) if ranked else None
        for a in range(count):
            m = jnp.max(work, axis=0, keepdims=True)
            out.append(m)
            hit = work >= m
            if ranked:
                rank = jnp.where(hit, float(a), rank)
            work = jnp.where(hit, -jnp.inf, work)
        return out, rank

    v1, rank1 = top(s1, PEER_TOPK, True)
    v2, rank2 = top(s2, PEER_TOPK, True)
    rid = lax.broadcasted_iota(jnp.int32, (PEER_TOPK, s2.shape[1]), 0)
    v2all = jnp.zeros((PEER_TOPK, s2.shape[1]), jnp.float32)
    for b, vb in enumerate(v2):
        v2all = jnp.where(rid == b, vb, v2all)
    cand = jnp.concatenate([v1[0] + v2all] + [v1[a] + v2all[:8] for a in range(1, PEER_TOPK)], axis=0)
    best, _ = top(cand, PEER_TOPK, False)
    tau = best[PEER_TOPK - 1]
    den = jnp.zeros_like(tau)
    n2 = jnp.zeros_like(s1)
    for a in range(PEER_TOPK):
        pair = v1[a] + v2all
        keep = pair >= tau
        den = den + jnp.sum(jnp.where(keep, jnp.exp(pair - best[0]), 0.0), axis=0, keepdims=True)
        n2 = jnp.where(rank1 == float(a), jnp.sum(keep.astype(jnp.float32), axis=0, keepdims=True), n2)
    c1_ref[...] = jnp.where(rank1 < none, jnp.exp(s1 - v1[0]) / den, 0.0)
    n2_ref[...] = n2
    a2_ref[...] = jnp.where(rank2 < none, jnp.exp(s2 - v2[0]), 0.0).astype(a2_ref.dtype)
    r2_ref[...] = rank2.astype(r2_ref.dtype)


def _peer_select(q, k1, k2, tn):
    n = q.shape[0]
    keyspec = pl.BlockSpec((None, PEER_NKEYS, PEER_HALF), lambda i, h: (h, 0, 0))
    out = pl.BlockSpec((None, PEER_NKEYS, tn), lambda i, h: (h, 0, i))
    shape = lambda dt: jax.ShapeDtypeStruct((PEER_HEADS, PEER_NKEYS, n), dt)
    return pl.pallas_call(
        _peer_select_kernel,
        grid=(n // tn, PEER_HEADS),
        in_specs=[pl.BlockSpec((tn, 2 * PEER_HALF), lambda i, h: (i, h)), keyspec, keyspec],
        out_specs=[out] * 4,
        out_shape=[shape(jnp.float32), shape(jnp.float32), shape(jnp.bfloat16), shape(jnp.bfloat16)],
        compiler_params=_cparams("parallel", "parallel"),
        name="peer_select",
    )(q, k1.astype(jnp.bfloat16), k2.astype(jnp.bfloat16))


def _peer_expert_kernel(hn_ref, u_ref, v_ref, c1_ref, n2_ref, a2_ref, r2_ref, o_ref,
                        st_ref, p_ref, a2s_ref, r2s_ref):
    c = pl.program_id(1)

    @pl.when(c == 0)
    def _():
        o_ref[...] = jnp.zeros_like(o_ref)
        a2s_ref[...] = a2_ref[...]
        r2s_ref[...] = r2_ref[...]

    st_ref[...] = _dot_nt(u_ref[...], hn_ref[...])
    tn = st_ref.shape[1]
    sub = 16
    for ii in range(PEER_ROWS):
        for lb in range(tn // 128):
            lanes = slice(lb * 128, (lb + 1) * 128)
            row = lambda ref, h: jnp.broadcast_to(ref[h, ii:ii + 1, lanes], (sub, 128)).astype(jnp.bfloat16)
            n2 = [row(n2_ref, h) for h in range(PEER_HEADS)]
            c1 = [row(c1_ref, h) for h in range(PEER_HEADS)]
            for jb in range(PEER_NKEYS // sub):
                js = slice(jb * sub, (jb + 1) * sub)
                es = slice(ii * PEER_NKEYS + jb * sub, ii * PEER_NKEYS + (jb + 1) * sub)
                s = st_ref[es, lanes]
                act = (0.5 * s * (1.0 + lax.erf(s * (2.0 ** -0.5)))).astype(jnp.bfloat16)
                w = None
                for h in range(PEER_HEADS):
                    sel = jnp.where(r2s_ref[h, js, lanes] < n2[h], a2s_ref[h, js, lanes], jnp.bfloat16(0)) * c1[h]
                    w = sel if w is None else w + sel
                p_ref[es, lanes] = w * act
    o_ref[...] += _dot_tn(p_ref[...], v_ref[...])


def _peer_experts(hn, u_bf, v_bf, sel, tn):
    n = hn.shape[0]
    ce = PEER_ROWS * PEER_NKEYS
    c1, n2, a2, r2 = sel
    rowblk = pl.BlockSpec((PEER_HEADS, PEER_ROWS, tn), lambda i, c: (0, c, i))
    allblk = pl.BlockSpec((PEER_HEADS, PEER_NKEYS, tn), lambda i, c: (0, 0, i),
                          pipeline_mode=pl.Buffered(1))
    tok = pl.BlockSpec((tn, D_MODEL), lambda i, c: (i, 0))
    table = pl.BlockSpec((ce, D_MODEL), lambda i, c: (c, 0))
    return pl.pallas_call(
        _peer_expert_kernel,
        grid=(n // tn, PEER_EXPERTS // ce),
        in_specs=[tok, table, table, rowblk, rowblk, allblk, allblk],
        out_specs=tok,
        out_shape=jax.ShapeDtypeStruct((n, D_MODEL), jnp.float32),
        scratch_shapes=[pltpu.VMEM((ce, tn), jnp.float32),
                        pltpu.VMEM((ce, tn), jnp.bfloat16),
                        pltpu.VMEM((PEER_HEADS, PEER_NKEYS, tn), jnp.bfloat16),
                        pltpu.VMEM((PEER_HEADS, PEER_NKEYS, tn), jnp.bfloat16)],
        compiler_params=_cparams("parallel", "arbitrary"),
        name="peer_experts",
    )(hn, u_bf, v_bf, c1, n2, a2, r2)


def _arrange_w_in(w_in):
    sizes = (512, 256, 256, 512, 512, 16, 1536, 1536, 1536, 3072)
    offs = np.concatenate([[0], np.cumsum(sizes)])
    wt = w_in.T
    up, gq, gk, gv, gr, ga, aq, ak, av, gz = [wt[offs[i]:offs[i + 1]] for i in range(10)]
    kv = []
    for g in range(3):
        kv += [ak[g * ATT_GW:(g + 1) * ATT_GW], av[g * ATT_GW:(g + 1) * ATT_GW]]
    pad = jnp.zeros((Z_WIDTH - Z_GA - GLA_GATE_RANK, w_in.shape[0]), w_in.dtype)
    return jnp.concatenate([gz] + kv + [aq, up, gv, gr, gq, gk, ga, pad], axis=0).astype(jnp.bfloat16)


def _layer(x, hn, B, T, t_valid, pos0, pool_hist, gla_s0, caches, layer, rel_table, lw, tiles):
    (w_in_r, w_a2, b_a, w_pool, s_pool, gla_norm_g, w_branch, w_out, norm2_g,
     wq_bf, k1, k2, u_bf, v_bf) = lw
    tm, tt, chunk, tn_sel, tn = tiles
    z = _matmul(hn, w_in_r, tm, 2048, "in_proj", b_is_transposed=True)
    hist16 = jnp.pad(pool_hist, ((0, 0), (POOL_HALO - POOL_HIST, 0), (0, 0)))
    y_pool = _pool_mix(z, hist16, w_pool, s_pool, B, T, tt, pos0)
    y_gla, st = _gla(z, jnp.swapaxes(gla_s0, 2, 3), w_a2, b_a, gla_norm_g, B, T, chunk, t_valid)
    if caches is None:
        att = [_att_prompt(z, rel_table, gi, B, T) for gi in range(3)]
    else:
        att = _att_sample(z, caches, layer, rel_table, B, T)
    x1, hn2 = _merge(y_pool, y_gla, att, z, x, w_branch, w_out, norm2_g, tm)
    q = _matmul(hn2, wq_bf, tm, 1024, "peer_query")
    sel = _peer_select(q, k1, k2, tn_sel)
    delta = _peer_experts(hn2, u_bf, v_bf, sel, tn)
    z3 = z.reshape(B, T, Z_WIDTH)
    u_new = z3[:, :t_valid, Z_POOL:Z_POOL + POOL_WIDTH]
    pool_tail = jnp.concatenate([pool_hist, u_new], axis=1)[:, -POOL_HIST:]
    kv_new = [z3[:, :t_valid, Z_KV + 2 * ATT_GW * g:Z_KV + 2 * ATT_GW * (g + 1)]
              .reshape(B, t_valid, 2, ATT_HPG, ATT_HEAD_DIM) for g in range(3)]
    return x1, delta, pool_tail, jnp.swapaxes(st, 2, 3), kv_new


def kernel(x_prompt, x_sample, state_pool, state_gla, cache_att1, cache_att2, cache_att3, rel_table,
           norm1_g, w_in, w_a2, b_a, w_pool, s_pool, gla_norm_g, w_branch, w_out, norm2_g,
           peer_wq, peer_k1, peer_k2, peer_u, peer_v, final_norm_g):
    bp, tp, _ = x_prompt.shape
    bs, ts, _ = x_sample.shape
    depth = w_in.shape[0]
    ts_pad = 8
    caches = (cache_att1, cache_att2, cache_att3)
    xp = x_prompt.reshape(bp * tp, D_MODEL)
    xs = jnp.pad(x_sample, ((0, 0), (0, ts_pad - ts), (0, 0))).reshape(bs * ts_pad, D_MODEL)
    tiles_p = (512, 512, GLA_CHUNK, 512, 1024)
    rows_s = min(256, bs * ts_pad)
    tiles_s = (rows_s, ts_pad, ts_pad, rows_s, rows_s)
    pool_p, pool_s, gla_p, gla_s = [], [], [], []
    att_p = [[] for _ in range(3)]
    att_s = [[] for _ in range(3)]
    hp = _rmsnorm(xp, norm1_g[0], jnp.bfloat16, tiles_p[0])
    hs = _rmsnorm(xs, norm1_g[0], jnp.bfloat16, tiles_s[0])
    for l in range(depth):
        lw = (_arrange_w_in(w_in[l]), w_a2[l], b_a[l], w_pool[l], s_pool[l], gla_norm_g[l], w_branch[l],
              w_out[l], norm2_g[l], peer_wq[l].astype(jnp.bfloat16), peer_k1[l], peer_k2[l],
              peer_u[l].astype(jnp.bfloat16), peer_v[l].astype(jnp.bfloat16))
        xp, dp, pt, gp, kvp = _layer(
            xp, hp, bp, tp, tp, 0, jnp.zeros((bp, POOL_HIST, POOL_WIDTH), jnp.float32),
            jnp.zeros((bp, GLA_HEADS, GLA_DK, GLA_DV), jnp.float32), None, l, rel_table, lw, tiles_p)
        xs, ds, ps, gs, kvs = _layer(
            xs, hs, bs, ts_pad, ts, PAST_LEN, state_pool[l], state_gla[l], caches, l, rel_table, lw, tiles_s)
        last = l == depth - 1
        gain = final_norm_g if last else norm1_g[l + 1]
        xp, hp = _add_rmsnorm(xp, dp, gain, jnp.float32 if last else jnp.bfloat16, tiles_p[0])
        xs, hs = _add_rmsnorm(xs, ds, gain, jnp.float32 if last else jnp.bfloat16, tiles_s[0])
        pool_p.append(pt)
        pool_s.append(ps)
        gla_p.append(gp)
        gla_s.append(gs)
        for g, (w, _) in enumerate(ATT_GROUPS):
            att_p[g].append(kvp[g][:, tp - min(w, tp):])
            att_s[g].append(kvs[g])
    y_prompt = hp.reshape(bp, tp, D_MODEL)
    y_sample = hs.reshape(bs, ts_pad, D_MODEL)[:, :ts]
    outs = [y_prompt, y_sample, jnp.stack(pool_p), jnp.stack(pool_s), jnp.stack(gla_p), jnp.stack(gla_s)]
    for g in range(3):
        outs += [jnp.stack(att_p[g]), jnp.stack(att_s[g])]
    return tuple(outs)
```

```python
import functools
import math

import numpy as np
import jax
import jax.numpy as jnp
from jax import lax
from jax.experimental import pallas as pl
from jax.experimental.pallas import tpu as pltpu

D_MODEL = 1024
RMS_EPS = 1e-6
PAST_LEN = 8192

POOL_WINDOWS = (2, 4, 8, 16)
POOL_GROUP = 128
POOL_WIDTH = 512
POOL_HIST = 15
POOL_HALO = 16

GLA_HEADS = 4
GLA_DK = 64
GLA_DV = 128
GLA_KW = 256
GLA_VW = 512
GLA_GATE_RANK = 16
GLA_TAU = 16.0
GLA_CHUNK = 64
GLA_SEQS = 4

ATT_GROUPS = ((128, 1), (512, 4), (2048, 16))
ATT_HPG = 8
ATT_HEAD_DIM = 64
ATT_GW = ATT_HPG * ATT_HEAD_DIM
ATT_BAND = 128
ATT_QBLOCK = 256
REL_BUCKETS = 32
REL_MAX_DIST = 2048
NEG = -1e30

PEER_HEADS = 8
PEER_NKEYS = 128
PEER_EXPERTS = PEER_NKEYS * PEER_NKEYS
PEER_HALF = 128
PEER_TOPK = 16
PEER_ROWS = 8

Z_GZ = 0
Z_KV = 3072
Z_Q = 6144
Z_POOL = 7680
Z_GV = 8192
Z_GR = 8704
Z_GQ = 9216
Z_GK = 9472
Z_GA = 9728
Z_WIDTH = 10240

VMEM_LIMIT = 48 * 1024 * 1024


def _cparams(*sem):
    return pltpu.CompilerParams(dimension_semantics=sem, vmem_limit_bytes=VMEM_LIMIT)


def _dot_nt(a, b):
    return lax.dot_general(a, b, (((1,), (1,)), ((), ())), preferred_element_type=jnp.float32)


def _dot_tn(a, b):
    return lax.dot_general(a, b, (((0,), (0,)), ((), ())), preferred_element_type=jnp.float32)


def _rmsnorm_kernel(x_ref, g_ref, o_ref):
    x = x_ref[...]
    ms = jnp.mean(x * x, axis=-1, keepdims=True)
    o_ref[...] = (x * lax.rsqrt(ms + RMS_EPS) * g_ref[...]).astype(o_ref.dtype)


def _rmsnorm(x, g, out_dtype, tm):
    n, d = x.shape
    return pl.pallas_call(
        _rmsnorm_kernel,
        grid=(n // tm,),
        in_specs=[pl.BlockSpec((tm, d), lambda i: (i, 0)),
                  pl.BlockSpec((1, d), lambda i: (0, 0))],
        out_specs=pl.BlockSpec((tm, d), lambda i: (i, 0)),
        out_shape=jax.ShapeDtypeStruct((n, d), out_dtype),
        compiler_params=_cparams("parallel"),
        name="rmsnorm",
    )(x, g.reshape(1, d))


def _add_rmsnorm_kernel(x_ref, d_ref, g_ref, s_ref, o_ref):
    x = x_ref[...] + d_ref[...]
    s_ref[...] = x
    ms = jnp.mean(x * x, axis=-1, keepdims=True)
    o_ref[...] = (x * lax.rsqrt(ms + RMS_EPS) * g_ref[...]).astype(o_ref.dtype)


def _add_rmsnorm(x, delta, g, out_dtype, tm):
    n, d = x.shape
    blk = pl.BlockSpec((tm, d), lambda i: (i, 0))
    return pl.pallas_call(
        _add_rmsnorm_kernel,
        grid=(n // tm,),
        in_specs=[blk, blk, pl.BlockSpec((1, d), lambda i: (0, 0))],
        out_specs=[blk, blk],
        out_shape=[jax.ShapeDtypeStruct((n, d), jnp.float32), jax.ShapeDtypeStruct((n, d), out_dtype)],
        compiler_params=_cparams("parallel"),
        name="add_rmsnorm",
    )(x, delta, g.reshape(1, d))


def _mm_kernel(a_ref, b_ref, o_ref):
    o_ref[...] = jnp.dot(a_ref[...], b_ref[...], preferred_element_type=jnp.float32)


def _mm_nt_kernel(a_ref, b_ref, o_ref):
    o_ref[...] = _dot_nt(a_ref[...], b_ref[...])


def _matmul(a, b, tm, tn, name, b_is_transposed=False):
    m, k = a.shape
    n = b.shape[0] if b_is_transposed else b.shape[1]
    b_spec = (pl.BlockSpec((tn, k), lambda j, i: (j, 0)) if b_is_transposed
              else pl.BlockSpec((k, tn), lambda j, i: (0, j)))
    return pl.pallas_call(
        _mm_nt_kernel if b_is_transposed else _mm_kernel,
        grid=(n // tn, m // tm),
        in_specs=[pl.BlockSpec((tm, k), lambda j, i: (i, 0)), b_spec],
        out_specs=pl.BlockSpec((tm, tn), lambda j, i: (i, j)),
        out_shape=jax.ShapeDtypeStruct((m, n), jnp.float32),
        compiler_params=_cparams("parallel", "parallel"),
        name=name,
    )(a, b)


def _pool_kernel(u_ref, prev_ref, hist_ref, w_ref, s_ref, o_ref, ext_ref, *, tt, pos0):
    t = pl.program_id(1)
    ext_ref[0:POOL_HALO, :] = jnp.where(t == 0, hist_ref[...], prev_ref[...])
    u = u_ref[...]
    ext_ref[POOL_HALO:, :] = u
    pos = (pos0 + t * tt + lax.broadcasted_iota(jnp.int32, (tt, 1), 0)).astype(jnp.float32)
    for gi, w in enumerate(POOL_WINDOWS):
        lo, hi = gi * POOL_GROUP, (gi + 1) * POOL_GROUP
        s = ext_ref[POOL_HALO:POOL_HALO + tt, lo:hi]
        for back in range(1, w):
            s = s + ext_ref[POOL_HALO - back:POOL_HALO - back + tt, lo:hi]
        cnt = jnp.minimum(float(w), pos + 1.0)
        d = s / cnt - u[:, lo:hi]
        y = jnp.dot(d.astype(jnp.bfloat16), w_ref[gi], preferred_element_type=jnp.float32)
        o_ref[:, lo:hi] = y * s_ref[:, lo:hi]


def _pool_mix(z, hist16, w_pool, s_pool, B, T, tt, pos0):
    nt = T // tt
    cb = Z_POOL // POOL_WIDTH

    def prev_map(b, t):
        return (jnp.maximum(b * T + t * tt - POOL_HALO, 0) // POOL_HALO, cb)

    return pl.pallas_call(
        functools.partial(_pool_kernel, tt=tt, pos0=pos0),
        grid=(B, nt),
        in_specs=[pl.BlockSpec((tt, POOL_WIDTH), lambda b, t: (b * nt + t, cb)),
                  pl.BlockSpec((POOL_HALO, POOL_WIDTH), prev_map),
                  pl.BlockSpec((None, POOL_HALO, POOL_WIDTH), lambda b, t: (b, 0, 0)),
                  pl.BlockSpec((4, POOL_GROUP, POOL_GROUP), lambda b, t: (0, 0, 0)),
                  pl.BlockSpec((1, POOL_WIDTH), lambda b, t: (0, 0))],
        out_specs=pl.BlockSpec((tt, POOL_WIDTH), lambda b, t: (b * nt + t, 0)),
        out_shape=jax.ShapeDtypeStruct((B * T, POOL_WIDTH), jnp.float32),
        scratch_shapes=[pltpu.VMEM((POOL_HALO + tt, POOL_WIDTH), jnp.float32)],
        compiler_params=_cparams("parallel", "parallel"),
        name="pool_mix",
    )(z, z, hist16, w_pool.astype(jnp.bfloat16), s_pool.reshape(1, POOL_WIDTH))


def _gla_kernel(q_ref, k_ref, v_ref, r_ref, ga_ref, wa_ref, ba_ref, gn_ref, s0_ref,
                y_ref, sout_ref, st_ref, *, chunk, t_valid):
    c = pl.program_id(1)

    @pl.when(c == 0)
    def _():
        st_ref[...] = s0_ref[...]

    row = c * chunk + lax.broadcasted_iota(jnp.int32, (chunk, 1), 0)
    valid = row < t_valid
    ri = lax.broadcasted_iota(jnp.int32, (chunk, chunk), 0)
    ci = lax.broadcasted_iota(jnp.int32, (chunk, chunk), 1)
    causal = ri >= ci
    for s in range(q_ref.shape[0]):
        x = jnp.dot(ga_ref[s].astype(jnp.bfloat16), wa_ref[...],
                    preferred_element_type=jnp.float32) + ba_ref[...]
        g = -(jnp.maximum(-x, 0.0) + jnp.log1p(jnp.exp(-jnp.abs(x)))) / GLA_TAU
        g = jnp.where(valid, g, 0.0)
        b = jnp.dot(causal.astype(jnp.float32), g, preferred_element_type=jnp.float32,
                    precision=lax.Precision.HIGHEST)
        b_last = b[chunk - 1:chunk, :]
        k = k_ref[s]
        qe = (q_ref[s] * (GLA_DK ** -0.5) * jnp.exp(b)).astype(jnp.bfloat16)
        ke = (k * jnp.exp(-b)).astype(jnp.bfloat16)
        kl = jnp.where(valid, k * jnp.exp(b_last - b), 0.0).astype(jnp.bfloat16)
        a_last = jnp.exp(b_last)
        v = v_ref[s].astype(jnp.bfloat16)
        r = r_ref[s]
        for h in range(GLA_HEADS):
            ks = slice(h * GLA_DK, (h + 1) * GLA_DK)
            vs = slice(h * GLA_DV, (h + 1) * GLA_DV)
            att = jnp.where(causal, _dot_nt(qe[:, ks], ke[:, ks]), 0.0)
            st = st_ref[s, h]
            o = (jnp.dot(att.astype(jnp.bfloat16), v[:, vs], preferred_element_type=jnp.float32)
                 + _dot_nt(qe[:, ks], st.astype(jnp.bfloat16)))
            st_ref[s, h] = st * a_last[:, ks] + _dot_tn(v[:, vs], kl[:, ks])
            o = o * lax.rsqrt(jnp.mean(o * o, axis=-1, keepdims=True) + RMS_EPS) * gn_ref[...]
            rh = r[:, vs]
            y_ref[s, :, vs] = o * (rh / (1.0 + jnp.exp(-rh)))

    @pl.when(c == pl.num_programs(1) - 1)
    def _():
        sout_ref[...] = st_ref[...]


def _gla(z, s0t, w_a2, b_a, gnorm, B, T, chunk, t_valid):
    nc = T // chunk
    ns = math.gcd(B, GLA_SEQS)
    wa = jnp.zeros((128, GLA_KW), jnp.float32).at[:GLA_GATE_RANK].set(w_a2).astype(jnp.bfloat16)
    z3 = z.reshape(B, T, Z_WIDTH)
    col = lambda width, off: pl.BlockSpec((ns, chunk, width), lambda p, c: (p, c, off // width))
    state = pl.BlockSpec((ns, GLA_HEADS, GLA_DV, GLA_DK), lambda p, c: (p, 0, 0, 0))
    y, st = pl.pallas_call(
        functools.partial(_gla_kernel, chunk=chunk, t_valid=t_valid),
        grid=(B // ns, nc),
        in_specs=[col(GLA_KW, Z_GQ), col(GLA_KW, Z_GK), col(GLA_VW, Z_GV), col(GLA_VW, Z_GR), col(128, Z_GA),
                  pl.BlockSpec((128, GLA_KW), lambda p, c: (0, 0)),
                  pl.BlockSpec((1, GLA_KW), lambda p, c: (0, 0)),
                  pl.BlockSpec((1, GLA_DV), lambda p, c: (0, 0)),
                  state],
        out_specs=[pl.BlockSpec((ns, chunk, GLA_VW), lambda p, c: (p, c, 0)), state],
        out_shape=[jax.ShapeDtypeStruct((B, T, GLA_VW), jnp.float32),
                   jax.ShapeDtypeStruct((B, GLA_HEADS, GLA_DV, GLA_DK), jnp.float32)],
        scratch_shapes=[pltpu.VMEM((ns, GLA_HEADS, GLA_DV, GLA_DK), jnp.float32)],
        compiler_params=_cparams("parallel", "arbitrary"),
        name="gla_scan",
    )(z3, z3, z3, z3, z3, wa, b_a.reshape(1, GLA_KW), gnorm.reshape(1, GLA_DV), s0t)
    return y.reshape(B * T, GLA_VW), st


def _att_kernel(q_ref, kp_ref, vp_ref, kc_ref, vc_ref, bias_ref, o_ref, l_ref, k_ref, v_ref):
    kp_rows, qb = kp_ref.shape[0], q_ref.shape[0]
    k_ref[0:kp_rows, :] = kp_ref[...].astype(jnp.bfloat16)
    k_ref[kp_rows:, :] = kc_ref[...].astype(jnp.bfloat16)
    v_ref[0:kp_rows, :] = vp_ref[...].astype(jnp.bfloat16)
    v_ref[kp_rows:, :] = vc_ref[...].astype(jnp.bfloat16)
    q = (q_ref[...] * (ATT_HEAD_DIM ** -0.5)).astype(jnp.bfloat16)
    for h in range(ATT_HPG):
        sl = slice(h * ATT_HEAD_DIM, (h + 1) * ATT_HEAD_DIM)
        s = _dot_nt(q[:, sl], k_ref[:, sl]) + bias_ref[h]
        m = jnp.max(s, axis=-1, keepdims=True)
        p = jnp.exp(s - m)
        den = jnp.sum(p, axis=-1, keepdims=True)
        o = jnp.dot(p.astype(jnp.bfloat16), v_ref[:, sl], preferred_element_type=jnp.float32)
        o_ref[:, sl] = o / den
        l_ref[:, sl] = jnp.broadcast_to(m + jnp.log(den), (qb, ATT_HEAD_DIM))


def _rel_bucket(dist):
    d = np.asarray(dist, dtype=np.int64)
    max_exact = REL_BUCKETS // 2
    ratio = np.log(np.maximum(d, 1) / max_exact) / np.log(REL_MAX_DIST / max_exact)
    large = np.minimum(max_exact + (ratio * (REL_BUCKETS - max_exact)).astype(np.int64), REL_BUCKETS - 1)
    return np.where(d < max_exact, d, large).astype(np.int32)


def _att_bias(rel_table, gi, qb):
    _, dil = ATT_GROUPS[gi]
    cols = ATT_BAND + qb
    back = np.arange(ATT_BAND, -1, -1)
    tab = rel_table[_rel_bucket(back * dil)][:, gi * ATT_HPG:(gi + 1) * ATT_HPG].T.astype(jnp.float32)
    v = jnp.concatenate([tab, jnp.full((ATT_HPG, cols - ATT_BAND), NEG, jnp.float32)], axis=1)
    return jnp.tile(v, (1, qb))[:, :qb * cols].reshape(ATT_HPG, qb, cols)


def _att_prompt(z, rel_table, gi, B, T):
    _, dil = ATT_GROUPS[gi]
    qb = min(ATT_QBLOCK, T // dil)
    nm = T // dil // qb
    back = qb // ATT_BAND
    if dil == 1:
        zq = zkv = z.reshape(B, T, Z_WIDTH)
        per_q = per_kv = Z_WIDTH // ATT_GW
        cq, ck = Z_Q // ATT_GW + gi, Z_KV // ATT_GW + 2 * gi
    else:
        zq = z[:, Z_Q + gi * ATT_GW:Z_Q + (gi + 1) * ATT_GW].reshape(B, T // dil, dil * ATT_GW)
        zkv = z[:, Z_KV + 2 * gi * ATT_GW:Z_KV + 2 * (gi + 1) * ATT_GW].reshape(B, T // dil, dil * 2 * ATT_GW)
        per_q, per_kv, cq, ck = 1, 2, 0, 0
    cur = lambda per, col: pl.BlockSpec((None, qb, ATT_GW), lambda b, r, m: (b, m, r * per + col))
    prev = lambda per, col: pl.BlockSpec((None, ATT_BAND, ATT_GW),
                                         lambda b, r, m: (b, jnp.maximum(m * back - 1, 0), r * per + col))
    bias = _att_bias(rel_table, gi, qb)
    bias = jnp.stack([bias, bias.at[:, :, :ATT_BAND].set(NEG)])
    out = pl.BlockSpec((None, qb, ATT_GW), lambda b, r, m: (b, m, r))
    shape = jax.ShapeDtypeStruct((B, T // dil, dil * ATT_GW), jnp.float32)
    o, l = pl.pallas_call(
        _att_kernel,
        grid=(B, dil, nm),
        in_specs=[cur(per_q, cq), prev(per_kv, ck), prev(per_kv, ck + 1), cur(per_kv, ck), cur(per_kv, ck + 1),
                  pl.BlockSpec((None,) + bias.shape[1:], lambda b, r, m: (jnp.where(m == 0, 1, 0), 0, 0, 0))],
        out_specs=[out, out],
        out_shape=[shape, shape],
        scratch_shapes=[pltpu.VMEM((ATT_BAND + qb, ATT_GW), jnp.bfloat16),
                        pltpu.VMEM((ATT_BAND + qb, ATT_GW), jnp.bfloat16)],
        compiler_params=_cparams("parallel", "parallel", "parallel"),
        name="att_prompt_g%d" % gi,
    )(zq, zkv, zkv, zkv, zkv, bias)
    return o.reshape(B * T, ATT_GW), l.reshape(B * T, ATT_GW)


def _att_sample_kernel(q_ref, kv_ref, c0_ref, c1_ref, c2_ref, b0_ref, b1_ref, b2_ref, bc_ref, o_ref):
    bf = jnp.bfloat16
    T = q_ref.shape[0]
    rows = ATT_HPG * T
    own = (lax.broadcasted_iota(jnp.int32, (rows, ATT_GW), 0) // T
           == lax.broadcasted_iota(jnp.int32, (rows, ATT_GW), 1) // ATT_HEAD_DIM)
    nums, dens, lses = [], [], []
    for gi, (c_ref, bh_ref) in enumerate(((c0_ref, b0_ref), (c1_ref, b1_ref), (c2_ref, b2_ref))):
        width = c_ref.shape[-1]
        q = q_ref[:, gi * ATT_GW:(gi + 1) * ATT_GW] * (ATT_HEAD_DIM ** -0.5)
        qblk = jnp.where(own, jnp.concatenate([q] * ATT_HPG, axis=0), 0.0).astype(bf)
        kc = kv_ref[:, 2 * gi * ATT_GW:(2 * gi + 1) * ATT_GW].astype(bf)
        vc = kv_ref[:, (2 * gi + 1) * ATT_GW:(2 * gi + 2) * ATT_GW].astype(bf)
        kt = c_ref[0].reshape(ATT_GW, width).astype(bf)
        vt = c_ref[1].reshape(ATT_GW, width).astype(bf)
        s_h = jnp.dot(qblk, kt, preferred_element_type=jnp.float32) + bh_ref[...]
        s_c = _dot_nt(qblk, kc) + bc_ref[gi]
        m = jnp.maximum(jnp.max(s_h, axis=-1, keepdims=True), jnp.max(s_c, axis=-1, keepdims=True))
        p_h = jnp.exp(s_h - m)
        p_c = jnp.exp(s_c - m)
        den = jnp.sum(p_h, axis=-1, keepdims=True) + jnp.sum(p_c, axis=-1, keepdims=True)
        nums.append(_dot_nt(p_h.astype(bf), vt) + jnp.dot(p_c.astype(bf), vc, preferred_element_type=jnp.float32))
        dens.append(den)
        lses.append(m + jnp.log(den))
    lm = jnp.maximum(jnp.maximum(lses[0], lses[1]), lses[2])
    es = [jnp.exp(l - lm) for l in lses]
    tot = es[0] + es[1] + es[2]
    y = sum(nums[g] * (es[g] / (dens[g] * tot)) for g in range(3))
    y = jnp.where(own, y, 0.0)
    o_ref[...] = jnp.sum(y.reshape(ATT_HPG, T, ATT_GW), axis=0)


def _att_sample(z, caches, layer, rel_table, B, T):
    z3 = z.reshape(B, T, Z_WIDTH)
    views, cspecs, bh, bc = [], [], [], []
    tq = np.arange(T)[:, None]
    for gi, (width, dil) in enumerate(ATT_GROUPS):
        cache = caches[gi]
        assert cache.shape[2] == width
        views.append(jnp.transpose(cache, (0, 1, 3, 4, 5, 2)))
        cspecs.append(pl.BlockSpec((None, None, 2, ATT_HPG, ATT_HEAD_DIM, width),
                                   lambda b: (layer, b, 0, 0, 0, 0)))
        steps = np.arange(ATT_BAND + 1)
        tab = rel_table[_rel_bucket(steps * dil)][:, gi * ATT_HPG:(gi + 1) * ATT_HPG].T.astype(jnp.float32)
        for dist, dst in ((width + tq - np.arange(width)[None, :], bh), (tq - np.arange(T)[None, :], bc)):
            ok = (dist >= 0) & (dist % dil == 0) & (dist // dil <= ATT_BAND)
            j = np.where(ok, dist // dil, 0)
            bias = jnp.where(jnp.asarray(ok)[None], tab[:, j], NEG)
            dst.append(bias.reshape(ATT_HPG * T, bias.shape[-1]))
    bc = jnp.stack(bc)
    const = lambda a: pl.BlockSpec(a.shape, lambda b: (0,) * a.ndim, pipeline_mode=pl.Buffered(1))
    y = pl.pallas_call(
        _att_sample_kernel,
        grid=(B,),
        in_specs=[pl.BlockSpec((None, T, 3 * ATT_GW), lambda b: (b, 0, Z_Q // (3 * ATT_GW))),
                  pl.BlockSpec((None, T, 6 * ATT_GW), lambda b: (b, 0, Z_KV // (6 * ATT_GW))),
                  *cspecs, *[const(a) for a in bh], const(bc)],
        out_specs=pl.BlockSpec((None, T, ATT_GW), lambda b: (b, 0, 0)),
        out_shape=jax.ShapeDtypeStruct((B, T, ATT_GW), jnp.float32),
        compiler_params=_cparams("parallel"),
        name="att_sample",
    )(z3, z3, *views, *bh, bc)
    return y.reshape(B * T, ATT_GW)


def _merge_kernel(yp_ref, yg_ref, *refs):
    att_refs, (gz_ref, x_ref, wb_ref, wo_ref, g2_ref, x1_ref, hn_ref) = refs[:-7], refs[-7:]
    if len(att_refs) == 1:
        ya = att_refs[0][...]
    else:
        o0_ref, o1_ref, o2_ref, l0_ref, l1_ref, l2_ref = att_refs
        l0, l1, l2 = l0_ref[...], l1_ref[...], l2_ref[...]
        lm = jnp.maximum(jnp.maximum(l0, l1), l2)
        e0, e1, e2 = jnp.exp(l0 - lm), jnp.exp(l1 - lm), jnp.exp(l2 - lm)
        ya = (e0 * o0_ref[...] + e1 * o1_ref[...] + e2 * o2_ref[...]) / (e0 + e1 + e2)
    acc = None
    for bi, y in enumerate((yp_ref[...], yg_ref[...], ya)):
        proj = jnp.dot(y.astype(jnp.bfloat16), wb_ref[bi], preferred_element_type=jnp.float32)
        gz = gz_ref[:, bi * D_MODEL:(bi + 1) * D_MODEL]
        term = proj / (1.0 + jnp.exp(-gz))
        acc = term if acc is None else acc + term
    x1 = x_ref[...] + jnp.dot(acc.astype(jnp.bfloat16), wo_ref[...], preferred_element_type=jnp.float32)
    x1_ref[...] = x1
    ms = jnp.mean(x1 * x1, axis=-1, keepdims=True)
    hn_ref[...] = (x1 * lax.rsqrt(ms + RMS_EPS) * g2_ref[...]).astype(hn_ref.dtype)


def _merge(yp, yg, att, z, x, w_branch, w_out, norm2_g, tm):
    n = x.shape[0]
    half = pl.BlockSpec((tm, 512), lambda i: (i, 0))
    full = pl.BlockSpec((tm, D_MODEL), lambda i: (i, 0))
    if isinstance(att, list):
        (o0, l0), (o1, l1), (o2, l2) = att
        att = (o0, o1, o2, l0, l1, l2)
    else:
        att = (att,)
    return pl.pallas_call(
        _merge_kernel,
        grid=(n // tm,),
        in_specs=[half] * (2 + len(att)) + [
            pl.BlockSpec((tm, 3 * D_MODEL), lambda i: (i, Z_GZ // (3 * D_MODEL))),
            full,
            pl.BlockSpec((3, 512, D_MODEL), lambda i: (0, 0, 0)),
            pl.BlockSpec((D_MODEL, D_MODEL), lambda i: (0, 0)),
            pl.BlockSpec((1, D_MODEL), lambda i: (0, 0))],
        out_specs=[full, full],
        out_shape=[jax.ShapeDtypeStruct((n, D_MODEL), jnp.float32),
                   jax.ShapeDtypeStruct((n, D_MODEL), jnp.bfloat16)],
        compiler_params=_cparams("parallel"),
        name="branch_merge",
    )(yp, yg, *att, z, x, w_branch.astype(jnp.bfloat16),
      w_out.astype(jnp.bfloat16), norm2_g.reshape(1, D_MODEL))


def _peer_select_kernel(q_ref, k1_ref, k2_ref, c1_ref, n2_ref, a2_ref, r2_ref):
    q = q_ref[...].astype(jnp.bfloat16)
    s1 = _dot_nt(k1_ref[...], q[:, :PEER_HALF])
    s2 = _dot_nt(k2_ref[...], q[:, PEER_HALF:])
    none = float(PEER_TOPK)

    def top(vals, count, ranked):
        out, work = [], vals
        rank = jnp.full(vals.shape, none, jnp.float32) if ranked else None
        for a in range(count):
            m = jnp.max(work, axis=0, keepdims=True)
            out.append(m)
            hit = work >= m
            if ranked:
                rank = jnp.where(hit, float(a), rank)
            work = jnp.where(hit, -jnp.inf, work)
        return out, rank

    v1, rank1 = top(s1, PEER_TOPK, True)
    v2, rank2 = top(s2, PEER_TOPK, True)
    rid = lax.broadcasted_iota(jnp.int32, (PEER_TOPK, s2.shape[1]), 0)
    v2all = jnp.zeros((PEER_TOPK, s2.shape[1]), jnp.float32)
    for b, vb in enumerate(v2):
        v2all = jnp.where(rid == b, vb, v2all)
    cand = jnp.concatenate([v1[0] + v2all] + [v1[a] + v2all[:8] for a in range(1, PEER_TOPK)], axis=0)
    best, _ = top(cand, PEER_TOPK, False)
    tau = best[PEER_TOPK - 1]
    den = jnp.zeros_like(tau)
    n2 = jnp.zeros_like(s1)
    for a in range(PEER_TOPK):
        pair = v1[a] + v2all
        keep = pair >= tau
        den = den + jnp.sum(jnp.where(keep, jnp.exp(pair - best[0]), 0.0), axis=0, keepdims=True)
        n2 = jnp.where(rank1 == float(a), jnp.sum(keep.astype(jnp.float32), axis=0, keepdims=True), n2)
    c1_ref[...] = jnp.where(rank1 < none, 0.5 * jnp.exp(s1 - v1[0]) / den, 0.0)
    n2_ref[...] = n2
    a2_ref[...] = jnp.where(rank2 < none, jnp.exp(s2 - v2[0]), 0.0).astype(a2_ref.dtype)
    r2_ref[...] = rank2.astype(r2_ref.dtype)


def _peer_select(q, k1, k2, tn):
    n = q.shape[0]
    keyspec = pl.BlockSpec((None, PEER_NKEYS, PEER_HALF), lambda i, h: (h, 0, 0))
    out = pl.BlockSpec((None, PEER_NKEYS, tn), lambda i, h: (h, 0, i))
    shape = lambda dt: jax.ShapeDtypeStruct((PEER_HEADS, PEER_NKEYS, n), dt)
    return pl.pallas_call(
        _peer_select_kernel,
        grid=(n // tn, PEER_HEADS),
        in_specs=[pl.BlockSpec((tn, 2 * PEER_HALF), lambda i, h: (i, h)), keyspec, keyspec],
        out_specs=[out] * 4,
        out_shape=[shape(jnp.float32), shape(jnp.float32), shape(jnp.bfloat16), shape(jnp.bfloat16)],
        compiler_params=_cparams("parallel", "parallel"),
        name="peer_select",
    )(q, k1.astype(jnp.bfloat16), k2.astype(jnp.bfloat16))


def _peer_expert_kernel(hn_ref, u_ref, v_ref, c1_ref, n2_ref, a2_ref, r2_ref, o_ref,
                        st_ref, p_ref, a2s_ref, r2s_ref):
    c = pl.program_id(1)

    @pl.when(c == 0)
    def _():
        o_ref[...] = jnp.zeros_like(o_ref)
        a2s_ref[...] = a2_ref[...]
        r2s_ref[...] = r2_ref[...]

    st_ref[...] = _dot_nt(u_ref[...], hn_ref[...])
    tn = st_ref.shape[1]
    sub = 16
    for ii in range(PEER_ROWS):
        for lb in range(tn // 128):
            lanes = slice(lb * 128, (lb + 1) * 128)
            row = lambda ref, h: jnp.broadcast_to(ref[h, ii:ii + 1, lanes], (sub, 128)).astype(jnp.bfloat16)
            n2 = [row(n2_ref, h) for h in range(PEER_HEADS)]
            c1 = [row(c1_ref, h) for h in range(PEER_HEADS)]
            for jb in range(PEER_NKEYS // sub):
                js = slice(jb * sub, (jb + 1) * sub)
                es = slice(ii * PEER_NKEYS + jb * sub, ii * PEER_NKEYS + (jb + 1) * sub)
                s = st_ref[es, lanes]
                act = (s * (1.0 + lax.erf(s * (2.0 ** -0.5)))).astype(jnp.bfloat16)
                w = None
                for h in range(PEER_HEADS):
                    sel = jnp.where(r2s_ref[h, js, lanes] < n2[h], a2s_ref[h, js, lanes], jnp.bfloat16(0)) * c1[h]
                    w = sel if w is None else w + sel
                p_ref[es, lanes] = w * act
    o_ref[...] += _dot_tn(p_ref[...], v_ref[...])


def _peer_experts(hn, u_bf, v_bf, sel, tn):
    n = hn.shape[0]
    ce = PEER_ROWS * PEER_NKEYS
    c1, n2, a2, r2 = sel
    rowblk = pl.BlockSpec((PEER_HEADS, PEER_ROWS, tn), lambda i, c: (0, c, i))
    allblk = pl.BlockSpec((PEER_HEADS, PEER_NKEYS, tn), lambda i, c: (0, 0, i),
                          pipeline_mode=pl.Buffered(1))
    tok = pl.BlockSpec((tn, D_MODEL), lambda i, c: (i, 0))
    table = pl.BlockSpec((ce, D_MODEL), lambda i, c: (c, 0))
    return pl.pallas_call(
        _peer_expert_kernel,
        grid=(n // tn, PEER_EXPERTS // ce),
        in_specs=[tok, table, table, rowblk, rowblk, allblk, allblk],
        out_specs=tok,
        out_shape=jax.ShapeDtypeStruct((n, D_MODEL), jnp.float32),
        scratch_shapes=[pltpu.VMEM((ce, tn), jnp.float32),
                        pltpu.VMEM((ce, tn), jnp.bfloat16),
                        pltpu.VMEM((PEER_HEADS, PEER_NKEYS, tn), jnp.bfloat16),
                        pltpu.VMEM((PEER_HEADS, PEER_NKEYS, tn), jnp.bfloat16)],
        compiler_params=_cparams("parallel", "arbitrary"),
        name="peer_experts",
    )(hn, u_bf, v_bf, c1, n2, a2, r2)


def _arrange_w_in(w_in):
    sizes = (512, 256, 256, 512, 512, 16, 1536, 1536, 1536, 3072)
    offs = np.concatenate([[0], np.cumsum(sizes)])
    wt = w_in.T
    up, gq, gk, gv, gr, ga, aq, ak, av, gz = [wt[offs[i]:offs[i + 1]] for i in range(10)]
    kv = []
    for g in range(3):
        kv += [ak[g * ATT_GW:(g + 1) * ATT_GW], av[g * ATT_GW:(g + 1) * ATT_GW]]
    pad = jnp.zeros((Z_WIDTH - Z_GA - GLA_GATE_RANK, w_in.shape[0]), w_in.dtype)
    return jnp.concatenate([gz] + kv + [aq, up, gv, gr, gq, gk, ga, pad], axis=0).astype(jnp.bfloat16)


def _layer(x, hn, B, T, t_valid, pos0, pool_hist, gla_s0, caches, layer, rel_table, lw, tiles):
    (w_in_r, w_a2, b_a, w_pool, s_pool, gla_norm_g, w_branch, w_out, norm2_g,
     wq_bf, k1, k2, u_bf, v_bf) = lw
    tm, tt, chunk, tn_sel, tn = tiles
    z = _matmul(hn, w_in_r, tm, 2048, "in_proj", b_is_transposed=True)
    hist16 = jnp.pad(pool_hist, ((0, 0), (POOL_HALO - POOL_HIST, 0), (0, 0)))
    y_pool = _pool_mix(z, hist16, w_pool, s_pool, B, T, tt, pos0)
    y_gla, st = _gla(z, jnp.swapaxes(gla_s0, 2, 3), w_a2, b_a, gla_norm_g, B, T, chunk, t_valid)
    if caches is None:
        att = [_att_prompt(z, rel_table, gi, B, T) for gi in range(3)]
    else:
        att = _att_sample(z, caches, layer, rel_table, B, T)
    x1, hn2 = _merge(y_pool, y_gla, att, z, x, w_branch, w_out, norm2_g, tm)
    q = _matmul(hn2, wq_bf, tm, 1024, "peer_query")
    sel = _peer_select(q, k1, k2, tn_sel)
    delta = _peer_experts(hn2, u_bf, v_bf, sel, tn)
    z3 = z.reshape(B, T, Z_WIDTH)
    u_new = z3[:, :t_valid, Z_POOL:Z_POOL + POOL_WIDTH]
    pool_tail = jnp.concatenate([pool_hist, u_new], axis=1)[:, -POOL_HIST:]
    kv_new = [z3[:, :t_valid, Z_KV + 2 * ATT_GW * g:Z_KV + 2 * ATT_GW * (g + 1)]
              .reshape(B, t_valid, 2, ATT_HPG, ATT_HEAD_DIM) for g in range(3)]
    return x1, delta, pool_tail, jnp.swapaxes(st, 2, 3), kv_new


def kernel(x_prompt, x_sample, state_pool, state_gla, cache_att1, cache_att2, cache_att3, rel_table,
           norm1_g, w_in, w_a2, b_a, w_pool, s_pool, gla_norm_g, w_branch, w_out, norm2_g,
           peer_wq, peer_k1, peer_k2, peer_u, peer_v, final_norm_g):
    bp, tp, _ = x_prompt.shape
    bs, ts, _ = x_sample.shape
    depth = w_in.shape[0]
    ts_pad = 8
    caches = (cache_att1, cache_att2, cache_att3)
    xp = x_prompt.reshape(bp * tp, D_MODEL)
    xs = jnp.pad(x_sample, ((0, 0), (0, ts_pad - ts), (0, 0))).reshape(bs * ts_pad, D_MODEL)
    tiles_p = (512, 512, GLA_CHUNK, 512, 1024)
    rows_s = min(256, bs * ts_pad)
    tiles_s = (rows_s, ts_pad, ts_pad, rows_s, rows_s)
    pool_p, pool_s, gla_p, gla_s = [], [], [], []
    att_p = [[] for _ in range(3)]
    att_s = [[] for _ in range(3)]
    hp = _rmsnorm(xp, norm1_g[0], jnp.bfloat16, tiles_p[0])
    hs = _rmsnorm(xs, norm1_g[0], jnp.bfloat16, tiles_s[0])
    for l in range(depth):
        lw = (_arrange_w_in(w_in[l]), w_a2[l], b_a[l], w_pool[l], s_pool[l], gla_norm_g[l], w_branch[l],
              w_out[l], norm2_g[l], peer_wq[l].astype(jnp.bfloat16), peer_k1[l], peer_k2[l],
              peer_u[l].astype(jnp.bfloat16), peer_v[l].astype(jnp.bfloat16))
        xp, dp, pt, gp, kvp = _layer(
            xp, hp, bp, tp, tp, 0, jnp.zeros((bp, POOL_HIST, POOL_WIDTH), jnp.float32),
            jnp.zeros((bp, GLA_HEADS, GLA_DK, GLA_DV), jnp.float32), None, l, rel_table, lw, tiles_p)
        xs, ds, ps, gs, kvs = _layer(
            xs, hs, bs, ts_pad, ts, PAST_LEN, state_pool[l], state_gla[l], caches, l, rel_table, lw, tiles_s)
        last = l == depth - 1
        gain = final_norm_g if last else norm1_g[l + 1]
        xp, hp = _add_rmsnorm(xp, dp, gain, jnp.float32 if last else jnp.bfloat16, tiles_p[0])
        xs, hs = _add_rmsnorm(xs, ds, gain, jnp.float32 if last else jnp.bfloat16, tiles_s[0])
        pool_p.append(pt)
        pool_s.append(ps)
        gla_p.append(gp)
        gla_s.append(gs)
        for g, (w, _) in enumerate(ATT_GROUPS):
            att_p[g].append(kvp[g][:, tp - min(w, tp):])
            att_s[g].append(kvs[g])
    y_prompt = hp.reshape(bp, tp, D_MODEL)
    y_sample = hs.reshape(bs, ts_pad, D_MODEL)[:, :ts]
    outs = [y_prompt, y_sample, jnp.stack(pool_p), jnp.stack(pool_s), jnp.stack(gla_p), jnp.stack(gla_s)]
    for g in range(3):
        outs += [jnp.stack(att_p[g]), jnp.stack(att_s[g])]
    return tuple(outs)
```

```python
import functools
import math

import numpy as np
import jax
import jax.numpy as jnp
from jax import lax
from jax.experimental import pallas as pl
from jax.experimental.pallas import tpu as pltpu

D_MODEL = 1024
RMS_EPS = 1e-6
PAST_LEN = 8192

POOL_WINDOWS = (2, 4, 8, 16)
POOL_GROUP = 128
POOL_WIDTH = 512
POOL_HIST = 15
POOL_HALO = 16

GLA_HEADS = 4
GLA_DK = 64
GLA_DV = 128
GLA_KW = 256
GLA_VW = 512
GLA_GATE_RANK = 16
GLA_TAU = 16.0
GLA_CHUNK = 64
GLA_SEQS = 4

ATT_GROUPS = ((128, 1), (512, 4), (2048, 16))
ATT_HPG = 8
ATT_HEAD_DIM = 64
ATT_GW = ATT_HPG * ATT_HEAD_DIM
ATT_BAND = 128
ATT_QBLOCK = 256
REL_BUCKETS = 32
REL_MAX_DIST = 2048
NEG = -1e30

PEER_HEADS = 8
PEER_NKEYS = 128
PEER_EXPERTS = PEER_NKEYS * PEER_NKEYS
PEER_HALF = 128
PEER_TOPK = 16
PEER_ROWS = 8
PEER_MARK = 2.0 ** 100
PEER_MARK_STEP = 2.0 ** 80

Z_GZ = 0
Z_KV = 3072
Z_Q = 6144
Z_POOL = 7680
Z_GV = 8192
Z_GR = 8704
Z_GQ = 9216
Z_GK = 9472
Z_GA = 9728
Z_WIDTH = 10240

VMEM_LIMIT = 48 * 1024 * 1024


def _cparams(*sem):
    return pltpu.CompilerParams(dimension_semantics=sem, vmem_limit_bytes=VMEM_LIMIT)


def _dot_nt(a, b):
    return lax.dot_general(a, b, (((1,), (1,)), ((), ())), preferred_element_type=jnp.float32)


def _dot_tn(a, b):
    return lax.dot_general(a, b, (((0,), (0,)), ((), ())), preferred_element_type=jnp.float32)


def _rmsnorm_kernel(x_ref, g_ref, o_ref):
    x = x_ref[...]
    ms = jnp.mean(x * x, axis=-1, keepdims=True)
    o_ref[...] = (x * lax.rsqrt(ms + RMS_EPS) * g_ref[...]).astype(o_ref.dtype)


def _rmsnorm(x, g, out_dtype, tm):
    n, d = x.shape
    return pl.pallas_call(
        _rmsnorm_kernel,
        grid=(n // tm,),
        in_specs=[pl.BlockSpec((tm, d), lambda i: (i, 0)),
                  pl.BlockSpec((1, d), lambda i: (0, 0))],
        out_specs=pl.BlockSpec((tm, d), lambda i: (i, 0)),
        out_shape=jax.ShapeDtypeStruct((n, d), out_dtype),
        compiler_params=_cparams("parallel"),
        name="rmsnorm",
    )(x, g.reshape(1, d))


def _add_rmsnorm_kernel(x_ref, d_ref, g_ref, s_ref, o_ref):
    x = x_ref[...] + d_ref[...]
    s_ref[...] = x
    ms = jnp.mean(x * x, axis=-1, keepdims=True)
    o_ref[...] = (x * lax.rsqrt(ms + RMS_EPS) * g_ref[...]).astype(o_ref.dtype)


def _add_rmsnorm(x, delta, g, out_dtype, tm):
    n, d = x.shape
    blk = pl.BlockSpec((tm, d), lambda i: (i, 0))
    return pl.pallas_call(
        _add_rmsnorm_kernel,
        grid=(n // tm,),
        in_specs=[blk, blk, pl.BlockSpec((1, d), lambda i: (0, 0))],
        out_specs=[blk, blk],
        out_shape=[jax.ShapeDtypeStruct((n, d), jnp.float32), jax.ShapeDtypeStruct((n, d), out_dtype)],
        compiler_params=_cparams("parallel"),
        name="add_rmsnorm",
    )(x, delta, g.reshape(1, d))


def _mm_kernel(a_ref, b_ref, o_ref):
    o_ref[...] = jnp.dot(a_ref[...], b_ref[...], preferred_element_type=jnp.float32)


def _mm_nt_kernel(a_ref, b_ref, o_ref):
    o_ref[...] = _dot_nt(a_ref[...], b_ref[...])


def _matmul(a, b, tm, tn, name, b_is_transposed=False):
    m, k = a.shape
    n = b.shape[0] if b_is_transposed else b.shape[1]
    b_spec = (pl.BlockSpec((tn, k), lambda j, i: (j, 0)) if b_is_transposed
              else pl.BlockSpec((k, tn), lambda j, i: (0, j)))
    return pl.pallas_call(
        _mm_nt_kernel if b_is_transposed else _mm_kernel,
        grid=(n // tn, m // tm),
        in_specs=[pl.BlockSpec((tm, k), lambda j, i: (i, 0)), b_spec],
        out_specs=pl.BlockSpec((tm, tn), lambda j, i: (i, j)),
        out_shape=jax.ShapeDtypeStruct((m, n), jnp.float32),
        compiler_params=_cparams("parallel", "parallel"),
        name=name,
    )(a, b)


def _pool_kernel(u_ref, prev_ref, hist_ref, w_ref, s_ref, o_ref, ext_ref, *, tt, pos0):
    t = pl.program_id(1)
    ext_ref[0:POOL_HALO, :] = jnp.where(t == 0, hist_ref[...], prev_ref[...])
    u = u_ref[...]
    ext_ref[POOL_HALO:, :] = u
    pos = (pos0 + t * tt + lax.broadcasted_iota(jnp.int32, (tt, 1), 0)).astype(jnp.float32)
    for gi, w in enumerate(POOL_WINDOWS):
        lo, hi = gi * POOL_GROUP, (gi + 1) * POOL_GROUP
        s = ext_ref[POOL_HALO:POOL_HALO + tt, lo:hi]
        for back in range(1, w):
            s = s + ext_ref[POOL_HALO - back:POOL_HALO - back + tt, lo:hi]
        cnt = jnp.minimum(float(w), pos + 1.0)
        d = s / cnt - u[:, lo:hi]
        y = jnp.dot(d.astype(jnp.bfloat16), w_ref[gi], preferred_element_type=jnp.float32)
        o_ref[:, lo:hi] = y * s_ref[:, lo:hi]


def _pool_mix(z, hist16, w_pool, s_pool, B, T, tt, pos0):
    nt = T // tt
    cb = Z_POOL // POOL_WIDTH

    def prev_map(b, t):
        return (jnp.maximum(b * T + t * tt - POOL_HALO, 0) // POOL_HALO, cb)

    return pl.pallas_call(
        functools.partial(_pool_kernel, tt=tt, pos0=pos0),
        grid=(B, nt),
        in_specs=[pl.BlockSpec((tt, POOL_WIDTH), lambda b, t: (b * nt + t, cb)),
                  pl.BlockSpec((POOL_HALO, POOL_WIDTH), prev_map),
                  pl.BlockSpec((None, POOL_HALO, POOL_WIDTH), lambda b, t: (b, 0, 0)),
                  pl.BlockSpec((4, POOL_GROUP, POOL_GROUP), lambda b, t: (0, 0, 0)),
                  pl.BlockSpec((1, POOL_WIDTH), lambda b, t: (0, 0))],
        out_specs=pl.BlockSpec((tt, POOL_WIDTH), lambda b, t: (b * nt + t, 0)),
        out_shape=jax.ShapeDtypeStruct((B * T, POOL_WIDTH), jnp.float32),
        scratch_shapes=[pltpu.VMEM((POOL_HALO + tt, POOL_WIDTH), jnp.float32)],
        compiler_params=_cparams("parallel", "parallel"),
        name="pool_mix",
    )(z, z, hist16, w_pool.astype(jnp.bfloat16), s_pool.reshape(1, POOL_WIDTH))


def _gla_kernel(q_ref, k_ref, v_ref, r_ref, ga_ref, wa_ref, ba_ref, gn_ref, s0_ref,
                y_ref, sout_ref, st_ref, *, chunk, t_valid):
    c = pl.program_id(1)

    @pl.when(c == 0)
    def _():
        st_ref[...] = s0_ref[...]

    row = c * chunk + lax.broadcasted_iota(jnp.int32, (chunk, 1), 0)
    valid = row < t_valid
    ri = lax.broadcasted_iota(jnp.int32, (chunk, chunk), 0)
    ci = lax.broadcasted_iota(jnp.int32, (chunk, chunk), 1)
    causal = ri >= ci
    for s in range(q_ref.shape[0]):
        x = jnp.dot(ga_ref[s].astype(jnp.bfloat16), wa_ref[...],
                    preferred_element_type=jnp.float32) + ba_ref[...]
        g = -(jnp.maximum(-x, 0.0) + jnp.log1p(jnp.exp(-jnp.abs(x)))) / GLA_TAU
        g = jnp.where(valid, g, 0.0)
        b = jnp.dot(causal.astype(jnp.float32), g, preferred_element_type=jnp.float32,
                    precision=lax.Precision.HIGHEST)
        b_last = b[chunk - 1:chunk, :]
        k = k_ref[s]
        qe = (q_ref[s] * (GLA_DK ** -0.5) * jnp.exp(b)).astype(jnp.bfloat16)
        ke = (k * jnp.exp(-b)).astype(jnp.bfloat16)
        kl = jnp.where(valid, k * jnp.exp(b_last - b), 0.0).astype(jnp.bfloat16)
        a_last = jnp.exp(b_last)
        v = v_ref[s].astype(jnp.bfloat16)
        r = r_ref[s]
        for h in range(GLA_HEADS):
            ks = slice(h * GLA_DK, (h + 1) * GLA_DK)
            vs = slice(h * GLA_DV, (h + 1) * GLA_DV)
            att = jnp.where(causal, _dot_nt(qe[:, ks], ke[:, ks]), 0.0)
            st = st_ref[s, h]
            o = (jnp.dot(att.astype(jnp.bfloat16), v[:, vs], preferred_element_type=jnp.float32)
                 + _dot_nt(qe[:, ks], st.astype(jnp.bfloat16)))
            st_ref[s, h] = st * a_last[:, ks] + _dot_tn(v[:, vs], kl[:, ks])
            o = o * lax.rsqrt(jnp.mean(o * o, axis=-1, keepdims=True) + RMS_EPS) * gn_ref[...]
            rh = r[:, vs]
            y_ref[s, :, vs] = o * (rh / (1.0 + jnp.exp(-rh)))

    @pl.when(c == pl.num_programs(1) - 1)
    def _():
        sout_ref[...] = st_ref[...]


def _gla(z, s0t, w_a2, b_a, gnorm, B, T, chunk, t_valid):
    nc = T // chunk
    ns = math.gcd(B, GLA_SEQS)
    wa = jnp.zeros((128, GLA_KW), jnp.float32).at[:GLA_GATE_RANK].set(w_a2).astype(jnp.bfloat16)
    z3 = z.reshape(B, T, Z_WIDTH)
    col = lambda width, off: pl.BlockSpec((ns, chunk, width), lambda p, c: (p, c, off // width))
    state = pl.BlockSpec((ns, GLA_HEADS, GLA_DV, GLA_DK), lambda p, c: (p, 0, 0, 0))
    y, st = pl.pallas_call(
        functools.partial(_gla_kernel, chunk=chunk, t_valid=t_valid),
        grid=(B // ns, nc),
        in_specs=[col(GLA_KW, Z_GQ), col(GLA_KW, Z_GK), col(GLA_VW, Z_GV), col(GLA_VW, Z_GR), col(128, Z_GA),
                  pl.BlockSpec((128, GLA_KW), lambda p, c: (0, 0)),
                  pl.BlockSpec((1, GLA_KW), lambda p, c: (0, 0)),
                  pl.BlockSpec((1, GLA_DV), lambda p, c: (0, 0)),
                  state],
        out_specs=[pl.BlockSpec((ns, chunk, GLA_VW), lambda p, c: (p, c, 0)), state],
        out_shape=[jax.ShapeDtypeStruct((B, T, GLA_VW), jnp.float32),
                   jax.ShapeDtypeStruct((B, GLA_HEADS, GLA_DV, GLA_DK), jnp.float32)],
        scratch_shapes=[pltpu.VMEM((ns, GLA_HEADS, GLA_DV, GLA_DK), jnp.float32)],
        compiler_params=_cparams("parallel", "arbitrary"),
        name="gla_scan",
    )(z3, z3, z3, z3, z3, wa, b_a.reshape(1, GLA_KW), gnorm.reshape(1, GLA_DV), s0t)
    return y.reshape(B * T, GLA_VW), st


def _att_kernel(q_ref, kp_ref, vp_ref, kc_ref, vc_ref, bias_ref, o_ref, l_ref, k_ref, v_ref):
    kp_rows, qb = kp_ref.shape[0], q_ref.shape[0]
    k_ref[0:kp_rows, :] = kp_ref[...].astype(jnp.bfloat16)
    k_ref[kp_rows:, :] = kc_ref[...].astype(jnp.bfloat16)
    v_ref[0:kp_rows, :] = vp_ref[...].astype(jnp.bfloat16)
    v_ref[kp_rows:, :] = vc_ref[...].astype(jnp.bfloat16)
    q = (q_ref[...] * (ATT_HEAD_DIM ** -0.5)).astype(jnp.bfloat16)
    for h in range(ATT_HPG):
        sl = slice(h * ATT_HEAD_DIM, (h + 1) * ATT_HEAD_DIM)
        s = _dot_nt(q[:, sl], k_ref[:, sl]) + bias_ref[h]
        m = jnp.max(s, axis=-1, keepdims=True)
        p = jnp.exp(s - m)
        den = jnp.sum(p, axis=-1, keepdims=True)
        o = jnp.dot(p.astype(jnp.bfloat16), v_ref[:, sl], preferred_element_type=jnp.float32)
        o_ref[:, sl] = o / den
        l_ref[:, sl] = jnp.broadcast_to(m + jnp.log(den), (qb, ATT_HEAD_DIM))


def _rel_bucket(dist):
    d = np.asarray(dist, dtype=np.int64)
    max_exact = REL_BUCKETS // 2
    ratio = np.log(np.maximum(d, 1) / max_exact) / np.log(REL_MAX_DIST / max_exact)
    large = np.minimum(max_exact + (ratio * (REL_BUCKETS - max_exact)).astype(np.int64), REL_BUCKETS - 1)
    return np.where(d < max_exact, d, large).astype(np.int32)


def _att_bias(rel_table, gi, qb):
    _, dil = ATT_GROUPS[gi]
    cols = ATT_BAND + qb
    back = np.arange(ATT_BAND, -1, -1)
    tab = rel_table[_rel_bucket(back * dil)][:, gi * ATT_HPG:(gi + 1) * ATT_HPG].T.astype(jnp.float32)
    v = jnp.concatenate([tab, jnp.full((ATT_HPG, cols - ATT_BAND), NEG, jnp.float32)], axis=1)
    return jnp.tile(v, (1, qb))[:, :qb * cols].reshape(ATT_HPG, qb, cols)


def _att_prompt(z, rel_table, gi, B, T):
    _, dil = ATT_GROUPS[gi]
    qb = min(ATT_QBLOCK, T // dil)
    nm = T // dil // qb
    back = qb // ATT_BAND
    if dil == 1:
        zq = zkv = z.reshape(B, T, Z_WIDTH)
        per_q = per_kv = Z_WIDTH // ATT_GW
        cq, ck = Z_Q // ATT_GW + gi, Z_KV // ATT_GW + 2 * gi
    else:
        zq = z[:, Z_Q + gi * ATT_GW:Z_Q + (gi + 1) * ATT_GW].reshape(B, T // dil, dil * ATT_GW)
        zkv = z[:, Z_KV + 2 * gi * ATT_GW:Z_KV + 2 * (gi + 1) * ATT_GW].reshape(B, T // dil, dil * 2 * ATT_GW)
        per_q, per_kv, cq, ck = 1, 2, 0, 0
    cur = lambda per, col: pl.BlockSpec((None, qb, ATT_GW), lambda b, r, m: (b, m, r * per + col))
    prev = lambda per, col: pl.BlockSpec((None, ATT_BAND, ATT_GW),
                                         lambda b, r, m: (b, jnp.maximum(m * back - 1, 0), r * per + col))
    bias = _att_bias(rel_table, gi, qb)
    bias = jnp.stack([bias, bias.at[:, :, :ATT_BAND].set(NEG)])
    out = pl.BlockSpec((None, qb, ATT_GW), lambda b, r, m: (b, m, r))
    shape = jax.ShapeDtypeStruct((B, T // dil, dil * ATT_GW), jnp.float32)
    o, l = pl.pallas_call(
        _att_kernel,
        grid=(B, dil, nm),
        in_specs=[cur(per_q, cq), prev(per_kv, ck), prev(per_kv, ck + 1), cur(per_kv, ck), cur(per_kv, ck + 1),
                  pl.BlockSpec((None,) + bias.shape[1:], lambda b, r, m: (jnp.where(m == 0, 1, 0), 0, 0, 0))],
        out_specs=[out, out],
        out_shape=[shape, shape],
        scratch_shapes=[pltpu.VMEM((ATT_BAND + qb, ATT_GW), jnp.bfloat16),
                        pltpu.VMEM((ATT_BAND + qb, ATT_GW), jnp.bfloat16)],
        compiler_params=_cparams("parallel", "parallel", "parallel"),
        name="att_prompt_g%d" % gi,
    )(zq, zkv, zkv, zkv, zkv, bias)
    return o.reshape(B * T, ATT_GW), l.reshape(B * T, ATT_GW)


def _att_sample_kernel(q_ref, kv_ref, c0_ref, c1_ref, c2_ref, b0_ref, b1_ref, b2_ref, bc_ref, o_ref):
    bf = jnp.bfloat16
    T = q_ref.shape[0]
    rows = ATT_HPG * T
    own = (lax.broadcasted_iota(jnp.int32, (rows, ATT_GW), 0) // T
           == lax.broadcasted_iota(jnp.int32, (rows, ATT_GW), 1) // ATT_HEAD_DIM)
    nums, dens, lses = [], [], []
    for gi, (c_ref, bh_ref) in enumerate(((c0_ref, b0_ref), (c1_ref, b1_ref), (c2_ref, b2_ref))):
        width = c_ref.shape[-1]
        q = q_ref[:, gi * ATT_GW:(gi + 1) * ATT_GW] * (ATT_HEAD_DIM ** -0.5)
        qblk = jnp.where(own, jnp.concatenate([q] * ATT_HPG, axis=0), 0.0).astype(bf)
        kc = kv_ref[:, 2 * gi * ATT_GW:(2 * gi + 1) * ATT_GW].astype(bf)
        vc = kv_ref[:, (2 * gi + 1) * ATT_GW:(2 * gi + 2) * ATT_GW].astype(bf)
        kt = c_ref[0].reshape(ATT_GW, width).astype(bf)
        vt = c_ref[1].reshape(ATT_GW, width).astype(bf)
        s_h = jnp.dot(qblk, kt, preferred_element_type=jnp.float32) + bh_ref[...]
        s_c = _dot_nt(qblk, kc) + bc_ref[gi]
        m = jnp.maximum(jnp.max(s_h, axis=-1, keepdims=True), jnp.max(s_c, axis=-1, keepdims=True))
        p_h = jnp.exp(s_h - m)
        p_c = jnp.exp(s_c - m)
        den = jnp.sum(p_h, axis=-1, keepdims=True) + jnp.sum(p_c, axis=-1, keepdims=True)
        nums.append(_dot_nt(p_h.astype(bf), vt) + jnp.dot(p_c.astype(bf), vc, preferred_element_type=jnp.float32))
        dens.append(den)
        lses.append(m + jnp.log(den))
    lm = jnp.maximum(jnp.maximum(lses[0], lses[1]), lses[2])
    es = [jnp.exp(l - lm) for l in lses]
    tot = es[0] + es[1] + es[2]
    y = sum(nums[g] * (es[g] / (dens[g] * tot)) for g in range(3))
    y = jnp.where(own, y, 0.0)
    o_ref[...] = jnp.sum(y.reshape(ATT_HPG, T, ATT_GW), axis=0)


def _att_sample(z, caches, layer, rel_table, B, T):
    z3 = z.reshape(B, T, Z_WIDTH)
    views, cspecs, bh, bc = [], [], [], []
    tq = np.arange(T)[:, None]
    for gi, (width, dil) in enumerate(ATT_GROUPS):
        cache = caches[gi]
        assert cache.shape[2] == width
        views.append(jnp.transpose(cache, (0, 1, 3, 4, 5, 2)))
        cspecs.append(pl.BlockSpec((None, None, 2, ATT_HPG, ATT_HEAD_DIM, width),
                                   lambda b: (layer, b, 0, 0, 0, 0)))
        steps = np.arange(ATT_BAND + 1)
        tab = rel_table[_rel_bucket(steps * dil)][:, gi * ATT_HPG:(gi + 1) * ATT_HPG].T.astype(jnp.float32)
        for dist, dst in ((width + tq - np.arange(width)[None, :], bh), (tq - np.arange(T)[None, :], bc)):
            ok = (dist >= 0) & (dist % dil == 0) & (dist // dil <= ATT_BAND)
            j = np.where(ok, dist // dil, 0)
            bias = jnp.where(jnp.asarray(ok)[None], tab[:, j], NEG)
            dst.append(bias.reshape(ATT_HPG * T, bias.shape[-1]))
    bc = jnp.stack(bc)
    const = lambda a: pl.BlockSpec(a.shape, lambda b: (0,) * a.ndim, pipeline_mode=pl.Buffered(1))
    y = pl.pallas_call(
        _att_sample_kernel,
        grid=(B,),
        in_specs=[pl.BlockSpec((None, T, 3 * ATT_GW), lambda b: (b, 0, Z_Q // (3 * ATT_GW))),
                  pl.BlockSpec((None, T, 6 * ATT_GW), lambda b: (b, 0, Z_KV // (6 * ATT_GW))),
                  *cspecs, *[const(a) for a in bh], const(bc)],
        out_specs=pl.BlockSpec((None, T, ATT_GW), lambda b: (b, 0, 0)),
        out_shape=jax.ShapeDtypeStruct((B, T, ATT_GW), jnp.float32),
        compiler_params=_cparams("parallel"),
        name="att_sample",
    )(z3, z3, *views, *bh, bc)
    return y.reshape(B * T, ATT_GW)


def _merge_kernel(yp_ref, yg_ref, *refs):
    att_refs, (gz_ref, x_ref, wb_ref, wo_ref, g2_ref, x1_ref, hn_ref) = refs[:-7], refs[-7:]
    if len(att_refs) == 1:
        ya = att_refs[0][...]
    else:
        o0_ref, o1_ref, o2_ref, l0_ref, l1_ref, l2_ref = att_refs
        l0, l1, l2 = l0_ref[...], l1_ref[...], l2_ref[...]
        lm = jnp.maximum(jnp.maximum(l0, l1), l2)
        e0, e1, e2 = jnp.exp(l0 - lm), jnp.exp(l1 - lm), jnp.exp(l2 - lm)
        ya = (e0 * o0_ref[...] + e1 * o1_ref[...] + e2 * o2_ref[...]) / (e0 + e1 + e2)
    acc = None
    for bi, y in enumerate((yp_ref[...], yg_ref[...], ya)):
        proj = jnp.dot(y.astype(jnp.bfloat16), wb_ref[bi], preferred_element_type=jnp.float32)
        gz = gz_ref[:, bi * D_MODEL:(bi + 1) * D_MODEL]
        term = proj / (1.0 + jnp.exp(-gz))
        acc = term if acc is None else acc + term
    x1 = x_ref[...] + jnp.dot(acc.astype(jnp.bfloat16), wo_ref[...], preferred_element_type=jnp.float32)
    x1_ref[...] = x1
    ms = jnp.mean(x1 * x1, axis=-1, keepdims=True)
    hn_ref[...] = (x1 * lax.rsqrt(ms + RMS_EPS) * g2_ref[...]).astype(hn_ref.dtype)


def _merge(yp, yg, att, z, x, w_branch, w_out, norm2_g, tm):
    n = x.shape[0]
    half = pl.BlockSpec((tm, 512), lambda i: (i, 0))
    full = pl.BlockSpec((tm, D_MODEL), lambda i: (i, 0))
    if isinstance(att, list):
        (o0, l0), (o1, l1), (o2, l2) = att
        att = (o0, o1, o2, l0, l1, l2)
    else:
        att = (att,)
    return pl.pallas_call(
        _merge_kernel,
        grid=(n // tm,),
        in_specs=[half] * (2 + len(att)) + [
            pl.BlockSpec((tm, 3 * D_MODEL), lambda i: (i, Z_GZ // (3 * D_MODEL))),
            full,
            pl.BlockSpec((3, 512, D_MODEL), lambda i: (0, 0, 0)),
            pl.BlockSpec((D_MODEL, D_MODEL), lambda i: (0, 0)),
            pl.BlockSpec((1, D_MODEL), lambda i: (0, 0))],
        out_specs=[full, full],
        out_shape=[jax.ShapeDtypeStruct((n, D_MODEL), jnp.float32),
                   jax.ShapeDtypeStruct((n, D_MODEL), jnp.bfloat16)],
        compiler_params=_cparams("parallel"),
        name="branch_merge",
    )(yp, yg, *att, z, x, w_branch.astype(jnp.bfloat16),
      w_out.astype(jnp.bfloat16), norm2_g.reshape(1, D_MODEL))


def _peer_select_kernel(q_ref, k1_ref, k2_ref, c1_ref, n2_ref, a2_ref, r2_ref):
    q = q_ref[...].astype(jnp.bfloat16)
    s1 = _dot_nt(k1_ref[...], q[:, :PEER_HALF])
    s2 = _dot_nt(k2_ref[...], q[:, PEER_HALF:])
    none = float(PEER_TOPK)

    def top(vals, count, ranked):
        out, work = [], vals
        for a in range(count):
            m = jnp.max(work, axis=0, keepdims=True)
            out.append(m)
            work = jnp.where(work >= m, -(PEER_MARK + a * PEER_MARK_STEP) if ranked else -jnp.inf, work)
        rank = None
        if ranked:
            rank = jnp.where(work <= -PEER_MARK, work * (-1.0 / PEER_MARK_STEP) - PEER_MARK / PEER_MARK_STEP, none)
        return out, rank

    v1, rank1 = top(s1, PEER_TOPK, True)
    v2, rank2 = top(s2, PEER_TOPK, True)
    rid = lax.broadcasted_iota(jnp.int32, (PEER_TOPK, s2.shape[1]), 0)
    v2all = jnp.zeros((PEER_TOPK, s2.shape[1]), jnp.float32)
    for b, vb in enumerate(v2):
        v2all = jnp.where(rid == b, vb, v2all)
    cand = jnp.concatenate([v1[0] + v2all] + [v1[a] + v2all[:8] for a in range(1, PEER_TOPK)], axis=0)
    best, _ = top(cand, PEER_TOPK, False)
    tau = best[PEER_TOPK - 1]
    den = jnp.zeros_like(tau)
    n2 = jnp.zeros_like(s1)
    for a in range(PEER_TOPK):
        pair = v1[a] + v2all
        keep = pair >= tau
        den = den + jnp.sum(jnp.where(keep, jnp.exp(pair - best[0]), 0.0), axis=0, keepdims=True)
        n2 = jnp.where(rank1 == float(a), jnp.sum(keep.astype(jnp.float32), axis=0, keepdims=True), n2)
    c1_ref[...] = jnp.where(rank1 < none, 0.5 * jnp.exp(s1 - v1[0]) / den, 0.0)
    n2_ref[...] = n2
    a2_ref[...] = jnp.where(rank2 < none, jnp.exp(s2 - v2[0]), 0.0).astype(a2_ref.dtype)
    r2_ref[...] = rank2.astype(r2_ref.dtype)


def _peer_select(q, k1, k2, tn):
    n = q.shape[0]
    keyspec = pl.BlockSpec((None, PEER_NKEYS, PEER_HALF), lambda i, h: (h, 0, 0))
    out = pl.BlockSpec((None, PEER_NKEYS, tn), lambda i, h: (h, 0, i))
    shape = lambda dt: jax.ShapeDtypeStruct((PEER_HEADS, PEER_NKEYS, n), dt)
    return pl.pallas_call(
        _peer_select_kernel,
        grid=(n // tn, PEER_HEADS),
        in_specs=[pl.BlockSpec((tn, 2 * PEER_HALF), lambda i, h: (i, h)), keyspec, keyspec],
        out_specs=[out] * 4,
        out_shape=[shape(jnp.float32), shape(jnp.float32), shape(jnp.bfloat16), shape(jnp.bfloat16)],
        compiler_params=_cparams("parallel", "parallel"),
        name="peer_select",
    )(q, k1.astype(jnp.bfloat16), k2.astype(jnp.bfloat16))


def _peer_expert_kernel(hn_ref, u_ref, v_ref, c1_ref, n2_ref, a2_ref, r2_ref, o_ref,
                        st_ref, p_ref, a2s_ref, r2s_ref):
    c = pl.program_id(1)

    @pl.when(c == 0)
    def _():
        o_ref[...] = jnp.zeros_like(o_ref)
        a2s_ref[...] = a2_ref[...]
        r2s_ref[...] = r2_ref[...]

    st_ref[...] = _dot_nt(u_ref[...], hn_ref[...])
    tn = st_ref.shape[1]
    sub = 16
    for ii in range(PEER_ROWS):
        for lb in range(tn // 128):
            lanes = slice(lb * 128, (lb + 1) * 128)
            row = lambda ref, h: jnp.broadcast_to(ref[h, ii:ii + 1, lanes], (sub, 128)).astype(jnp.bfloat16)
            n2 = [row(n2_ref, h) for h in range(PEER_HEADS)]
            c1 = [row(c1_ref, h) for h in range(PEER_HEADS)]
            for jb in range(PEER_NKEYS // sub):
                js = slice(jb * sub, (jb + 1) * sub)
                es = slice(ii * PEER_NKEYS + jb * sub, ii * PEER_NKEYS + (jb + 1) * sub)
                s = st_ref[es, lanes]
                act = (s * (1.0 + lax.erf(s * (2.0 ** -0.5)))).astype(jnp.bfloat16)
                w = None
                for h in range(PEER_HEADS):
                    sel = jnp.where(r2s_ref[h, js, lanes] < n2[h], a2s_ref[h, js, lanes], jnp.bfloat16(0)) * c1[h]
                    w = sel if w is None else w + sel
                p_ref[es, lanes] = w * act
    o_ref[...] += _dot_tn(p_ref[...], v_ref[...])


def _peer_experts(hn, u_bf, v_bf, sel, tn):
    n = hn.shape[0]
    ce = PEER_ROWS * PEER_NKEYS
    c1, n2, a2, r2 = sel
    rowblk = pl.BlockSpec((PEER_HEADS, PEER_ROWS, tn), lambda i, c: (0, c, i))
    allblk = pl.BlockSpec((PEER_HEADS, PEER_NKEYS, tn), lambda i, c: (0, 0, i),
                          pipeline_mode=pl.Buffered(1))
    tok = pl.BlockSpec((tn, D_MODEL), lambda i, c: (i, 0))
    table = pl.BlockSpec((ce, D_MODEL), lambda i, c: (c, 0))
    return pl.pallas_call(
        _peer_expert_kernel,
        grid=(n // tn, PEER_EXPERTS // ce),
        in_specs=[tok, table, table, rowblk, rowblk, allblk, allblk],
        out_specs=tok,
        out_shape=jax.ShapeDtypeStruct((n, D_MODEL), jnp.float32),
        scratch_shapes=[pltpu.VMEM((ce, tn), jnp.float32),
                        pltpu.VMEM((ce, tn), jnp.bfloat16),
                        pltpu.VMEM((PEER_HEADS, PEER_NKEYS, tn), jnp.bfloat16),
                        pltpu.VMEM((PEER_HEADS, PEER_NKEYS, tn), jnp.bfloat16)],
        compiler_params=_cparams("parallel", "arbitrary"),
        name="peer_experts",
    )(hn, u_bf, v_bf, c1, n2, a2, r2)


def _arrange_w_in(w_in):
    sizes = (512, 256, 256, 512, 512, 16, 1536, 1536, 1536, 3072)
    offs = np.concatenate([[0], np.cumsum(sizes)])
    wt = w_in.T
    up, gq, gk, gv, gr, ga, aq, ak, av, gz = [wt[offs[i]:offs[i + 1]] for i in range(10)]
    kv = []
    for g in range(3):
        kv += [ak[g * ATT_GW:(g + 1) * ATT_GW], av[g * ATT_GW:(g + 1) * ATT_GW]]
    pad = jnp.zeros((Z_WIDTH - Z_GA - GLA_GATE_RANK, w_in.shape[0]), w_in.dtype)
    return jnp.concatenate([gz] + kv + [aq, up, gv, gr, gq, gk, ga, pad], axis=0).astype(jnp.bfloat16)


def _layer(x, hn, B, T, t_valid, pos0, pool_hist, gla_s0, caches, layer, rel_table, lw, tiles):
    (w_in_r, w_a2, b_a, w_pool, s_pool, gla_norm_g, w_branch, w_out, norm2_g,
     wq_bf, k1, k2, u_bf, v_bf) = lw
    tm, tt, chunk, tn_sel, tn = tiles
    z = _matmul(hn, w_in_r, tm, 2048, "in_proj", b_is_transposed=True)
    hist16 = jnp.pad(pool_hist, ((0, 0), (POOL_HALO - POOL_HIST, 0), (0, 0)))
    y_pool = _pool_mix(z, hist16, w_pool, s_pool, B, T, tt, pos0)
    y_gla, st = _gla(z, jnp.swapaxes(gla_s0, 2, 3), w_a2, b_a, gla_norm_g, B, T, chunk, t_valid)
    if caches is None:
        att = [_att_prompt(z, rel_table, gi, B, T) for gi in range(3)]
    else:
        att = _att_sample(z, caches, layer, rel_table, B, T)
    x1, hn2 = _merge(y_pool, y_gla, att, z, x, w_branch, w_out, norm2_g, tm)
    q = _matmul(hn2, wq_bf, tm, 1024, "peer_query")
    sel = _peer_select(q, k1, k2, tn_sel)
    delta = _peer_experts(hn2, u_bf, v_bf, sel, tn)
    z3 = z.reshape(B, T, Z_WIDTH)
    u_new = z3[:, max(t_valid - POOL_HIST, 0):t_valid, Z_POOL:Z_POOL + POOL_WIDTH]
    pool_tail = jnp.concatenate([pool_hist, u_new], axis=1)[:, -POOL_HIST:]
    kv_new = []
    for g, (width, _) in enumerate(ATT_GROUPS):
        rows = min(width, t_valid)
        kv = z3[:, t_valid - rows:t_valid, Z_KV + 2 * ATT_GW * g:Z_KV + 2 * ATT_GW * (g + 1)]
        kv_new.append(kv.reshape(B, rows, 2, ATT_HPG, ATT_HEAD_DIM))
    return x1, delta, pool_tail, jnp.swapaxes(st, 2, 3), kv_new


def kernel(x_prompt, x_sample, state_pool, state_gla, cache_att1, cache_att2, cache_att3, rel_table,
           norm1_g, w_in, w_a2, b_a, w_pool, s_pool, gla_norm_g, w_branch, w_out, norm2_g,
           peer_wq, peer_k1, peer_k2, peer_u, peer_v, final_norm_g):
    bp, tp, _ = x_prompt.shape
    bs, ts, _ = x_sample.shape
    depth = w_in.shape[0]
    ts_pad = 8
    caches = (cache_att1, cache_att2, cache_att3)
    xp = x_prompt.reshape(bp * tp, D_MODEL)
    xs = jnp.pad(x_sample, ((0, 0), (0, ts_pad - ts), (0, 0))).reshape(bs * ts_pad, D_MODEL)
    tiles_p = (512, 512, GLA_CHUNK, 512, 1024)
    rows_s = min(256, bs * ts_pad)
    tiles_s = (rows_s, ts_pad, ts_pad, rows_s, rows_s)
    pool_p, pool_s, gla_p, gla_s = [], [], [], []
    att_p = [[] for _ in range(3)]
    att_s = [[] for _ in range(3)]
    hp = _rmsnorm(xp, norm1_g[0], jnp.bfloat16, tiles_p[0])
    hs = _rmsnorm(xs, norm1_g[0], jnp.bfloat16, tiles_s[0])
    for l in range(depth):
        lw = (_arrange_w_in(w_in[l]), w_a2[l], b_a[l], w_pool[l], s_pool[l], gla_norm_g[l], w_branch[l],
              w_out[l], norm2_g[l], peer_wq[l].astype(jnp.bfloat16), peer_k1[l], peer_k2[l],
              peer_u[l].astype(jnp.bfloat16), peer_v[l].astype(jnp.bfloat16))
        xp, dp, pt, gp, kvp = _layer(
            xp, hp, bp, tp, tp, 0, jnp.zeros((bp, POOL_HIST, POOL_WIDTH), jnp.float32),
            jnp.zeros((bp, GLA_HEADS, GLA_DK, GLA_DV), jnp.float32), None, l, rel_table, lw, tiles_p)
        xs, ds, ps, gs, kvs = _layer(
            xs, hs, bs, ts_pad, ts, PAST_LEN, state_pool[l], state_gla[l], caches, l, rel_table, lw, tiles_s)
        last = l == depth - 1
        gain = final_norm_g if last else norm1_g[l + 1]
        xp, hp = _add_rmsnorm(xp, dp, gain, jnp.float32 if last else jnp.bfloat16, tiles_p[0])
        xs, hs = _add_rmsnorm(xs, ds, gain, jnp.float32 if last else jnp.bfloat16, tiles_s[0])
        pool_p.append(pt)
        pool_s.append(ps)
        gla_p.append(gp)
        gla_s.append(gs)
        for g in range(3):
            att_p[g].append(kvp[g])
            att_s[g].append(kvs[g])
    y_prompt = hp.reshape(bp, tp, D_MODEL)
    y_sample = hs.reshape(bs, ts_pad, D_MODEL)[:, :ts]
    outs = [y_prompt, y_sample, jnp.stack(pool_p), jnp.stack(pool_s), jnp.stack(gla_p), jnp.stack(gla_s)]
    for g in range(3):
        outs += [jnp.stack(att_p[g]), jnp.stack(att_s[g])]
    return tuple(outs)
```

```python
import functools
import math

import numpy as np
import jax
import jax.numpy as jnp
from jax import lax
from jax.experimental import pallas as pl
from jax.experimental.pallas import tpu as pltpu

D_MODEL = 1024
RMS_EPS = 1e-6
PAST_LEN = 8192

POOL_WINDOWS = (2, 4, 8, 16)
POOL_GROUP = 128
POOL_WIDTH = 512
POOL_HIST = 15
POOL_HALO = 16

GLA_HEADS = 4
GLA_DK = 64
GLA_DV = 128
GLA_KW = 256
GLA_VW = 512
GLA_GATE_RANK = 16
GLA_TAU = 16.0
GLA_CHUNK = 64
GLA_SEQS = 4

ATT_GROUPS = ((128, 1), (512, 4), (2048, 16))
ATT_HPG = 8
ATT_HEAD_DIM = 64
ATT_GW = ATT_HPG * ATT_HEAD_DIM
ATT_BAND = 128
ATT_QBLOCK = 256
REL_BUCKETS = 32
REL_MAX_DIST = 2048
NEG = -1e30

PEER_HEADS = 8
PEER_NKEYS = 128
PEER_EXPERTS = PEER_NKEYS * PEER_NKEYS
PEER_HALF = 128
PEER_TOPK = 16
PEER_ROWS = 8
PEER_MARK = 2.0 ** 100
PEER_MARK_STEP = 2.0 ** 80

Z_GZ = 0
Z_KV = 3072
Z_Q = 6144
Z_POOL = 7680
Z_GV = 8192
Z_GR = 8704
Z_GQ = 9216
Z_GK = 9472
Z_GA = 9728
Z_WIDTH = 10240

VMEM_LIMIT = 48 * 1024 * 1024


def _cparams(*sem):
    return pltpu.CompilerParams(dimension_semantics=sem, vmem_limit_bytes=VMEM_LIMIT)


def _dot_nt(a, b):
    return lax.dot_general(a, b, (((1,), (1,)), ((), ())), preferred_element_type=jnp.float32)


def _dot_tn(a, b):
    return lax.dot_general(a, b, (((0,), (0,)), ((), ())), preferred_element_type=jnp.float32)


def _rmsnorm_kernel(x_ref, g_ref, o_ref):
    x = x_ref[...]
    ms = jnp.mean(x * x, axis=-1, keepdims=True)
    o_ref[...] = (x * lax.rsqrt(ms + RMS_EPS) * g_ref[...]).astype(o_ref.dtype)


def _rmsnorm(x, g, out_dtype, tm):
    n, d = x.shape
    return pl.pallas_call(
        _rmsnorm_kernel,
        grid=(n // tm,),
        in_specs=[pl.BlockSpec((tm, d), lambda i: (i, 0)),
                  pl.BlockSpec((1, d), lambda i: (0, 0))],
        out_specs=pl.BlockSpec((tm, d), lambda i: (i, 0)),
        out_shape=jax.ShapeDtypeStruct((n, d), out_dtype),
        compiler_params=_cparams("parallel"),
        name="rmsnorm",
    )(x, g.reshape(1, d))


def _add_rmsnorm_kernel(x_ref, d_ref, g_ref, s_ref, o_ref):
    x = x_ref[...] + d_ref[...]
    s_ref[...] = x
    ms = jnp.mean(x * x, axis=-1, keepdims=True)
    o_ref[...] = (x * lax.rsqrt(ms + RMS_EPS) * g_ref[...]).astype(o_ref.dtype)


def _add_rmsnorm(x, delta, g, out_dtype, tm):
    n, d = x.shape
    blk = pl.BlockSpec((tm, d), lambda i: (i, 0))
    return pl.pallas_call(
        _add_rmsnorm_kernel,
        grid=(n // tm,),
        in_specs=[blk, blk, pl.BlockSpec((1, d), lambda i: (0, 0))],
        out_specs=[blk, blk],
        out_shape=[jax.ShapeDtypeStruct((n, d), jnp.float32), jax.ShapeDtypeStruct((n, d), out_dtype)],
        compiler_params=_cparams("parallel"),
        name="add_rmsnorm",
    )(x, delta, g.reshape(1, d))


def _mm_kernel(a_ref, b_ref, o_ref):
    o_ref[...] = jnp.dot(a_ref[...], b_ref[...], preferred_element_type=jnp.float32)


def _mm_nt_kernel(a_ref, b_ref, o_ref):
    o_ref[...] = _dot_nt(a_ref[...], b_ref[...])


def _matmul(a, b, tm, tn, name, b_is_transposed=False):
    m, k = a.shape
    n = b.shape[0] if b_is_transposed else b.shape[1]
    b_spec = (pl.BlockSpec((tn, k), lambda j, i: (j, 0)) if b_is_transposed
              else pl.BlockSpec((k, tn), lambda j, i: (0, j)))
    return pl.pallas_call(
        _mm_nt_kernel if b_is_transposed else _mm_kernel,
        grid=(n // tn, m // tm),
        in_specs=[pl.BlockSpec((tm, k), lambda j, i: (i, 0)), b_spec],
        out_specs=pl.BlockSpec((tm, tn), lambda j, i: (i, j)),
        out_shape=jax.ShapeDtypeStruct((m, n), jnp.float32),
        compiler_params=_cparams("parallel", "parallel"),
        name=name,
    )(a, b)


def _pool_kernel(u_ref, prev_ref, hist_ref, w_ref, s_ref, o_ref, ext_ref, *, tt, pos0):
    t = pl.program_id(1)
    ext_ref[0:POOL_HALO, :] = jnp.where(t == 0, hist_ref[...], prev_ref[...])
    u = u_ref[...]
    ext_ref[POOL_HALO:, :] = u
    pos = (pos0 + t * tt + lax.broadcasted_iota(jnp.int32, (tt, 1), 0)).astype(jnp.float32)
    for gi, w in enumerate(POOL_WINDOWS):
        lo, hi = gi * POOL_GROUP, (gi + 1) * POOL_GROUP
        s = ext_ref[POOL_HALO:POOL_HALO + tt, lo:hi]
        for back in range(1, w):
            s = s + ext_ref[POOL_HALO - back:POOL_HALO - back + tt, lo:hi]
        cnt = jnp.minimum(float(w), pos + 1.0)
        d = s / cnt - u[:, lo:hi]
        y = jnp.dot(d.astype(jnp.bfloat16), w_ref[gi], preferred_element_type=jnp.float32)
        o_ref[:, lo:hi] = y * s_ref[:, lo:hi]


def _pool_mix(z, hist16, w_pool, s_pool, B, T, tt, pos0):
    nt = T // tt
    cb = Z_POOL // POOL_WIDTH

    def prev_map(b, t):
        return (jnp.maximum(b * T + t * tt - POOL_HALO, 0) // POOL_HALO, cb)

    return pl.pallas_call(
        functools.partial(_pool_kernel, tt=tt, pos0=pos0),
        grid=(B, nt),
        in_specs=[pl.BlockSpec((tt, POOL_WIDTH), lambda b, t: (b * nt + t, cb)),
                  pl.BlockSpec((POOL_HALO, POOL_WIDTH), prev_map),
                  pl.BlockSpec((None, POOL_HALO, POOL_WIDTH), lambda b, t: (b, 0, 0)),
                  pl.BlockSpec((4, POOL_GROUP, POOL_GROUP), lambda b, t: (0, 0, 0)),
                  pl.BlockSpec((1, POOL_WIDTH), lambda b, t: (0, 0))],
        out_specs=pl.BlockSpec((tt, POOL_WIDTH), lambda b, t: (b * nt + t, 0)),
        out_shape=jax.ShapeDtypeStruct((B * T, POOL_WIDTH), jnp.float32),
        scratch_shapes=[pltpu.VMEM((POOL_HALO + tt, POOL_WIDTH), jnp.float32)],
        compiler_params=_cparams("parallel", "parallel"),
        name="pool_mix",
    )(z, z, hist16, w_pool.astype(jnp.bfloat16), s_pool.reshape(1, POOL_WIDTH))


def _gla_kernel(q_ref, k_ref, v_ref, r_ref, ga_ref, wa_ref, ba_ref, gn_ref, s0_ref,
                y_ref, sout_ref, st_ref, *, chunk, t_valid):
    c = pl.program_id(1)

    @pl.when(c == 0)
    def _():
        st_ref[...] = s0_ref[...]

    row = c * chunk + lax.broadcasted_iota(jnp.int32, (chunk, 1), 0)
    valid = row < t_valid
    ri = lax.broadcasted_iota(jnp.int32, (chunk, chunk), 0)
    ci = lax.broadcasted_iota(jnp.int32, (chunk, chunk), 1)
    causal = ri >= ci
    for s in range(q_ref.shape[0]):
        x = jnp.dot(ga_ref[s].astype(jnp.bfloat16), wa_ref[...],
                    preferred_element_type=jnp.float32) + ba_ref[...]
        g = -(jnp.maximum(-x, 0.0) + jnp.log1p(jnp.exp(-jnp.abs(x)))) / GLA_TAU
        g = jnp.where(valid, g, 0.0)
        b = jnp.dot(causal.astype(jnp.float32), g, preferred_element_type=jnp.float32,
                    precision=lax.Precision.HIGHEST)
        b_last = b[chunk - 1:chunk, :]
        k = k_ref[s]
        qe = (q_ref[s] * (GLA_DK ** -0.5) * jnp.exp(b)).astype(jnp.bfloat16)
        ke = (k * jnp.exp(-b)).astype(jnp.bfloat16)
        kl = jnp.where(valid, k * jnp.exp(b_last - b), 0.0).astype(jnp.bfloat16)
        a_last = jnp.exp(b_last)
        v = v_ref[s].astype(jnp.bfloat16)
        r = r_ref[s]
        for h in range(GLA_HEADS):
            ks = slice(h * GLA_DK, (h + 1) * GLA_DK)
            vs = slice(h * GLA_DV, (h + 1) * GLA_DV)
            att = jnp.where(causal, _dot_nt(qe[:, ks], ke[:, ks]), 0.0)
            st = st_ref[s, h]
            o = (jnp.dot(att.astype(jnp.bfloat16), v[:, vs], preferred_element_type=jnp.float32)
                 + _dot_nt(qe[:, ks], st.astype(jnp.bfloat16)))
            st_ref[s, h] = st * a_last[:, ks] + _dot_tn(v[:, vs], kl[:, ks])
            o = o * lax.rsqrt(jnp.mean(o * o, axis=-1, keepdims=True) + RMS_EPS) * gn_ref[...]
            rh = r[:, vs]
            y_ref[s, :, vs] = o * (rh / (1.0 + jnp.exp(-rh)))

    @pl.when(c == pl.num_programs(1) - 1)
    def _():
        sout_ref[...] = st_ref[...]


def _gla(z, s0t, w_a2, b_a, gnorm, B, T, chunk, t_valid):
    nc = T // chunk
    ns = math.gcd(B, GLA_SEQS)
    wa = jnp.zeros((128, GLA_KW), jnp.float32).at[:GLA_GATE_RANK].set(w_a2).astype(jnp.bfloat16)
    z3 = z.reshape(B, T, Z_WIDTH)
    col = lambda width, off: pl.BlockSpec((ns, chunk, width), lambda p, c: (p, c, off // width))
    state = pl.BlockSpec((ns, GLA_HEADS, GLA_DV, GLA_DK), lambda p, c: (p, 0, 0, 0))
    y, st = pl.pallas_call(
        functools.partial(_gla_kernel, chunk=chunk, t_valid=t_valid),
        grid=(B // ns, nc),
        in_specs=[col(GLA_KW, Z_GQ), col(GLA_KW, Z_GK), col(GLA_VW, Z_GV), col(GLA_VW, Z_GR), col(128, Z_GA),
                  pl.BlockSpec((128, GLA_KW), lambda p, c: (0, 0)),
                  pl.BlockSpec((1, GLA_KW), lambda p, c: (0, 0)),
                  pl.BlockSpec((1, GLA_DV), lambda p, c: (0, 0)),
                  state],
        out_specs=[pl.BlockSpec((ns, chunk, GLA_VW), lambda p, c: (p, c, 0)), state],
        out_shape=[jax.ShapeDtypeStruct((B, T, GLA_VW), jnp.float32),
                   jax.ShapeDtypeStruct((B, GLA_HEADS, GLA_DV, GLA_DK), jnp.float32)],
        scratch_shapes=[pltpu.VMEM((ns, GLA_HEADS, GLA_DV, GLA_DK), jnp.float32)],
        compiler_params=_cparams("parallel", "arbitrary"),
        name="gla_scan",
    )(z3, z3, z3, z3, z3, wa, b_a.reshape(1, GLA_KW), gnorm.reshape(1, GLA_DV), s0t)
    return y.reshape(B * T, GLA_VW), st


def _att_kernel(q_ref, kp_ref, vp_ref, kc_ref, vc_ref, bias_ref, o_ref, l_ref, k_ref, v_ref):
    kp_rows, qb = kp_ref.shape[0], q_ref.shape[0]
    k_ref[0:kp_rows, :] = kp_ref[...].astype(jnp.bfloat16)
    k_ref[kp_rows:, :] = kc_ref[...].astype(jnp.bfloat16)
    v_ref[0:kp_rows, :] = vp_ref[...].astype(jnp.bfloat16)
    v_ref[kp_rows:, :] = vc_ref[...].astype(jnp.bfloat16)
    q = (q_ref[...] * (ATT_HEAD_DIM ** -0.5)).astype(jnp.bfloat16)
    for h in range(ATT_HPG):
        sl = slice(h * ATT_HEAD_DIM, (h + 1) * ATT_HEAD_DIM)
        s = _dot_nt(q[:, sl], k_ref[:, sl]) + bias_ref[h]
        m = jnp.max(s, axis=-1, keepdims=True)
        p = jnp.exp(s - m)
        den = jnp.sum(p, axis=-1, keepdims=True)
        o = jnp.dot(p.astype(jnp.bfloat16), v_ref[:, sl], preferred_element_type=jnp.float32)
        o_ref[:, sl] = o / den
        l_ref[:, sl] = jnp.broadcast_to(m + jnp.log(den), (qb, ATT_HEAD_DIM))


def _rel_bucket(dist):
    d = np.asarray(dist, dtype=np.int64)
    max_exact = REL_BUCKETS // 2
    ratio = np.log(np.maximum(d, 1) / max_exact) / np.log(REL_MAX_DIST / max_exact)
    large = np.minimum(max_exact + (ratio * (REL_BUCKETS - max_exact)).astype(np.int64), REL_BUCKETS - 1)
    return np.where(d < max_exact, d, large).astype(np.int32)


def _att_bias(rel_table, gi, qb):
    _, dil = ATT_GROUPS[gi]
    cols = ATT_BAND + qb
    back = np.arange(ATT_BAND, -1, -1)
    tab = rel_table[_rel_bucket(back * dil)][:, gi * ATT_HPG:(gi + 1) * ATT_HPG].T.astype(jnp.float32)
    v = jnp.concatenate([tab, jnp.full((ATT_HPG, cols - ATT_BAND), NEG, jnp.float32)], axis=1)
    return jnp.tile(v, (1, qb))[:, :qb * cols].reshape(ATT_HPG, qb, cols)


def _att_prompt(z, rel_table, gi, B, T):
    _, dil = ATT_GROUPS[gi]
    qb = min(ATT_QBLOCK, T // dil)
    nm = T // dil // qb
    back = qb // ATT_BAND
    if dil == 1:
        zq = zkv = z.reshape(B, T, Z_WIDTH)
        per_q = per_kv = Z_WIDTH // ATT_GW
        cq, ck = Z_Q // ATT_GW + gi, Z_KV // ATT_GW + 2 * gi
    else:
        zq = z[:, Z_Q + gi * ATT_GW:Z_Q + (gi + 1) * ATT_GW].reshape(B, T // dil, dil * ATT_GW)
        zkv = z[:, Z_KV + 2 * gi * ATT_GW:Z_KV + 2 * (gi + 1) * ATT_GW].reshape(B, T // dil, dil * 2 * ATT_GW)
        per_q, per_kv, cq, ck = 1, 2, 0, 0
    cur = lambda per, col: pl.BlockSpec((None, qb, ATT_GW), lambda b, r, m: (b, m, r * per + col))
    prev = lambda per, col: pl.BlockSpec((None, ATT_BAND, ATT_GW),
                                         lambda b, r, m: (b, jnp.maximum(m * back - 1, 0), r * per + col))
    bias = _att_bias(rel_table, gi, qb)
    bias = jnp.stack([bias, bias.at[:, :, :ATT_BAND].set(NEG)])
    out = pl.BlockSpec((None, qb, ATT_GW), lambda b, r, m: (b, m, r))
    shape = jax.ShapeDtypeStruct((B, T // dil, dil * ATT_GW), jnp.float32)
    o, l = pl.pallas_call(
        _att_kernel,
        grid=(B, dil, nm),
        in_specs=[cur(per_q, cq), prev(per_kv, ck), prev(per_kv, ck + 1), cur(per_kv, ck), cur(per_kv, ck + 1),
                  pl.BlockSpec((None,) + bias.shape[1:], lambda b, r, m: (jnp.where(m == 0, 1, 0), 0, 0, 0))],
        out_specs=[out, out],
        out_shape=[shape, shape],
        scratch_shapes=[pltpu.VMEM((ATT_BAND + qb, ATT_GW), jnp.bfloat16),
                        pltpu.VMEM((ATT_BAND + qb, ATT_GW), jnp.bfloat16)],
        compiler_params=_cparams("parallel", "parallel", "parallel"),
        name="att_prompt_g%d" % gi,
    )(zq, zkv, zkv, zkv, zkv, bias)
    return o.reshape(B * T, ATT_GW), l.reshape(B * T, ATT_GW)


def _att_sample_kernel(q_ref, kv_ref, c0_ref, c1_ref, c2_ref, b0_ref, b1_ref, b2_ref, bc_ref, o_ref):
    bf = jnp.bfloat16
    T = q_ref.shape[0]
    rows = ATT_HPG * T
    own = (lax.broadcasted_iota(jnp.int32, (rows, ATT_GW), 0) // T
           == lax.broadcasted_iota(jnp.int32, (rows, ATT_GW), 1) // ATT_HEAD_DIM)
    nums, dens, lses = [], [], []
    for gi, (c_ref, bh_ref) in enumerate(((c0_ref, b0_ref), (c1_ref, b1_ref), (c2_ref, b2_ref))):
        width = c_ref.shape[-1]
        q = q_ref[:, gi * ATT_GW:(gi + 1) * ATT_GW] * (ATT_HEAD_DIM ** -0.5)
        qblk = jnp.where(own, jnp.concatenate([q] * ATT_HPG, axis=0), 0.0).astype(bf)
        kc = kv_ref[:, 2 * gi * ATT_GW:(2 * gi + 1) * ATT_GW].astype(bf)
        vc = kv_ref[:, (2 * gi + 1) * ATT_GW:(2 * gi + 2) * ATT_GW].astype(bf)
        kt = c_ref[0].reshape(ATT_GW, width).astype(bf)
        vt = c_ref[1].reshape(ATT_GW, width).astype(bf)
        s_h = jnp.dot(qblk, kt, preferred_element_type=jnp.float32) + bh_ref[...]
        s_c = _dot_nt(qblk, kc) + bc_ref[gi]
        m = jnp.maximum(jnp.max(s_h, axis=-1, keepdims=True), jnp.max(s_c, axis=-1, keepdims=True))
        p_h = jnp.exp(s_h - m)
        p_c = jnp.exp(s_c - m)
        den = jnp.sum(p_h, axis=-1, keepdims=True) + jnp.sum(p_c, axis=-1, keepdims=True)
        nums.append(_dot_nt(p_h.astype(bf), vt) + jnp.dot(p_c.astype(bf), vc, preferred_element_type=jnp.float32))
        dens.append(den)
        lses.append(m + jnp.log(den))
    lm = jnp.maximum(jnp.maximum(lses[0], lses[1]), lses[2])
    es = [jnp.exp(l - lm) for l in lses]
    tot = es[0] + es[1] + es[2]
    y = sum(nums[g] * (es[g] / (dens[g] * tot)) for g in range(3))
    y = jnp.where(own, y, 0.0)
    o_ref[...] = jnp.sum(y.reshape(ATT_HPG, T, ATT_GW), axis=0)


def _att_sample(z, caches, layer, rel_table, B, T):
    z3 = z.reshape(B, T, Z_WIDTH)
    views, cspecs, bh, bc = [], [], [], []
    tq = np.arange(T)[:, None]
    for gi, (width, dil) in enumerate(ATT_GROUPS):
        cache = caches[gi]
        assert cache.shape[2] == width
        views.append(jnp.transpose(cache, (0, 1, 3, 4, 5, 2)))
        cspecs.append(pl.BlockSpec((None, None, 2, ATT_HPG, ATT_HEAD_DIM, width),
                                   lambda b: (layer, b, 0, 0, 0, 0)))
        steps = np.arange(ATT_BAND + 1)
        tab = rel_table[_rel_bucket(steps * dil)][:, gi * ATT_HPG:(gi + 1) * ATT_HPG].T.astype(jnp.float32)
        for dist, dst in ((width + tq - np.arange(width)[None, :], bh), (tq - np.arange(T)[None, :], bc)):
            ok = (dist >= 0) & (dist % dil == 0) & (dist // dil <= ATT_BAND)
            j = np.where(ok, dist // dil, 0)
            bias = jnp.where(jnp.asarray(ok)[None], tab[:, j], NEG)
            dst.append(bias.reshape(ATT_HPG * T, bias.shape[-1]))
    bc = jnp.stack(bc)
    const = lambda a: pl.BlockSpec(a.shape, lambda b: (0,) * a.ndim, pipeline_mode=pl.Buffered(1))
    y = pl.pallas_call(
        _att_sample_kernel,
        grid=(B,),
        in_specs=[pl.BlockSpec((None, T, 3 * ATT_GW), lambda b: (b, 0, Z_Q // (3 * ATT_GW))),
                  pl.BlockSpec((None, T, 6 * ATT_GW), lambda b: (b, 0, Z_KV // (6 * ATT_GW))),
                  *cspecs, *[const(a) for a in bh], const(bc)],
        out_specs=pl.BlockSpec((None, T, ATT_GW), lambda b: (b, 0, 0)),
        out_shape=jax.ShapeDtypeStruct((B, T, ATT_GW), jnp.float32),
        compiler_params=_cparams("parallel"),
        name="att_sample",
    )(z3, z3, *views, *bh, bc)
    return y.reshape(B * T, ATT_GW)


def _merge_kernel(yp_ref, yg_ref, *refs):
    att_refs, (gz_ref, x_ref, wb_ref, wo_ref, g2_ref, x1_ref, hn_ref) = refs[:-7], refs[-7:]
    if len(att_refs) == 1:
        ya = att_refs[0][...]
    else:
        o0_ref, o1_ref, o2_ref, l0_ref, l1_ref, l2_ref = att_refs
        l0, l1, l2 = l0_ref[...], l1_ref[...], l2_ref[...]
        lm = jnp.maximum(jnp.maximum(l0, l1), l2)
        e0, e1, e2 = jnp.exp(l0 - lm), jnp.exp(l1 - lm), jnp.exp(l2 - lm)
        ya = (e0 * o0_ref[...] + e1 * o1_ref[...] + e2 * o2_ref[...]) / (e0 + e1 + e2)
    acc = None
    for bi, y in enumerate((yp_ref[...], yg_ref[...], ya)):
        proj = jnp.dot(y.astype(jnp.bfloat16), wb_ref[bi], preferred_element_type=jnp.float32)
        gz = gz_ref[:, bi * D_MODEL:(bi + 1) * D_MODEL]
        term = proj / (1.0 + jnp.exp(-gz))
        acc = term if acc is None else acc + term
    x1 = x_ref[...] + jnp.dot(acc.astype(jnp.bfloat16), wo_ref[...], preferred_element_type=jnp.float32)
    x1_ref[...] = x1
    ms = jnp.mean(x1 * x1, axis=-1, keepdims=True)
    hn_ref[...] = (x1 * lax.rsqrt(ms + RMS_EPS) * g2_ref[...]).astype(hn_ref.dtype)


def _merge(yp, yg, att, z, x, w_branch, w_out, norm2_g, tm):
    n = x.shape[0]
    half = pl.BlockSpec((tm, 512), lambda i: (i, 0))
    full = pl.BlockSpec((tm, D_MODEL), lambda i: (i, 0))
    if isinstance(att, list):
        (o0, l0), (o1, l1), (o2, l2) = att
        att = (o0, o1, o2, l0, l1, l2)
    else:
        att = (att,)
    return pl.pallas_call(
        _merge_kernel,
        grid=(n // tm,),
        in_specs=[half] * (2 + len(att)) + [
            pl.BlockSpec((tm, 3 * D_MODEL), lambda i: (i, Z_GZ // (3 * D_MODEL))),
            full,
            pl.BlockSpec((3, 512, D_MODEL), lambda i: (0, 0, 0)),
            pl.BlockSpec((D_MODEL, D_MODEL), lambda i: (0, 0)),
            pl.BlockSpec((1, D_MODEL), lambda i: (0, 0))],
        out_specs=[full, full],
        out_shape=[jax.ShapeDtypeStruct((n, D_MODEL), jnp.float32),
                   jax.ShapeDtypeStruct((n, D_MODEL), jnp.bfloat16)],
        compiler_params=_cparams("parallel"),
        name="branch_merge",
    )(yp, yg, *att, z, x, w_branch.astype(jnp.bfloat16),
      w_out.astype(jnp.bfloat16), norm2_g.reshape(1, D_MODEL))


def _peer_select_kernel(q_ref, k1_ref, k2_ref, c1_ref, n2_ref, a2_ref, r2_ref):
    q = q_ref[...].astype(jnp.bfloat16)
    s1 = _dot_nt(k1_ref[...], q[:, :PEER_HALF])
    s2 = _dot_nt(k2_ref[...], q[:, PEER_HALF:])
    none = float(PEER_TOPK)

    def top(vals, count, ranked):
        out, work = [], vals
        for a in range(count):
            m = jnp.max(work, axis=0, keepdims=True)
            out.append(m)
            work = jnp.where(work >= m, -(PEER_MARK + a * PEER_MARK_STEP) if ranked else -jnp.inf, work)
        rank = None
        if ranked:
            rank = jnp.where(work <= -PEER_MARK, work * (-1.0 / PEER_MARK_STEP) - PEER_MARK / PEER_MARK_STEP, none)
        return out, rank

    v1, rank1 = top(s1, PEER_TOPK, True)
    v2, rank2 = top(s2, PEER_TOPK, True)
    rid = lax.broadcasted_iota(jnp.int32, (PEER_TOPK, s2.shape[1]), 0)
    v2all = jnp.zeros((PEER_TOPK, s2.shape[1]), jnp.float32)
    for b, vb in enumerate(v2):
        v2all = jnp.where(rid == b, vb, v2all)
    cand = jnp.concatenate([v1[0] + v2all] + [v1[a] + v2all[:8] for a in range(1, PEER_TOPK)], axis=0)
    best, _ = top(cand, PEER_TOPK, False)
    tau = best[PEER_TOPK - 1]
    den = jnp.zeros_like(tau)
    n2 = jnp.zeros_like(s1)
    for a in range(PEER_TOPK):
        pair = v1[a] + v2all
        keep = pair >= tau
        den = den + jnp.sum(jnp.where(keep, jnp.exp(pair - best[0]), 0.0), axis=0, keepdims=True)
        n2 = jnp.where(rank1 == float(a), jnp.sum(keep.astype(jnp.float32), axis=0, keepdims=True), n2)
    c1_ref[...] = jnp.where(rank1 < none, 0.5 * jnp.exp(s1 - v1[0]) / den, 0.0)
    n2_ref[...] = n2
    a2_ref[...] = jnp.where(rank2 < none, jnp.exp(s2 - v2[0]), 0.0).astype(a2_ref.dtype)
    r2_ref[...] = rank2.astype(r2_ref.dtype)


def _peer_select(q, k1, k2, tn):
    n = q.shape[0]
    keyspec = pl.BlockSpec((None, PEER_NKEYS, PEER_HALF), lambda i, h: (h, 0, 0))
    out = pl.BlockSpec((None, PEER_NKEYS, tn), lambda i, h: (h, 0, i))
    shape = lambda dt: jax.ShapeDtypeStruct((PEER_HEADS, PEER_NKEYS, n), dt)
    return pl.pallas_call(
        _peer_select_kernel,
        grid=(n // tn, PEER_HEADS),
        in_specs=[pl.BlockSpec((tn, 2 * PEER_HALF), lambda i, h: (i, h)), keyspec, keyspec],
        out_specs=[out] * 4,
        out_shape=[shape(jnp.float32), shape(jnp.float32), shape(jnp.bfloat16), shape(jnp.bfloat16)],
        compiler_params=_cparams("parallel", "parallel"),
        name="peer_select",
    )(q, k1.astype(jnp.bfloat16), k2.astype(jnp.bfloat16))


def _peer_expert_kernel(hn_ref, u_ref, v_ref, c1_ref, n2_ref, a2_ref, r2_ref, o_ref,
                        st_ref, p_ref, a2s_ref, r2s_ref):
    c = pl.program_id(1)

    @pl.when(c == 0)
    def _():
        o_ref[...] = jnp.zeros_like(o_ref)
        a2s_ref[...] = a2_ref[...]
        r2s_ref[...] = r2_ref[...]

    st_ref[...] = _dot_nt(u_ref[...], hn_ref[...])
    tn = st_ref.shape[1]
    sub = 16
    for ii in range(PEER_ROWS):
        for lb in range(tn // 128):
            lanes = slice(lb * 128, (lb + 1) * 128)
            row = lambda ref, h: jnp.broadcast_to(ref[h, ii:ii + 1, lanes], (sub, 128)).astype(jnp.bfloat16)
            n2 = [row(n2_ref, h) for h in range(PEER_HEADS)]
            c1 = [row(c1_ref, h) for h in range(PEER_HEADS)]
            for jb in range(PEER_NKEYS // sub):
                js = slice(jb * sub, (jb + 1) * sub)
                es = slice(ii * PEER_NKEYS + jb * sub, ii * PEER_NKEYS + (jb + 1) * sub)
                s = st_ref[es, lanes]
                act = (s * (1.0 + lax.erf(s * (2.0 ** -0.5)))).astype(jnp.bfloat16)
                w = None
                for h in range(PEER_HEADS):
                    sel = jnp.where(r2s_ref[h, js, lanes] < n2[h], a2s_ref[h, js, lanes], jnp.bfloat16(0)) * c1[h]
                    w = sel if w is None else w + sel
                p_ref[es, lanes] = w * act
    o_ref[...] += _dot_tn(p_ref[...], v_ref[...])


def _peer_experts(hn, u_bf, v_bf, sel, tn):
    n = hn.shape[0]
    ce = PEER_ROWS * PEER_NKEYS
    c1, n2, a2, r2 = sel
    rowblk = pl.BlockSpec((PEER_HEADS, PEER_ROWS, tn), lambda i, c: (0, c, i))
    allblk = pl.BlockSpec((PEER_HEADS, PEER_NKEYS, tn), lambda i, c: (0, 0, i))
    tok = pl.BlockSpec((tn, D_MODEL), lambda i, c: (i, 0))
    table = pl.BlockSpec((ce, D_MODEL), lambda i, c: (c, 0))
    return pl.pallas_call(
        _peer_expert_kernel,
        grid=(n // tn, PEER_EXPERTS // ce),
        in_specs=[tok, table, table, rowblk, rowblk, allblk, allblk],
        out_specs=tok,
        out_shape=jax.ShapeDtypeStruct((n, D_MODEL), jnp.float32),
        scratch_shapes=[pltpu.VMEM((ce, tn), jnp.float32),
                        pltpu.VMEM((ce, tn), jnp.bfloat16),
                        pltpu.VMEM((PEER_HEADS, PEER_NKEYS, tn), jnp.bfloat16),
                        pltpu.VMEM((PEER_HEADS, PEER_NKEYS, tn), jnp.bfloat16)],
        compiler_params=_cparams("parallel", "arbitrary"),
        name="peer_experts",
    )(hn, u_bf, v_bf, c1, n2, a2, r2)


def _arrange_w_in(w_in):
    sizes = (512, 256, 256, 512, 512, 16, 1536, 1536, 1536, 3072)
    offs = np.concatenate([[0], np.cumsum(sizes)])
    wt = w_in.T
    up, gq, gk, gv, gr, ga, aq, ak, av, gz = [wt[offs[i]:offs[i + 1]] for i in range(10)]
    kv = []
    for g in range(3):
        kv += [ak[g * ATT_GW:(g + 1) * ATT_GW], av[g * ATT_GW:(g + 1) * ATT_GW]]
    pad = jnp.zeros((Z_WIDTH - Z_GA - GLA_GATE_RANK, w_in.shape[0]), w_in.dtype)
    return jnp.concatenate([gz] + kv + [aq, up, gv, gr, gq, gk, ga, pad], axis=0).astype(jnp.bfloat16)


def _layer(x, hn, B, T, t_valid, pos0, pool_hist, gla_s0, caches, layer, rel_table, lw, tiles):
    (w_in_r, w_a2, b_a, w_pool, s_pool, gla_norm_g, w_branch, w_out, norm2_g,
     wq_bf, k1, k2, u_bf, v_bf) = lw
    tm, tt, chunk, tn_sel, tn = tiles
    z = _matmul(hn, w_in_r, min(2 * tm, x.shape[0]), 2048, "in_proj", b_is_transposed=True)
    hist16 = jnp.pad(pool_hist, ((0, 0), (POOL_HALO - POOL_HIST, 0), (0, 0)))
    y_pool = _pool_mix(z, hist16, w_pool, s_pool, B, T, tt, pos0)
    y_gla, st = _gla(z, jnp.swapaxes(gla_s0, 2, 3), w_a2, b_a, gla_norm_g, B, T, chunk, t_valid)
    if caches is None:
        att = [_att_prompt(z, rel_table, gi, B, T) for gi in range(3)]
    else:
        att = _att_sample(z, caches, layer, rel_table, B, T)
    x1, hn2 = _merge(y_pool, y_gla, att, z, x, w_branch, w_out, norm2_g, tm)
    q = _matmul(hn2, wq_bf, tm, 1024, "peer_query")
    sel = _peer_select(q, k1, k2, tn_sel)
    delta = _peer_experts(hn2, u_bf, v_bf, sel, tn)
    z3 = z.reshape(B, T, Z_WIDTH)
    u_new = z3[:, max(t_valid - POOL_HIST, 0):t_valid, Z_POOL:Z_POOL + POOL_WIDTH]
    pool_tail = jnp.concatenate([pool_hist, u_new], axis=1)[:, -POOL_HIST:]
    kv_new = []
    for g, (width, _) in enumerate(ATT_GROUPS):
        rows = min(width, t_valid)
        kv = z3[:, t_valid - rows:t_valid, Z_KV + 2 * ATT_GW * g:Z_KV + 2 * ATT_GW * (g + 1)]
        kv_new.append(kv.reshape(B, rows, 2, ATT_HPG, ATT_HEAD_DIM))
    return x1, delta, pool_tail, jnp.swapaxes(st, 2, 3), kv_new


def kernel(x_prompt, x_sample, state_pool, state_gla, cache_att1, cache_att2, cache_att3, rel_table,
           norm1_g, w_in, w_a2, b_a, w_pool, s_pool, gla_norm_g, w_branch, w_out, norm2_g,
           peer_wq, peer_k1, peer_k2, peer_u, peer_v, final_norm_g):
    bp, tp, _ = x_prompt.shape
    bs, ts, _ = x_sample.shape
    depth = w_in.shape[0]
    ts_pad = 8
    caches = (cache_att1, cache_att2, cache_att3)
    xp = x_prompt.reshape(bp * tp, D_MODEL)
    xs = jnp.pad(x_sample, ((0, 0), (0, ts_pad - ts), (0, 0))).reshape(bs * ts_pad, D_MODEL)
    tiles_p = (512, 512, GLA_CHUNK, 512, 1024)
    rows_s = min(256, bs * ts_pad)
    tiles_s = (rows_s, ts_pad, ts_pad, rows_s, rows_s)
    pool_p, pool_s, gla_p, gla_s = [], [], [], []
    att_p = [[] for _ in range(3)]
    att_s = [[] for _ in range(3)]
    hp = _rmsnorm(xp, norm1_g[0], jnp.bfloat16, tiles_p[0])
    hs = _rmsnorm(xs, norm1_g[0], jnp.bfloat16, tiles_s[0])
    for l in range(depth):
        lw = (_arrange_w_in(w_in[l]), w_a2[l], b_a[l], w_pool[l], s_pool[l], gla_norm_g[l], w_branch[l],
              w_out[l], norm2_g[l], peer_wq[l].astype(jnp.bfloat16), peer_k1[l], peer_k2[l],
              peer_u[l].astype(jnp.bfloat16), peer_v[l].astype(jnp.bfloat16))
        xp, dp, pt, gp, kvp = _layer(
            xp, hp, bp, tp, tp, 0, jnp.zeros((bp, POOL_HIST, POOL_WIDTH), jnp.float32),
            jnp.zeros((bp, GLA_HEADS, GLA_DK, GLA_DV), jnp.float32), None, l, rel_table, lw, tiles_p)
        xs, ds, ps, gs, kvs = _layer(
            xs, hs, bs, ts_pad, ts, PAST_LEN, state_pool[l], state_gla[l], caches, l, rel_table, lw, tiles_s)
        last = l == depth - 1
        gain = final_norm_g if last else norm1_g[l + 1]
        xp, hp = _add_rmsnorm(xp, dp, gain, jnp.float32 if last else jnp.bfloat16, tiles_p[0])
        xs, hs = _add_rmsnorm(xs, ds, gain, jnp.float32 if last else jnp.bfloat16, tiles_s[0])
        pool_p.append(pt)
        pool_s.append(ps)
        gla_p.append(gp)
        gla_s.append(gs)
        for g in range(3):
            att_p[g].append(kvp[g])
            att_s[g].append(kvs[g])
    y_prompt = hp.reshape(bp, tp, D_MODEL)
    y_sample = hs.reshape(bs, ts_pad, D_MODEL)[:, :ts]
    outs = [y_prompt, y_sample, jnp.stack(pool_p), jnp.stack(pool_s), jnp.stack(gla_p), jnp.stack(gla_s)]
    for g in range(3):
        outs += [jnp.stack(att_p[g]), jnp.stack(att_s[g])]
    return tuple(outs)
```
